```python
import jax, jax.numpy as jnp
from jax import lax
import numpy as np

D_MODEL = 1024
BATCH = 16
SEQ = 2048
DEPTH = 1

MLA_HEADS = 8
MLA_NOPE_DIM = 64
MLA_ROPE_DIM = 32
MLA_V_DIM = 64
MLA_Q_RANK = 384
MLA_KV_RANK = 256
MLA_QK_DIM = MLA_NOPE_DIM + MLA_ROPE_DIM
ROPE_THETA = 10000.0
Q_BLOCK = 128
MLSTM_HEADS = 4
MLSTM_HEAD_DIM = 128
MLSTM_WIDTH = MLSTM_HEADS * MLSTM_HEAD_DIM
CONV_WIDTH = 4
CHUNK = 64
D_FF = 2816
N_BRANCHES = 2
N_MOD = 9
EPS = 1e-6
IN_SPLITS = (MLA_Q_RANK, MLA_KV_RANK + MLA_ROPE_DIM, MLSTM_WIDTH, MLSTM_WIDTH, MLSTM_WIDTH,
             MLSTM_HEADS, MLSTM_HEADS, N_BRANCHES * D_MODEL)
D_IN = sum(IN_SPLITS)

kernel_name = "hybrid_mla_mlstm_macaron_adaln"


def rms_norm(x, g):
    xf = x.astype(jnp.float32)
    y = xf * lax.rsqrt(jnp.mean(xf * xf, axis=-1, keepdims=True) + EPS)
    return (y * g.astype(jnp.float32)).astype(x.dtype)


def modulate(x, shift, scale):
    return x * (1.0 + scale[:, None, :]) + shift[:, None, :]


def swiglu(x, w_gate, w_up, w_down):
    return (jax.nn.silu(x @ w_gate) * (x @ w_up)) @ w_down


def rope_angles(positions, dim):
    inv_freq = ROPE_THETA ** (-jnp.arange(0, dim, 2, dtype=jnp.float32) / dim)
    ang = positions.astype(jnp.float32)[..., None] * inv_freq
    return jnp.cos(ang), jnp.sin(ang)


def apply_rope(x, cos, sin):
    xf = x.astype(jnp.float32)
    x1, x2 = jnp.split(xf, 2, axis=-1)
    return jnp.concatenate([x1 * cos - x2 * sin, x2 * cos + x1 * sin], axis=-1).astype(x.dtype)


def mla_attention(q_lat, kv_lat, positions, q_a_norm, w_q_b, kv_a_norm, w_kv_b):
    B, S, _ = q_lat.shape
    H = MLA_HEADS
    q = (rms_norm(q_lat, q_a_norm) @ w_q_b).reshape(B, S, H, MLA_QK_DIM)
    q_nope, q_pe = q[..., :MLA_NOPE_DIM], q[..., MLA_NOPE_DIM:]
    c_kv, k_pe = kv_lat[..., :MLA_KV_RANK], kv_lat[..., MLA_KV_RANK:]
    kv = (rms_norm(c_kv, kv_a_norm) @ w_kv_b).reshape(B, S, H, MLA_NOPE_DIM + MLA_V_DIM)
    k_nope, v = kv[..., :MLA_NOPE_DIM], kv[..., MLA_NOPE_DIM:]
    cos, sin = rope_angles(positions, MLA_ROPE_DIM)
    q_pe = apply_rope(q_pe, cos[:, :, None, :], sin[:, :, None, :])
    k_pe = apply_rope(k_pe, cos, sin)
    k = jnp.concatenate([k_nope, jnp.broadcast_to(k_pe[:, :, None, :], (B, S, H, MLA_ROPE_DIM))], axis=-1)
    q = jnp.concatenate([q_nope, q_pe], axis=-1) * (MLA_QK_DIM ** -0.5)
    nb = S // Q_BLOCK
    qb = q.reshape(B, nb, Q_BLOCK, H, MLA_QK_DIM).transpose(1, 0, 3, 2, 4)
    k_pos = jnp.arange(S)

    def block(args):
        q_blk, blk = args
        s = jnp.einsum('bhqd,bkhd->bhqk', q_blk, k).astype(jnp.float32)
        q_pos = blk * Q_BLOCK + jnp.arange(Q_BLOCK)
        mask = k_pos[None, :] <= q_pos[:, None]
        s = jnp.where(mask, s, jnp.finfo(jnp.float32).min)
        p = jax.nn.softmax(s, axis=-1).astype(v.dtype)
        return jnp.einsum('bhqk,bkhd->bqhd', p, v)

    o = lax.map(block, (qb, jnp.arange(nb)))
    return o.transpose(1, 0, 2, 3, 4).reshape(B, S, H * MLA_V_DIM)


def causal_depthwise_conv(x, w, b):
    y = lax.conv_general_dilated(x, w.astype(x.dtype)[:, None, :], window_strides=(1,),
                                 padding=[(CONV_WIDTH - 1, 0)], dimension_numbers=('NWC', 'WIO', 'NWC'),
                                 feature_group_count=x.shape[-1])
    return y + b


def mlstm(x_m, v_in, o_pre, i_pre, f_pre, conv_w, conv_b, w_q_m, w_k_m, b_i, b_f, head_norm):
    B, S, _ = x_m.shape
    H, dh, L = MLSTM_HEADS, MLSTM_HEAD_DIM, CHUNK
    nc = S // L
    xc = jax.nn.silu(causal_depthwise_conv(x_m, conv_w, conv_b)).reshape(B, S, H, dh)
    q = jnp.einsum('bshd,hde->bhse', xc, w_q_m).astype(jnp.float32)
    k = jnp.einsum('bshd,hde->bhse', xc, w_k_m).astype(jnp.float32) * (dh ** -0.5)
    v = v_in.reshape(B, S, H, dh).transpose(0, 2, 1, 3).astype(jnp.float32)
    log_i = (i_pre + b_i).astype(jnp.float32).transpose(0, 2, 1)
    log_f = jax.nn.log_sigmoid((f_pre + b_f).astype(jnp.float32)).transpose(0, 2, 1)
    q = q.reshape(B, H, nc, L, dh)
    k = k.reshape(B, H, nc, L, dh)
    v = v.reshape(B, H, nc, L, dh)
    log_i = log_i.reshape(B, H, nc, L)
    bcum = jnp.cumsum(log_f.reshape(B, H, nc, L), axis=-1)
    g = bcum[..., -1]
    causal = jnp.tril(jnp.ones((L, L), dtype=bool))
    d_log = bcum[..., :, None] - bcum[..., None, :] + log_i[..., None, :]
    d_log = jnp.where(causal, d_log, -jnp.inf)
    w_state = g[..., None] - bcum + log_i
    m_loc = jnp.max(w_state, axis=-1)
    e_state = jnp.exp(w_state - m_loc[..., None])
    c_loc = jnp.einsum('bhcl,bhcld,bhcle->bhcde', e_state, v, k)
    n_loc = jnp.einsum('bhcl,bhcle->bhce', e_state, k)

    def step(carry, inp):
        C, n, m = carry
        c_l, n_l, m_l, g_c = inp
        m_new = jnp.maximum(g_c + m, m_l)
        a = jnp.exp(g_c + m - m_new)
        bb = jnp.exp(m_l - m_new)
        C_new = a[..., None, None] * C + bb[..., None, None] * c_l
        n_new = a[..., None] * n + bb[..., None] * n_l
        return (C_new, n_new, m_new), (C, n, m)

    init = (jnp.zeros((B, H, dh, dh), jnp.float32), jnp.zeros((B, H, dh), jnp.float32),
            jnp.zeros((B, H), jnp.float32))
    xs = (c_loc.transpose(2, 0, 1, 3, 4), n_loc.transpose(2, 0, 1, 3), m_loc.transpose(2, 0, 1), g.transpose(2, 0, 1))
    _, (C_prev, n_prev, m_prev) = lax.scan(step, init, xs)
    C_prev = C_prev.transpose(1, 2, 0, 3, 4)
    n_prev = n_prev.transpose(1, 2, 0, 3)
    m_prev = m_prev.transpose(1, 2, 0)
    inter_log = bcum + m_prev[..., None]
    m_t = jnp.maximum(inter_log, jnp.max(d_log, axis=-1))
    inter_w = jnp.exp(inter_log - m_t)
    s_mat = jnp.exp(d_log - m_t[..., None]) * jnp.einsum('bhcte,bhcse->bhcts', q, k)
    num = inter_w[..., None] * jnp.einsum('bhcde,bhcte->bhctd', C_prev, q) + jnp.einsum('bhcts,bhcsd->bhctd', s_mat, v)
    den = inter_w * jnp.einsum('bhce,bhcte->bhct', n_prev, q) + jnp.sum(s_mat, axis=-1)
    h = num / jnp.maximum(jnp.abs(den), jnp.exp(-m_t))[..., None]
    h = h.reshape(B, H, S, dh).transpose(0, 2, 1, 3)
    h = jax.nn.sigmoid(o_pre.astype(jnp.float32)).reshape(B, S, H, dh) * h
    h = h * lax.rsqrt(jnp.mean(h * h, axis=-1, keepdims=True) + EPS) * head_norm.astype(jnp.float32)
    return h.reshape(B, S, H * dh).astype(x_m.dtype)


def setup_inputs(seed: int = 0) -> dict:
    key = jax.random.key(seed)
    ks = iter(jax.random.split(key, 48))

    def normal(shape, s):
        return jax.random.normal(next(ks), shape, jnp.float32) * s

    def dense(shape, extra=1.0):
        return normal(shape, extra * shape[-2] ** -0.5)

    def gain(shape):
        return 1.0 + normal(shape, 0.05)

    Dp = DEPTH
    x = normal((BATCH, SEQ, D_MODEL), 1.0)
    c = normal((BATCH, D_MODEL), 1.0)
    offs = jax.random.randint(next(ks), (BATCH, 1), 0, 4096, dtype=jnp.int32)
    positions = (jnp.arange(SEQ, dtype=jnp.int32)[None, :] + offs).astype(jnp.int32)
    b_f = jnp.linspace(3.0, 6.0, MLSTM_HEADS, dtype=jnp.float32)[None, :] + normal((Dp, MLSTM_HEADS), 0.1)
    return {
        "x": x, "c": c, "positions": positions,
        "w_ada": dense((Dp, D_MODEL, N_MOD * D_MODEL), 0.5),
        "b_ada": normal((Dp, N_MOD * D_MODEL), 0.02),
        "norm_ff1": gain((Dp, D_MODEL)),
        "ff1_w_gate": dense((Dp, D_MODEL, D_FF)),
        "ff1_w_up": dense((Dp, D_MODEL, D_FF)),
        "ff1_w_down": dense((Dp, D_FF, D_MODEL)),
        "norm_mix": gain((Dp, D_MODEL)),
        "w_in": dense((Dp, D_MODEL, D_IN)),
        "q_a_norm": gain((Dp, MLA_Q_RANK)),
        "w_q_b": dense((Dp, MLA_Q_RANK, MLA_HEADS * MLA_QK_DIM)),
        "kv_a_norm": gain((Dp, MLA_KV_RANK)),
        "w_kv_b": dense((Dp, MLA_KV_RANK, MLA_HEADS * (MLA_NOPE_DIM + MLA_V_DIM))),
        "conv_w": dense((Dp, CONV_WIDTH, MLSTM_WIDTH)),
        "conv_b": normal((Dp, MLSTM_WIDTH), 0.02),
        "w_q_m": dense((Dp, MLSTM_HEADS, MLSTM_HEAD_DIM, MLSTM_HEAD_DIM)),
        "w_k_m": dense((Dp, MLSTM_HEADS, MLSTM_HEAD_DIM, MLSTM_HEAD_DIM)),
        "b_i": normal((Dp, MLSTM_HEADS), 0.1),
        "b_f": b_f,
        "mlstm_norm": gain((Dp, MLSTM_HEADS, MLSTM_HEAD_DIM)),
        "w_mla_out": dense((Dp, MLA_HEADS * MLA_V_DIM, D_MODEL)),
        "w_mlstm_out": dense((Dp, MLSTM_WIDTH, D_MODEL)),
        "w_o": dense((Dp, D_MODEL, D_MODEL)),
        "norm_ff2": gain((Dp, D_MODEL)),
        "ff2_w_gate": dense((Dp, D_MODEL, D_FF)),
        "ff2_w_up": dense((Dp, D_MODEL, D_FF)),
        "ff2_w_down": dense((Dp, D_FF, D_MODEL)),
        "norm_final": gain((D_MODEL,)),
    }


def reference(x, c, positions, w_ada, b_ada, norm_ff1, ff1_w_gate, ff1_w_up, ff1_w_down,
              norm_mix, w_in, q_a_norm, w_q_b, kv_a_norm, w_kv_b, conv_w, conv_b, w_q_m, w_k_m,
              b_i, b_f, mlstm_norm, w_mla_out, w_mlstm_out, w_o, norm_ff2, ff2_w_gate, ff2_w_up,
              ff2_w_down, norm_final):
    B, S, D = x.shape
    split_idx = np.cumsum(IN_SPLITS)[:-1].tolist()
    silu_c = jax.nn.silu(c)
    h = x
    for l in range(DEPTH):
        mod = silu_c @ w_ada[l] + b_ada[l]
        sh1, sc1, gt1, sh2, sc2, gt2, sh3, sc3, gt3 = jnp.split(mod, N_MOD, axis=-1)
        u = modulate(rms_norm(h, norm_ff1[l]), sh1, sc1)
        h = h + 0.5 * gt1[:, None, :] * swiglu(u, ff1_w_gate[l], ff1_w_up[l], ff1_w_down[l])
        u = modulate(rms_norm(h, norm_mix[l]), sh2, sc2)
        proj = u @ w_in[l]
        q_lat, kv_lat, x_m, v_m, o_m, i_m, f_m, gate_pre = jnp.split(proj, split_idx, axis=-1)
        y_a = mla_attention(q_lat, kv_lat, positions, q_a_norm[l], w_q_b[l], kv_a_norm[l], w_kv_b[l]) @ w_mla_out[l]
        y_b = mlstm(x_m, v_m, o_m, i_m, f_m, conv_w[l], conv_b[l], w_q_m[l], w_k_m[l], b_i[l], b_f[l],
                    mlstm_norm[l]) @ w_mlstm_out[l]
        gates = jax.nn.sigmoid(gate_pre).reshape(B, S, N_BRANCHES, D)
        y = gates[:, :, 0, :] * y_a + gates[:, :, 1, :] * y_b
        h = h + gt2[:, None, :] * (y @ w_o[l])
        u = modulate(rms_norm(h, norm_ff2[l]), sh3, sc3)
        h = h + 0.5 * gt3[:, None, :] * swiglu(u, ff2_w_gate[l], ff2_w_up[l], ff2_w_down[l])
    return rms_norm(h, norm_final)
```

```python
import functools

import jax
import jax.numpy as jnp
from jax import lax
from jax.experimental import pallas as pl
from jax.experimental.pallas import tpu as pltpu

F32 = jnp.float32
BF16 = jnp.bfloat16

V7X_LANES = 128
V7X_VMEM_BYTES = 64 * 1024 * 1024
VMEM_LIMIT_BYTES = V7X_VMEM_BYTES - 8 * 1024 * 1024

MLA_HEADS = 8
MLA_NOPE_DIM = 64
MLA_ROPE_DIM = 32
MLA_V_DIM = 64
MLA_QK_DIM = MLA_NOPE_DIM + MLA_ROPE_DIM
MLA_Q_RANK = 384
MLA_KV_RANK = 256
ROPE_THETA = 10000.0
MLSTM_HEADS = 4
MLSTM_HEAD_DIM = 128
MLSTM_WIDTH = MLSTM_HEADS * MLSTM_HEAD_DIM
CONV_WIDTH = 4
N_BRANCHES = 2
N_MOD = 9
EPS = 1e-6

HEAD_PAD = V7X_LANES
ROPE_LO = MLA_NOPE_DIM
ROPE_HALF = MLA_ROPE_DIM // 2

FFN_TM = 512
FFN_TF = 256
MIX_TM = 512
ATTN_TQ = 256
MLSTM_CHUNK = 128


def _dot(a, b):
    return jnp.dot(a, b, preferred_element_type=F32)


def _rms(x, gain):
    return x * lax.rsqrt(jnp.mean(x * x, axis=-1, keepdims=True) + EPS) * gain


def _params(n_axes):
    return pltpu.CompilerParams(
        dimension_semantics=("arbitrary",) * n_axes, vmem_limit_bytes=VMEM_LIMIT_BYTES)


def _resident(shape):
    zeros = (0,) * len(shape)
    return pl.BlockSpec(shape, lambda *_: zeros, pipeline_mode=pl.Buffered(1))


def _mod_spec(chunk, tm, seq, d):
    return pl.BlockSpec((None, 1, d), lambda i: ((i * tm) // seq, 0, chunk))


def _adaln_kernel(c_ref, w_ref, b_ref, o_ref):
    c = c_ref[...]
    sc = (c * jax.nn.sigmoid(c)).astype(BF16)
    o_ref[...] = _dot(sc, w_ref[...].astype(BF16)) + b_ref[...]


def _adaln(c, w_ada, b_ada):
    bsz, d = c.shape
    n = w_ada.shape[1]
    tn = d
    return pl.pallas_call(
        _adaln_kernel,
        grid=(n // tn,),
        in_specs=[pl.BlockSpec((bsz, d), lambda j: (0, 0)),
                  pl.BlockSpec((d, tn), lambda j: (0, j)),
                  pl.BlockSpec((1, tn), lambda j: (0, j))],
        out_specs=pl.BlockSpec((bsz, tn), lambda j: (0, j)),
        out_shape=jax.ShapeDtypeStruct((bsz, n), F32),
        compiler_params=_params(1),
        name="adaln",
    )(c, w_ada, b_ada.reshape(1, n))


def _ffn_kernel(*refs, d_ff, tf, final_norm):
    if final_norm:
        (x_ref, g_ref, sh_ref, sc_ref, gt_ref, wg_ref, wu_ref, wd_ref, gf_ref,
         o_ref, u_scr, a_scr) = refs
    else:
        (x_ref, g_ref, sh_ref, sc_ref, gt_ref, wg_ref, wu_ref, wd_ref,
         o_ref, u_scr, a_scr) = refs
    x = x_ref[...]
    u = _rms(x, g_ref[...]) * (1.0 + sc_ref[...]) + sh_ref[...]
    u_scr[...] = u.astype(BF16)
    for j in range(d_ff // tf):
        cols = slice(j * tf, (j + 1) * tf)
        g = _dot(u_scr[...], wg_ref[:, cols])
        up = _dot(u_scr[...], wu_ref[:, cols])
        a_scr[:, cols] = (g * jax.nn.sigmoid(g) * up).astype(BF16)
    down = _dot(a_scr[...], wd_ref[...])
    out = x + (0.5 * gt_ref[...]) * down
    if final_norm:
        out = _rms(out, gf_ref[...])
    o_ref[...] = out


def _ffn(h, mod3, chunk0, norm_g, wg, wu, wd, seq, final_g=None):
    t, d = h.shape
    d_ff = wg.shape[1]
    tm = FFN_TM
    row = lambda i: (i, 0)
    in_specs = [pl.BlockSpec((tm, d), row), _resident((1, d)),
                _mod_spec(chunk0, tm, seq, d), _mod_spec(chunk0 + 1, tm, seq, d),
                _mod_spec(chunk0 + 2, tm, seq, d),
                _resident((d, d_ff)), _resident((d, d_ff)), _resident((d_ff, d))]
    args = [h, norm_g.reshape(1, d), mod3, mod3, mod3, wg, wu, wd]
    if final_g is not None:
        in_specs.append(_resident((1, d)))
        args.append(final_g.reshape(1, d))
    return pl.pallas_call(
        functools.partial(_ffn_kernel, d_ff=d_ff, tf=FFN_TF, final_norm=final_g is not None),
        grid=(t // tm,),
        in_specs=in_specs,
        out_specs=pl.BlockSpec((tm, d), row),
        out_shape=jax.ShapeDtypeStruct((t, d), F32),
        scratch_shapes=[pltpu.VMEM((tm, d), BF16), pltpu.VMEM((tm, d_ff), BF16)],
        compiler_params=_params(1),
        name="ffn_final" if final_g is not None else "ffn",
    )(*args)


_C_QLAT = 0
_C_CKV = _C_QLAT + MLA_Q_RANK
_C_KPE = _C_CKV + MLA_KV_RANK
_C_XM = _C_KPE + HEAD_PAD
_C_VM = _C_XM + MLSTM_WIDTH
_C_OM = _C_VM + MLSTM_WIDTH
_C_IF = _C_OM + MLSTM_WIDTH
_C_GATE = _C_IF + V7X_LANES


def _rope_tables(pos_f32):
    lane = lax.broadcasted_iota(jnp.int32, (1, HEAD_PAD), 1)
    in_lo = (lane >= ROPE_LO) & (lane < ROPE_LO + ROPE_HALF)
    in_hi = (lane >= ROPE_LO + ROPE_HALF) & (lane < ROPE_LO + MLA_ROPE_DIM)
    idx = jnp.where(in_hi, lane - (ROPE_LO + ROPE_HALF), lane - ROPE_LO)
    idx = jnp.where(in_lo | in_hi, idx, 0).astype(F32)
    theta = jnp.full((1, HEAD_PAD), ROPE_THETA, F32)
    inv_freq = jnp.exp(-(idx / ROPE_HALF) * jnp.log(theta))
    ang = pos_f32 * inv_freq
    cos = jnp.cos(ang)
    sin = jnp.sin(ang)
    cos_m = jnp.where(lane < ROPE_LO, 1.0, jnp.where(in_lo | in_hi, cos, 0.0))
    sin_up = jnp.where(in_lo, -sin, 0.0)
    sin_dn = jnp.where(in_hi, sin, 0.0)
    return cos_m, sin_up, sin_dn


def _rope_group(x, tables):
    cos_m, sin_up, sin_dn = tables
    up = pltpu.roll(x, HEAD_PAD - ROPE_HALF, axis=1)
    dn = pltpu.roll(x, ROPE_HALF, axis=1)
    return x * cos_m + up * sin_up + dn * sin_dn


def _mixin_kernel(h_ref, pos_ref, g_ref, sh_ref, sc_ref, win_ref, qan_ref, wq_ref, kvn_ref,
                  wk_ref, wv_ref, bif_ref,
                  q_out, k_out, v_out, xm_out, vm_out, om_out, if_out, gate_out, u_scr):
    x = h_ref[...]
    u_scr[...] = (_rms(x, g_ref[...]) * (1.0 + sc_ref[...]) + sh_ref[...]).astype(BF16)

    def proj(lo, width):
        return _dot(u_scr[...], win_ref[:, lo:lo + width])

    tables = _rope_tables(pos_ref[...].astype(F32))

    q_lat = proj(_C_QLAT, MLA_Q_RANK)
    q = _dot(_rms(q_lat, qan_ref[...]).astype(BF16), wq_ref[...])
    q_scale = MLA_QK_DIM ** -0.5
    for hd in range(MLA_HEADS):
        grp = slice(hd * HEAD_PAD, (hd + 1) * HEAD_PAD)
        q_out[:, grp] = (_rope_group(q[:, grp], tables) * q_scale).astype(BF16)

    c_kv = proj(_C_CKV, MLA_KV_RANK)
    c_n = _rms(c_kv, kvn_ref[...]).astype(BF16)
    k_nope = _dot(c_n, wk_ref[...])
    k_pe = _rope_group(proj(_C_KPE, HEAD_PAD), tables)
    for hd in range(MLA_HEADS):
        grp = slice(hd * HEAD_PAD, (hd + 1) * HEAD_PAD)
        k_out[:, grp] = (k_nope[:, grp] + k_pe).astype(BF16)
    v_out[...] = _dot(c_n, wv_ref[...]).astype(BF16)

    xm_out[...] = proj(_C_XM, MLSTM_WIDTH)
    vm_out[...] = proj(_C_VM, MLSTM_WIDTH).astype(BF16)
    om_out[...] = proj(_C_OM, MLSTM_WIDTH)
    if_out[...] = proj(_C_IF, V7X_LANES) + bif_ref[...]
    d2 = gate_out.shape[1]
    gate_out[...] = jax.nn.sigmoid(proj(_C_GATE, d2))


def _mix_in(h, pos, mod3, norm_g, win_r, qan, wq_r, kvn, wk_r, wv_r, bif, seq):
    t, d = h.shape
    tm = MIX_TM
    row = lambda i: (i, 0)
    n_in = win_r.shape[1]
    qk_w = MLA_HEADS * HEAD_PAD
    v_w = MLA_HEADS * MLA_V_DIM
    outs = [(qk_w, BF16), (qk_w, BF16), (v_w, BF16), (MLSTM_WIDTH, F32), (MLSTM_WIDTH, BF16),
            (MLSTM_WIDTH, F32), (V7X_LANES, F32), (N_BRANCHES * d, F32)]
    return pl.pallas_call(
        _mixin_kernel,
        grid=(t // tm,),
        in_specs=[pl.BlockSpec((tm, d), row), pl.BlockSpec((tm, 1), row), _resident((1, d)),
                  _mod_spec(3, tm, seq, d), _mod_spec(4, tm, seq, d),
                  _resident((d, n_in)), _resident((1, MLA_Q_RANK)),
                  _resident((MLA_Q_RANK, qk_w)), _resident((1, MLA_KV_RANK)),
                  _resident((MLA_KV_RANK, qk_w)), _resident((MLA_KV_RANK, v_w)),
                  _resident((1, V7X_LANES))],
        out_specs=[pl.BlockSpec((tm, w), row) for w, _ in outs],
        out_shape=[jax.ShapeDtypeStruct((t, w), dt) for w, dt in outs],
        scratch_shapes=[pltpu.VMEM((tm, d), BF16)],
        compiler_params=_params(1),
        name="mix_in",
    )(h, pos, norm_g.reshape(1, d), mod3, mod3, win_r, qan.reshape(1, -1), wq_r,
      kvn.reshape(1, -1), wk_r, wv_r, bif)


def _attn_kernel(q_ref, k_ref, v_ref, o_ref, *, seq, tq):
    lane = lax.broadcasted_iota(jnp.int32, (1, 2 * MLA_V_DIM), 1)
    r_i = lax.broadcasted_iota(jnp.int32, (tq, tq), 0)
    c_i = lax.broadcasted_iota(jnp.int32, (tq, tq), 1)
    causal = c_i <= r_i
    neg = jnp.finfo(F32).min
    for qi in range(seq // tq):
        rows = slice(qi * tq, (qi + 1) * tq)
        kv = (qi + 1) * tq
        outs = []
        for j in range(2):
            grp = slice(j * HEAD_PAD, (j + 1) * HEAD_PAD)
            qh = q_ref[rows, grp]
            s_diag = lax.dot_general(qh, k_ref[kv - tq:kv, grp], (((1,), (1,)), ((), ())),
                                     preferred_element_type=F32)
            s_diag = jnp.where(causal, s_diag, neg)
            m = jnp.max(s_diag, axis=-1, keepdims=True)
            if qi > 0:
                s_off = lax.dot_general(qh, k_ref[0:kv - tq, grp], (((1,), (1,)), ((), ())),
                                        preferred_element_type=F32)
                m = jnp.maximum(m, jnp.max(s_off, axis=-1, keepdims=True))
                p_off = jnp.exp(s_off - m)
                p_diag = jnp.exp(s_diag - m)
                l = jnp.sum(p_off, axis=-1, keepdims=True) + jnp.sum(p_diag, axis=-1, keepdims=True)
                p = jnp.concatenate([p_off.astype(BF16), p_diag.astype(BF16)], axis=1)
            else:
                p_diag = jnp.exp(s_diag - m)
                l = jnp.sum(p_diag, axis=-1, keepdims=True)
                p = p_diag.astype(BF16)
            outs.append(_dot(p, v_ref[0:kv, :]) * (1.0 / l))
        o_ref[rows, :] = jnp.where(lane < MLA_V_DIM, outs[0], outs[1]).astype(BF16)


def _attention(q, k, v, bsz, seq):
    t = q.shape[0]
    pair_qk = 2 * HEAD_PAD
    pair_v = 2 * MLA_V_DIM
    blk = lambda b, g: (b, g)
    return pl.pallas_call(
        functools.partial(_attn_kernel, seq=seq, tq=ATTN_TQ),
        grid=(bsz, MLA_HEADS // 2),
        in_specs=[pl.BlockSpec((seq, pair_qk), blk), pl.BlockSpec((seq, pair_qk), blk),
                  pl.BlockSpec((seq, pair_v), blk)],
        out_specs=pl.BlockSpec((seq, pair_v), blk),
        out_shape=jax.ShapeDtypeStruct((t, MLA_HEADS * MLA_V_DIM), BF16),
        compiler_params=_params(2),
        name="mla_attn",
    )(q, k, v)


def _log_sigmoid(x):
    return jnp.minimum(x, 0.0) - jnp.log1p(jnp.exp(-jnp.abs(x)))


def _mlstm_kernel(xm_ref, vm_ref, om_ref, if_ref, cw_ref, cb_ref, wq_ref, wk_ref, hn_ref, o_ref,
                  q_scr, kt_scr, vaug_scr, *, seq, chunk):
    dh = MLSTM_HEAD_DIM
    head = pl.program_id(1)
    x = xm_ref[...]
    t_idx = lax.broadcasted_iota(jnp.int32, (seq, dh), 0)
    conv = x * cw_ref[CONV_WIDTH - 1:CONV_WIDTH, :] + cb_ref[...]
    for j in range(1, CONV_WIDTH):
        xs = jnp.where(t_idx >= j, pltpu.roll(x, j, axis=0), 0.0)
        conv = conv + xs * cw_ref[CONV_WIDTH - 1 - j:CONV_WIDTH - j, :]
    xc = (conv * jax.nn.sigmoid(conv)).astype(BF16)
    q_scr[...] = _dot(xc, wq_ref[...]).astype(BF16)
    kt_scr[...] = (_dot(xc, wk_ref[...]) * (dh ** -0.5)).T

    lane2 = lax.broadcasted_iota(jnp.int32, (seq, dh), 1)
    vaug_scr[:, 0:dh] = vm_ref[...]
    vaug_scr[:, dh:2 * dh] = jnp.where(lane2 == 0, 1.0, 0.0).astype(BF16)

    gates_t = if_ref[...].T[0:8, :]
    lane_in_chunk = lax.broadcasted_iota(jnp.int32, (8, seq), 1) % chunk
    bcum = _log_sigmoid(gates_t)
    step = 1
    while step < chunk:
        bcum = bcum + jnp.where(lane_in_chunk >= step, pltpu.roll(bcum, step, axis=1), 0.0)
        step *= 2
    sub = lax.broadcasted_iota(jnp.int32, (8, seq), 0)
    log_i_all = jnp.sum(jnp.where(sub == head, gates_t, 0.0), axis=0, keepdims=True)
    b_all = jnp.sum(jnp.where(sub == head + MLSTM_HEADS, bcum, 0.0), axis=0, keepdims=True)

    r_i = lax.broadcasted_iota(jnp.int32, (chunk, chunk), 0)
    c_i = lax.broadcasted_iota(jnp.int32, (chunk, chunk), 1)
    eye = r_i == c_i
    tril = c_i <= r_i
    last_lane = lax.broadcasted_iota(jnp.int32, (1, chunk), 1) == chunk - 1

    ct = jnp.zeros((dh, 2 * dh), F32)
    m_prev = jnp.zeros((1, 1), F32)
    for c in range(seq // chunk):
        cs = slice(c * chunk, (c + 1) * chunk)
        log_i = log_i_all[:, cs]
        b_row = b_all[:, cs]
        g = jnp.sum(jnp.where(last_lane, b_row, 0.0), axis=1, keepdims=True)
        b_col = jnp.sum(jnp.where(eye, b_row, 0.0), axis=1, keepdims=True)
        src = log_i - b_row
        d_log = jnp.where(tril, b_col + src, -jnp.inf)
        inter_log = b_col + m_prev
        m_t = jnp.maximum(inter_log, jnp.max(d_log, axis=1, keepdims=True))
        inter_w = jnp.exp(inter_log - m_t)
        q_c = q_scr[cs, :]
        kt_c = kt_scr[:, cs]
        v_c = vaug_scr[cs, :]
        s_mat = jnp.exp(d_log - m_t) * _dot(q_c, kt_c.astype(BF16))
        tot = inter_w * _dot(q_c, ct.astype(BF16)) + _dot(s_mat.astype(BF16), v_c)
        den = tot[:, dh:dh + 1]
        hh = tot[:, 0:dh] / jnp.maximum(jnp.abs(den), jnp.exp(-m_t))
        w_state = g + src
        m_loc = jnp.max(w_state, axis=1, keepdims=True)
        c_loc = _dot((kt_c * jnp.exp(w_state - m_loc)).astype(BF16), v_c)
        m_new = jnp.maximum(g + m_prev, m_loc)
        ct = jnp.exp(g + m_prev - m_new) * ct + jnp.exp(m_loc - m_new) * c_loc
        m_prev = m_new
        hh = jax.nn.sigmoid(om_ref[cs, :]) * hh
        o_ref[cs, :] = _rms(hh, hn_ref[...]).astype(BF16)


def _mlstm(xm, vm, om, ifg, conv_w, conv_b, wq, wk, head_norm, bsz, seq):
    t = xm.shape[0]
    dh = MLSTM_HEAD_DIM
    blk = lambda b, h: (b, h)
    per_head_row = lambda b, h: (0, h)
    return pl.pallas_call(
        functools.partial(_mlstm_kernel, seq=seq, chunk=MLSTM_CHUNK),
        grid=(bsz, MLSTM_HEADS),
        in_specs=[pl.BlockSpec((seq, dh), blk), pl.BlockSpec((seq, dh), blk),
                  pl.BlockSpec((seq, dh), blk), pl.BlockSpec((seq, V7X_LANES), lambda b, h: (b, 0)),
                  pl.BlockSpec((CONV_WIDTH, dh), per_head_row), pl.BlockSpec((1, dh), per_head_row),
                  pl.BlockSpec((None, dh, dh), lambda b, h: (h, 0, 0)),
                  pl.BlockSpec((None, dh, dh), lambda b, h: (h, 0, 0)),
                  pl.BlockSpec((1, dh), per_head_row)],
        out_specs=pl.BlockSpec((seq, dh), blk),
        out_shape=jax.ShapeDtypeStruct((t, MLSTM_WIDTH), BF16),
        scratch_shapes=[pltpu.VMEM((seq, dh), BF16), pltpu.VMEM((dh, seq), F32),
                        pltpu.VMEM((seq, 2 * dh), BF16)],
        compiler_params=_params(2),
        name="mlstm",
    )(xm, vm, om, ifg, conv_w, conv_b.reshape(1, -1), wq, wk, head_norm.reshape(1, -1))


def _mixout_kernel(h_ref, oa_ref, ob_ref, gate_ref, gt_ref, wa_ref, wb_ref, wo_ref, o_ref):
    d = h_ref.shape[1]
    y_a = _dot(oa_ref[...], wa_ref[...])
    y_b = _dot(ob_ref[...], wb_ref[...])
    y = gate_ref[:, 0:d] * y_a + gate_ref[:, d:2 * d] * y_b
    o_ref[...] = h_ref[...] + gt_ref[...] * _dot(y.astype(BF16), wo_ref[...])


def _mix_out(h, oa, ob, gates, mod3, wa, wb, wo, seq):
    t, d = h.shape
    tm = MIX_TM
    row = lambda i: (i, 0)
    return pl.pallas_call(
        _mixout_kernel,
        grid=(t // tm,),
        in_specs=[pl.BlockSpec((tm, d), row), pl.BlockSpec((tm, oa.shape[1]), row),
                  pl.BlockSpec((tm, ob.shape[1]), row), pl.BlockSpec((tm, N_BRANCHES * d), row),
                  _mod_spec(5, tm, seq, d),
                  _resident(wa.shape), _resident(wb.shape), _resident(wo.shape)],
        out_specs=pl.BlockSpec((tm, d), row),
        out_shape=jax.ShapeDtypeStruct((t, d), F32),
        compiler_params=_params(1),
        name="mix_out",
    )(h, oa, ob, gates, mod3, wa, wb, wo)


def _pad_cols(w, lo, width):
    return jnp.pad(w, ((0, 0), (lo, width - lo - w.shape[1])))


def _layout_w_in(w_in):
    o = 0
    parts = {}
    for name, n in (("q", MLA_Q_RANK), ("ckv", MLA_KV_RANK), ("kpe", MLA_ROPE_DIM),
                    ("xm", MLSTM_WIDTH), ("vm", MLSTM_WIDTH), ("om", MLSTM_WIDTH),
                    ("i", MLSTM_HEADS), ("f", MLSTM_HEADS), ("gate", None)):
        n = w_in.shape[1] - o if n is None else n
        parts[name] = w_in[:, o:o + n]
        o += n
    gates_if = _pad_cols(jnp.concatenate([parts["i"], parts["f"]], axis=1), 0, V7X_LANES)
    return jnp.concatenate(
        [parts["q"], parts["ckv"], _pad_cols(parts["kpe"], ROPE_LO, HEAD_PAD), parts["xm"],
         parts["vm"], parts["om"], gates_if, parts["gate"]], axis=1).astype(BF16)


def _layout_w_q_b(w_q_b):
    w = w_q_b.reshape(MLA_Q_RANK, MLA_HEADS, MLA_QK_DIM)
    w = jnp.pad(w, ((0, 0), (0, 0), (0, HEAD_PAD - MLA_QK_DIM)))
    return w.reshape(MLA_Q_RANK, MLA_HEADS * HEAD_PAD).astype(BF16)


def _layout_w_kv_b(w_kv_b):
    w = w_kv_b.reshape(MLA_KV_RANK, MLA_HEADS, MLA_NOPE_DIM + MLA_V_DIM)
    wk = jnp.pad(w[:, :, :MLA_NOPE_DIM], ((0, 0), (0, 0), (0, HEAD_PAD - MLA_NOPE_DIM)))
    wv = w[:, :, MLA_NOPE_DIM:]
    return (wk.reshape(MLA_KV_RANK, MLA_HEADS * HEAD_PAD).astype(BF16),
            wv.reshape(MLA_KV_RANK, MLA_HEADS * MLA_V_DIM).astype(BF16))


def kernel(x, c, positions, w_ada, b_ada, norm_ff1, ff1_w_gate, ff1_w_up, ff1_w_down, norm_mix, w_in,
           q_a_norm, w_q_b, kv_a_norm, w_kv_b, conv_w, conv_b, w_q_m, w_k_m, b_i, b_f, mlstm_norm,
           w_mla_out, w_mlstm_out, w_o, norm_ff2, ff2_w_gate, ff2_w_up, ff2_w_down, norm_final):
    bsz, seq, d = x.shape
    t = bsz * seq
    depth = w_ada.shape[0]
    h = x.reshape(t, d)
    pos = positions.reshape(t, 1)
    for l in range(depth):
        mod3 = _adaln(c, w_ada[l], b_ada[l]).reshape(bsz, 1, N_MOD * d)
        h = _ffn(h, mod3, 0, norm_ff1[l], ff1_w_gate[l].astype(BF16), ff1_w_up[l].astype(BF16),
                 ff1_w_down[l].astype(BF16), seq)
        wk_r, wv_r = _layout_w_kv_b(w_kv_b[l])
        bif = _pad_cols(jnp.concatenate([b_i[l], b_f[l]]).reshape(1, -1), 0, V7X_LANES)
        q, k, v, xm, vm, om, ifg, gates = _mix_in(
            h, pos, mod3, norm_mix[l], _layout_w_in(w_in[l]), q_a_norm[l], _layout_w_q_b(w_q_b[l]),
            kv_a_norm[l], wk_r, wv_r, bif, seq)
        o_a = _attention(q, k, v, bsz, seq)
        o_b = _mlstm(xm, vm, om, ifg, conv_w[l], conv_b[l], w_q_m[l].astype(BF16),
                     w_k_m[l].astype(BF16), mlstm_norm[l], bsz, seq)
        h = _mix_out(h, o_a, o_b, gates, mod3, w_mla_out[l].astype(BF16),
                     w_mlstm_out[l].astype(BF16), w_o[l].astype(BF16), seq)
        final_g = norm_final if l == depth - 1 else None
        h = _ffn(h, mod3, 6, norm_ff2[l], ff2_w_gate[l].astype(BF16), ff2_w_up[l].astype(BF16),
                 ff2_w_down[l].astype(BF16), seq, final_g=final_g)
    return h.reshape(bsz, seq, d)
```

```python
import functools

import jax
import jax.numpy as jnp
from jax import lax
from jax.experimental import pallas as pl
from jax.experimental.pallas import tpu as pltpu

F32 = jnp.float32
BF16 = jnp.bfloat16

V7X_LANES = 128
V7X_VMEM_BYTES = 64 * 1024 * 1024
VMEM_LIMIT_BYTES = V7X_VMEM_BYTES - 8 * 1024 * 1024

MLA_HEADS = 8
MLA_NOPE_DIM = 64
MLA_ROPE_DIM = 32
MLA_V_DIM = 64
MLA_QK_DIM = MLA_NOPE_DIM + MLA_ROPE_DIM
MLA_Q_RANK = 384
MLA_KV_RANK = 256
ROPE_THETA = 10000.0
MLSTM_HEADS = 4
MLSTM_HEAD_DIM = 128
MLSTM_WIDTH = MLSTM_HEADS * MLSTM_HEAD_DIM
CONV_WIDTH = 4
N_BRANCHES = 2
N_MOD = 9
EPS = 1e-6
LOG2_E = 1.4426950408889634

HEAD_PAD = V7X_LANES
ROPE_LO = MLA_NOPE_DIM
ROPE_HALF = MLA_ROPE_DIM // 2

FFN_TM = 512
FFN_TF = 256
MIX_TM = 512
ATTN_TQ = 256
MLSTM_CHUNK = 128


def _dot(a, b):
    return jnp.dot(a, b, preferred_element_type=F32)


def _rms(x, gain):
    return x * lax.rsqrt(jnp.mean(x * x, axis=-1, keepdims=True) + EPS) * gain


def _params(n_axes):
    return pltpu.CompilerParams(
        dimension_semantics=("arbitrary",) * n_axes, vmem_limit_bytes=VMEM_LIMIT_BYTES)


def _resident(shape):
    zeros = (0,) * len(shape)
    return pl.BlockSpec(shape, lambda *_: zeros, pipeline_mode=pl.Buffered(1))


def _mod_spec(chunk, tm, seq, d):
    return pl.BlockSpec((None, 1, d), lambda i: ((i * tm) // seq, 0, chunk))


def _adaln_kernel(c_ref, w_ref, b_ref, o_ref):
    c = c_ref[...]
    sc = (c * jax.nn.sigmoid(c)).astype(BF16)
    o_ref[...] = _dot(sc, w_ref[...].astype(BF16)) + b_ref[...]


def _adaln(c, w_ada, b_ada):
    bsz, d = c.shape
    n = w_ada.shape[1]
    tn = d
    return pl.pallas_call(
        _adaln_kernel,
        grid=(n // tn,),
        in_specs=[pl.BlockSpec((bsz, d), lambda j: (0, 0)),
                  pl.BlockSpec((d, tn), lambda j: (0, j)),
                  pl.BlockSpec((1, tn), lambda j: (0, j))],
        out_specs=pl.BlockSpec((bsz, tn), lambda j: (0, j)),
        out_shape=jax.ShapeDtypeStruct((bsz, n), F32),
        compiler_params=_params(1),
        name="adaln",
    )(c, w_ada, b_ada.reshape(1, n))


def _ffn_kernel(*refs, d_ff, tf, final_norm):
    if final_norm:
        (x_ref, g_ref, sh_ref, sc_ref, gt_ref, wg_ref, wu_ref, wd_ref, gf_ref,
         o_ref, u_scr, a_scr) = refs
    else:
        (x_ref, g_ref, sh_ref, sc_ref, gt_ref, wg_ref, wu_ref, wd_ref,
         o_ref, u_scr, a_scr) = refs
    x = x_ref[...]
    u = _rms(x, g_ref[...]) * (1.0 + sc_ref[...]) + sh_ref[...]
    u_scr[...] = u.astype(BF16)
    for j in range(d_ff // tf):
        cols = slice(j * tf, (j + 1) * tf)
        g = _dot(u_scr[...], wg_ref[:, cols])
        up = _dot(u_scr[...], wu_ref[:, cols])
        a_scr[:, cols] = (g * jax.nn.sigmoid(g) * up).astype(BF16)
    down = _dot(a_scr[...], wd_ref[...])
    out = x + (0.5 * gt_ref[...]) * down
    if final_norm:
        out = _rms(out, gf_ref[...])
    o_ref[...] = out


def _ffn(h, mod3, chunk0, norm_g, wg, wu, wd, seq, final_g=None):
    t, d = h.shape
    d_ff = wg.shape[1]
    tm = FFN_TM
    row = lambda i: (i, 0)
    in_specs = [pl.BlockSpec((tm, d), row), _resident((1, d)),
                _mod_spec(chunk0, tm, seq, d), _mod_spec(chunk0 + 1, tm, seq, d),
                _mod_spec(chunk0 + 2, tm, seq, d),
                _resident((d, d_ff)), _resident((d, d_ff)), _resident((d_ff, d))]
    args = [h, norm_g.reshape(1, d), mod3, mod3, mod3, wg, wu, wd]
    if final_g is not None:
        in_specs.append(_resident((1, d)))
        args.append(final_g.reshape(1, d))
    return pl.pallas_call(
        functools.partial(_ffn_kernel, d_ff=d_ff, tf=FFN_TF, final_norm=final_g is not None),
        grid=(t // tm,),
        in_specs=in_specs,
        out_specs=pl.BlockSpec((tm, d), row),
        out_shape=jax.ShapeDtypeStruct((t, d), F32),
        scratch_shapes=[pltpu.VMEM((tm, d), BF16), pltpu.VMEM((tm, d_ff), BF16)],
        compiler_params=_params(1),
        name="ffn_final" if final_g is not None else "ffn",
    )(*args)


_C_QLAT = 0
_C_CKV = _C_QLAT + MLA_Q_RANK
_C_KPE = _C_CKV + MLA_KV_RANK
_C_XM = _C_KPE + HEAD_PAD
_C_VM = _C_XM + MLSTM_WIDTH
_C_OM = _C_VM + MLSTM_WIDTH
_C_IF = _C_OM + MLSTM_WIDTH
_C_GATE = _C_IF + V7X_LANES


def _rope_tables(pos_f32):
    lane = lax.broadcasted_iota(jnp.int32, (1, HEAD_PAD), 1)
    in_lo = (lane >= ROPE_LO) & (lane < ROPE_LO + ROPE_HALF)
    in_hi = (lane >= ROPE_LO + ROPE_HALF) & (lane < ROPE_LO + MLA_ROPE_DIM)
    idx = jnp.where(in_hi, lane - (ROPE_LO + ROPE_HALF), lane - ROPE_LO)
    idx = jnp.where(in_lo | in_hi, idx, 0).astype(F32)
    theta = jnp.full((1, HEAD_PAD), ROPE_THETA, F32)
    inv_freq = jnp.exp(-(idx / ROPE_HALF) * jnp.log(theta))
    ang = pos_f32 * inv_freq
    cos = jnp.cos(ang)
    sin = jnp.sin(ang)
    cos_m = jnp.where(lane < ROPE_LO, 1.0, jnp.where(in_lo | in_hi, cos, 0.0))
    sin_up = jnp.where(in_lo, -sin, 0.0)
    sin_dn = jnp.where(in_hi, sin, 0.0)
    return cos_m, sin_up, sin_dn


def _rope_group(x, tables):
    cos_m, sin_up, sin_dn = tables
    up = pltpu.roll(x, HEAD_PAD - ROPE_HALF, axis=1)
    dn = pltpu.roll(x, ROPE_HALF, axis=1)
    return x * cos_m + up * sin_up + dn * sin_dn


def _mixin_kernel(h_ref, pos_ref, g_ref, sh_ref, sc_ref, win_ref, qan_ref, wq_ref, kvn_ref,
                  wk_ref, wv_ref, bif_ref,
                  q_out, k_out, v_out, xm_out, vm_out, om_out, if_out, gate_out, u_scr):
    x = h_ref[...]
    u_scr[...] = (_rms(x, g_ref[...]) * (1.0 + sc_ref[...]) + sh_ref[...]).astype(BF16)

    def proj(lo, width):
        return _dot(u_scr[...], win_ref[:, lo:lo + width])

    tables = _rope_tables(pos_ref[...].astype(F32))

    q_lat = proj(_C_QLAT, MLA_Q_RANK)
    q = _dot(_rms(q_lat, qan_ref[...]).astype(BF16), wq_ref[...])
    q_scale = MLA_QK_DIM ** -0.5 * LOG2_E
    for hd in range(MLA_HEADS):
        grp = slice(hd * HEAD_PAD, (hd + 1) * HEAD_PAD)
        q_out[:, grp] = (_rope_group(q[:, grp], tables) * q_scale).astype(BF16)

    c_kv = proj(_C_CKV, MLA_KV_RANK)
    c_n = _rms(c_kv, kvn_ref[...]).astype(BF16)
    k_nope = _dot(c_n, wk_ref[...])
    k_pe = _rope_group(proj(_C_KPE, HEAD_PAD), tables)
    for hd in range(MLA_HEADS):
        grp = slice(hd * HEAD_PAD, (hd + 1) * HEAD_PAD)
        k_out[:, grp] = (k_nope[:, grp] + k_pe).astype(BF16)
    v_out[...] = _dot(c_n, wv_ref[...]).astype(BF16)

    xm_out[...] = proj(_C_XM, MLSTM_WIDTH)
    vm_out[...] = proj(_C_VM, MLSTM_WIDTH).astype(BF16)
    om_out[...] = proj(_C_OM, MLSTM_WIDTH)
    if_out[...] = proj(_C_IF, V7X_LANES) + bif_ref[...]
    d2 = gate_out.shape[1]
    gate_out[...] = jax.nn.sigmoid(proj(_C_GATE, d2))


def _mix_in(h, pos, mod3, norm_g, win_r, qan, wq_r, kvn, wk_r, wv_r, bif, seq):
    t, d = h.shape
    tm = MIX_TM
    row = lambda i: (i, 0)
    n_in = win_r.shape[1]
    qk_w = MLA_HEADS * HEAD_PAD
    v_w = MLA_HEADS * MLA_V_DIM
    outs = [(qk_w, BF16), (qk_w, BF16), (v_w, BF16), (MLSTM_WIDTH, F32), (MLSTM_WIDTH, BF16),
            (MLSTM_WIDTH, F32), (V7X_LANES, F32), (N_BRANCHES * d, F32)]
    return pl.pallas_call(
        _mixin_kernel,
        grid=(t // tm,),
        in_specs=[pl.BlockSpec((tm, d), row), pl.BlockSpec((tm, 1), row), _resident((1, d)),
                  _mod_spec(3, tm, seq, d), _mod_spec(4, tm, seq, d),
                  _resident((d, n_in)), _resident((1, MLA_Q_RANK)),
                  _resident((MLA_Q_RANK, qk_w)), _resident((1, MLA_KV_RANK)),
                  _resident((MLA_KV_RANK, qk_w)), _resident((MLA_KV_RANK, v_w)),
                  _resident((1, V7X_LANES))],
        out_specs=[pl.BlockSpec((tm, w), row) for w, _ in outs],
        out_shape=[jax.ShapeDtypeStruct((t, w), dt) for w, dt in outs],
        scratch_shapes=[pltpu.VMEM((tm, d), BF16)],
        compiler_params=_params(1),
        name="mix_in",
    )(h, pos, norm_g.reshape(1, d), mod3, mod3, win_r, qan.reshape(1, -1), wq_r,
      kvn.reshape(1, -1), wk_r, wv_r, bif)


def _attn_kernel(q_ref, k_ref, v_ref, o_ref, vaug_scr, s_scr, p_scr, *, seq, tq):
    pair_v = 2 * MLA_V_DIM
    lane = lax.broadcasted_iota(jnp.int32, (1, pair_v), 1)
    vaug_scr[:, 0:pair_v] = v_ref[...]
    vaug_scr[:, pair_v:2 * pair_v] = jnp.where(
        lax.broadcasted_iota(jnp.int32, (seq, pair_v), 1) == 0, 1.0, 0.0).astype(BF16)
    r_i = lax.broadcasted_iota(jnp.int32, (tq, tq), 0)
    c_i = lax.broadcasted_iota(jnp.int32, (tq, tq), 1)
    causal = c_i <= r_i
    neg = jnp.finfo(F32).min
    for qi in range(seq // tq):
        rows = slice(qi * tq, (qi + 1) * tq)
        kv = (qi + 1) * tq
        outs = []
        for j in range(2):
            grp = slice(j * HEAD_PAD, (j + 1) * HEAD_PAD)
            qh = q_ref[rows, grp]
            m_vec = None
            for cj in range(qi + 1):
                cols = slice(cj * tq, (cj + 1) * tq)
                s = lax.dot_general(qh, k_ref[cols, grp], (((1,), (1,)), ((), ())),
                                    preferred_element_type=F32)
                if cj == qi:
                    s = jnp.where(causal, s, neg)
                s_scr[j, :, cols] = s
                for g in range(tq // V7X_LANES):
                    part = s[:, g * V7X_LANES:(g + 1) * V7X_LANES]
                    m_vec = part if m_vec is None else jnp.maximum(m_vec, part)
            m = jnp.max(m_vec, axis=-1, keepdims=True)
            for cj in range(qi + 1):
                cols = slice(cj * tq, (cj + 1) * tq)
                p_scr[j, :, cols] = jnp.exp2(s_scr[j, :, cols] - m).astype(BF16)
            o_aug = _dot(p_scr[j, :, 0:kv], vaug_scr[0:kv, :])
            outs.append(o_aug[:, 0:pair_v] * (1.0 / o_aug[:, pair_v:pair_v + 1]))
        o_ref[rows, :] = jnp.where(lane < MLA_V_DIM, outs[0], outs[1]).astype(BF16)


def _attention(q, k, v, bsz, seq):
    t = q.shape[0]
    pair_qk = 2 * HEAD_PAD
    pair_v = 2 * MLA_V_DIM
    blk = lambda b, g: (b, g)
    return pl.pallas_call(
        functools.partial(_attn_kernel, seq=seq, tq=ATTN_TQ),
        grid=(bsz, MLA_HEADS // 2),
        in_specs=[pl.BlockSpec((seq, pair_qk), blk), pl.BlockSpec((seq, pair_qk), blk),
                  pl.BlockSpec((seq, pair_v), blk)],
        out_specs=pl.BlockSpec((seq, pair_v), blk),
        out_shape=jax.ShapeDtypeStruct((t, MLA_HEADS * MLA_V_DIM), BF16),
        scratch_shapes=[pltpu.VMEM((seq, 2 * pair_v), BF16), pltpu.VMEM((2, ATTN_TQ, seq), F32),
                        pltpu.VMEM((2, ATTN_TQ, seq), BF16)],
        compiler_params=_params(2),
        name="mla_attn",
    )(q, k, v)


def _log_sigmoid(x):
    return jnp.minimum(x, 0.0) - jnp.log1p(jnp.exp(-jnp.abs(x)))


def _mlstm_kernel(xm_ref, vm_ref, om_ref, if_ref, cw_ref, cb_ref, wq_ref, wk_ref, hn_ref, o_ref,
                  q_scr, kt_scr, vaug_scr, *, seq, chunk):
    dh = MLSTM_HEAD_DIM
    head = pl.program_id(1)
    x = xm_ref[...]
    t_idx = lax.broadcasted_iota(jnp.int32, (seq, dh), 0)
    conv = x * cw_ref[CONV_WIDTH - 1:CONV_WIDTH, :] + cb_ref[...]
    for j in range(1, CONV_WIDTH):
        xs = jnp.where(t_idx >= j, pltpu.roll(x, j, axis=0), 0.0)
        conv = conv + xs * cw_ref[CONV_WIDTH - 1 - j:CONV_WIDTH - j, :]
    xc = (conv * jax.nn.sigmoid(conv)).astype(BF16)
    q_scr[...] = _dot(xc, wq_ref[...]).astype(BF16)
    kt_scr[...] = (_dot(xc, wk_ref[...]) * (dh ** -0.5)).T

    lane2 = lax.broadcasted_iota(jnp.int32, (seq, dh), 1)
    vaug_scr[:, 0:dh] = vm_ref[...]
    vaug_scr[:, dh:2 * dh] = jnp.where(lane2 == 0, 1.0, 0.0).astype(BF16)

    gates_t = if_ref[...].T[0:8, :]
    lane_in_chunk = lax.broadcasted_iota(jnp.int32, (8, seq), 1) % chunk
    bcum = _log_sigmoid(gates_t)
    step = 1
    while step < chunk:
        bcum = bcum + jnp.where(lane_in_chunk >= step, pltpu.roll(bcum, step, axis=1), 0.0)
        step *= 2
    sub = lax.broadcasted_iota(jnp.int32, (8, seq), 0)
    log_i_all = jnp.sum(jnp.where(sub == head, gates_t, 0.0), axis=0, keepdims=True)
    b_all = jnp.sum(jnp.where(sub == head + MLSTM_HEADS, bcum, 0.0), axis=0, keepdims=True)

    r_i = lax.broadcasted_iota(jnp.int32, (chunk, chunk), 0)
    c_i = lax.broadcasted_iota(jnp.int32, (chunk, chunk), 1)
    eye = r_i == c_i
    tril = c_i <= r_i
    last_lane = lax.broadcasted_iota(jnp.int32, (1, chunk), 1) == chunk - 1

    ct = jnp.zeros((dh, 2 * dh), F32)
    m_prev = jnp.zeros((1, 1), F32)
    for c in range(seq // chunk):
        cs = slice(c * chunk, (c + 1) * chunk)
        log_i = log_i_all[:, cs]
        b_row = b_all[:, cs]
        g = jnp.sum(jnp.where(last_lane, b_row, 0.0), axis=1, keepdims=True)
        b_col = jnp.sum(jnp.where(eye, b_row, 0.0), axis=1, keepdims=True)
        src = log_i - b_row
        d_log = jnp.where(tril, b_col + src, -jnp.inf)
        inter_log = b_col + m_prev
        m_t = jnp.maximum(inter_log, jnp.max(d_log, axis=1, keepdims=True))
        inter_w = jnp.exp(inter_log - m_t)
        q_c = q_scr[cs, :]
        kt_c = kt_scr[:, cs]
        v_c = vaug_scr[cs, :]
        s_mat = jnp.exp(d_log - m_t) * _dot(q_c, kt_c.astype(BF16))
        tot = inter_w * _dot(q_c, ct.astype(BF16)) + _dot(s_mat.astype(BF16), v_c)
        den = tot[:, dh:dh + 1]
        hh = tot[:, 0:dh] / jnp.maximum(jnp.abs(den), jnp.exp(-m_t))
        w_state = g + src
        m_loc = jnp.max(w_state, axis=1, keepdims=True)
        c_loc = _dot((kt_c * jnp.exp(w_state - m_loc)).astype(BF16), v_c)
        m_new = jnp.maximum(g + m_prev, m_loc)
        ct = jnp.exp(g + m_prev - m_new) * ct + jnp.exp(m_loc - m_new) * c_loc
        m_prev = m_new
        hh = jax.nn.sigmoid(om_ref[cs, :]) * hh
        o_ref[cs, :] = _rms(hh, hn_ref[...]).astype(BF16)


def _mlstm(xm, vm, om, ifg, conv_w, conv_b, wq, wk, head_norm, bsz, seq):
    t = xm.shape[0]
    dh = MLSTM_HEAD_DIM
    blk = lambda b, h: (b, h)
    per_head_row = lambda b, h: (0, h)
    return pl.pallas_call(
        functools.partial(_mlstm_kernel, seq=seq, chunk=MLSTM_CHUNK),
        grid=(bsz, MLSTM_HEADS),
        in_specs=[pl.BlockSpec((seq, dh), blk), pl.BlockSpec((seq, dh), blk),
                  pl.BlockSpec((seq, dh), blk), pl.BlockSpec((seq, V7X_LANES), lambda b, h: (b, 0)),
                  pl.BlockSpec((CONV_WIDTH, dh), per_head_row), pl.BlockSpec((1, dh), per_head_row),
                  pl.BlockSpec((None, dh, dh), lambda b, h: (h, 0, 0)),
                  pl.BlockSpec((None, dh, dh), lambda b, h: (h, 0, 0)),
                  pl.BlockSpec((1, dh), per_head_row)],
        out_specs=pl.BlockSpec((seq, dh), blk),
        out_shape=jax.ShapeDtypeStruct((t, MLSTM_WIDTH), BF16),
        scratch_shapes=[pltpu.VMEM((seq, dh), BF16), pltpu.VMEM((dh, seq), F32),
                        pltpu.VMEM((seq, 2 * dh), BF16)],
        compiler_params=_params(2),
        name="mlstm",
    )(xm, vm, om, ifg, conv_w, conv_b.reshape(1, -1), wq, wk, head_norm.reshape(1, -1))


def _mixout_kernel(h_ref, oa_ref, ob_ref, gate_ref, gt_ref, wa_ref, wb_ref, wo_ref, o_ref):
    d = h_ref.shape[1]
    y_a = _dot(oa_ref[...], wa_ref[...])
    y_b = _dot(ob_ref[...], wb_ref[...])
    y = gate_ref[:, 0:d] * y_a + gate_ref[:, d:2 * d] * y_b
    o_ref[...] = h_ref[...] + gt_ref[...] * _dot(y.astype(BF16), wo_ref[...])


def _mix_out(h, oa, ob, gates, mod3, wa, wb, wo, seq):
    t, d = h.shape
    tm = MIX_TM
    row = lambda i: (i, 0)
    return pl.pallas_call(
        _mixout_kernel,
        grid=(t // tm,),
        in_specs=[pl.BlockSpec((tm, d), row), pl.BlockSpec((tm, oa.shape[1]), row),
                  pl.BlockSpec((tm, ob.shape[1]), row), pl.BlockSpec((tm, N_BRANCHES * d), row),
                  _mod_spec(5, tm, seq, d),
                  _resident(wa.shape), _resident(wb.shape), _resident(wo.shape)],
        out_specs=pl.BlockSpec((tm, d), row),
        out_shape=jax.ShapeDtypeStruct((t, d), F32),
        compiler_params=_params(1),
        name="mix_out",
    )(h, oa, ob, gates, mod3, wa, wb, wo)


def _pad_cols(w, lo, width):
    return jnp.pad(w, ((0, 0), (lo, width - lo - w.shape[1])))


def _layout_w_in(w_in):
    o = 0
    parts = {}
    for name, n in (("q", MLA_Q_RANK), ("ckv", MLA_KV_RANK), ("kpe", MLA_ROPE_DIM),
                    ("xm", MLSTM_WIDTH), ("vm", MLSTM_WIDTH), ("om", MLSTM_WIDTH),
                    ("i", MLSTM_HEADS), ("f", MLSTM_HEADS), ("gate", None)):
        n = w_in.shape[1] - o if n is None else n
        parts[name] = w_in[:, o:o + n]
        o += n
    gates_if = _pad_cols(jnp.concatenate([parts["i"], parts["f"]], axis=1), 0, V7X_LANES)
    return jnp.concatenate(
        [parts["q"], parts["ckv"], _pad_cols(parts["kpe"], ROPE_LO, HEAD_PAD), parts["xm"],
         parts["vm"], parts["om"], gates_if, parts["gate"]], axis=1).astype(BF16)


def _layout_w_q_b(w_q_b):
    w = w_q_b.reshape(MLA_Q_RANK, MLA_HEADS, MLA_QK_DIM)
    w = jnp.pad(w, ((0, 0), (0, 0), (0, HEAD_PAD - MLA_QK_DIM)))
    return w.reshape(MLA_Q_RANK, MLA_HEADS * HEAD_PAD).astype(BF16)


def _layout_w_kv_b(w_kv_b):
    w = w_kv_b.reshape(MLA_KV_RANK, MLA_HEADS, MLA_NOPE_DIM + MLA_V_DIM)
    wk = jnp.pad(w[:, :, :MLA_NOPE_DIM], ((0, 0), (0, 0), (0, HEAD_PAD - MLA_NOPE_DIM)))
    wv = w[:, :, MLA_NOPE_DIM:]
    return (wk.reshape(MLA_KV_RANK, MLA_HEADS * HEAD_PAD).astype(BF16),
            wv.reshape(MLA_KV_RANK, MLA_HEADS * MLA_V_DIM).astype(BF16))


def kernel(x, c, positions, w_ada, b_ada, norm_ff1, ff1_w_gate, ff1_w_up, ff1_w_down, norm_mix, w_in,
           q_a_norm, w_q_b, kv_a_norm, w_kv_b, conv_w, conv_b, w_q_m, w_k_m, b_i, b_f, mlstm_norm,
           w_mla_out, w_mlstm_out, w_o, norm_ff2, ff2_w_gate, ff2_w_up, ff2_w_down, norm_final):
    bsz, seq, d = x.shape
    t = bsz * seq
    depth = w_ada.shape[0]
    h = x.reshape(t, d)
    pos = positions.reshape(t, 1)
    for l in range(depth):
        mod3 = _adaln(c, w_ada[l], b_ada[l]).reshape(bsz, 1, N_MOD * d)
        h = _ffn(h, mod3, 0, norm_ff1[l], ff1_w_gate[l].astype(BF16), ff1_w_up[l].astype(BF16),
                 ff1_w_down[l].astype(BF16), seq)
        wk_r, wv_r = _layout_w_kv_b(w_kv_b[l])
        bif = _pad_cols(jnp.concatenate([b_i[l], b_f[l]]).reshape(1, -1), 0, V7X_LANES)
        q, k, v, xm, vm, om, ifg, gates = _mix_in(
            h, pos, mod3, norm_mix[l], _layout_w_in(w_in[l]), q_a_norm[l], _layout_w_q_b(w_q_b[l]),
            kv_a_norm[l], wk_r, wv_r, bif, seq)
        o_a = _attention(q, k, v, bsz, seq)
        o_b = _mlstm(xm, vm, om, ifg, conv_w[l], conv_b[l], w_q_m[l].astype(BF16),
                     w_k_m[l].astype(BF16), mlstm_norm[l], bsz, seq)
        h = _mix_out(h, o_a, o_b, gates, mod3, w_mla_out[l].astype(BF16),
                     w_mlstm_out[l].astype(BF16), w_o[l].astype(BF16), seq)
        final_g = norm_final if l == depth - 1 else None
        h = _ffn(h, mod3, 6, norm_ff2[l], ff2_w_gate[l].astype(BF16), ff2_w_up[l].astype(BF16),
                 ff2_w_down[l].astype(BF16), seq, final_g=final_g)
    return h.reshape(bsz, seq, d)
```

```python
import functools

import jax
import jax.numpy as jnp
from jax import lax
from jax.experimental import pallas as pl
from jax.experimental.pallas import tpu as pltpu

F32 = jnp.float32
BF16 = jnp.bfloat16

V7X_LANES = 128
V7X_VMEM_BYTES = 64 * 1024 * 1024
VMEM_LIMIT_BYTES = V7X_VMEM_BYTES - 8 * 1024 * 1024

MLA_HEADS = 8
MLA_NOPE_DIM = 64
MLA_ROPE_DIM = 32
MLA_V_DIM = 64
MLA_QK_DIM = MLA_NOPE_DIM + MLA_ROPE_DIM
MLA_Q_RANK = 384
MLA_KV_RANK = 256
ROPE_THETA = 10000.0
MLSTM_HEADS = 4
MLSTM_HEAD_DIM = 128
MLSTM_WIDTH = MLSTM_HEADS * MLSTM_HEAD_DIM
CONV_WIDTH = 4
GATE_ROWS = 2 * MLSTM_HEADS
CONV_HALO = 8
N_BRANCHES = 2
N_MOD = 9
EPS = 1e-6
LOG2_E = 1.4426950408889634

HEAD_PAD = V7X_LANES
ROPE_LO = MLA_NOPE_DIM
ROPE_HALF = MLA_ROPE_DIM // 2

FFN_TM = 512
FFN_TF = 256
MIX_TM = 512
ATTN_TQ = 256
MLSTM_CHUNK = 128


def _dot(a, b):
    return jnp.dot(a, b, preferred_element_type=F32)


def _rms(x, gain):
    return x * lax.rsqrt(jnp.mean(x * x, axis=-1, keepdims=True) + EPS) * gain


def _params(n_axes):
    return pltpu.CompilerParams(
        dimension_semantics=("arbitrary",) * n_axes, vmem_limit_bytes=VMEM_LIMIT_BYTES)


def _resident(shape):
    zeros = (0,) * len(shape)
    return pl.BlockSpec(shape, lambda *_: zeros, pipeline_mode=pl.Buffered(1))


def _mod_spec(chunk, tm, seq, d):
    return pl.BlockSpec((None, 1, d), lambda i: ((i * tm) // seq, 0, chunk))


def _adaln_kernel(c_ref, w_ref, b_ref, o_ref):
    c = c_ref[...]
    sc = (c * jax.nn.sigmoid(c)).astype(BF16)
    o_ref[...] = _dot(sc, w_ref[...].astype(BF16)) + b_ref[...]


def _adaln(c, w_ada, b_ada):
    bsz, d = c.shape
    n = w_ada.shape[1]
    tn = d
    return pl.pallas_call(
        _adaln_kernel,
        grid=(n // tn,),
        in_specs=[pl.BlockSpec((bsz, d), lambda j: (0, 0)),
                  pl.BlockSpec((d, tn), lambda j: (0, j)),
                  pl.BlockSpec((1, tn), lambda j: (0, j))],
        out_specs=pl.BlockSpec((bsz, tn), lambda j: (0, j)),
        out_shape=jax.ShapeDtypeStruct((bsz, n), F32),
        compiler_params=_params(1),
        name="adaln",
    )(c, w_ada, b_ada.reshape(1, n))


def _ffn_kernel(*refs, d_ff, tf, final_norm):
    if final_norm:
        (x_ref, g_ref, sh_ref, sc_ref, gt_ref, wg_ref, wu_ref, wd_ref, gf_ref,
         o_ref, u_scr, a_scr) = refs
    else:
        (x_ref, g_ref, sh_ref, sc_ref, gt_ref, wg_ref, wu_ref, wd_ref,
         o_ref, u_scr, a_scr) = refs
    x = x_ref[...]
    u = _rms(x, g_ref[...]) * (1.0 + sc_ref[...]) + sh_ref[...]
    u_scr[...] = u.astype(BF16)
    for j in range(d_ff // tf):
        cols = slice(j * tf, (j + 1) * tf)
        g = _dot(u_scr[...], wg_ref[:, cols])
        up = _dot(u_scr[...], wu_ref[:, cols])
        a_scr[:, cols] = (g * jax.nn.sigmoid(g) * up).astype(BF16)
    down = _dot(a_scr[...], wd_ref[...])
    out = x + (0.5 * gt_ref[...]) * down
    if final_norm:
        out = _rms(out, gf_ref[...])
    o_ref[...] = out


def _ffn(h, mod3, chunk0, norm_g, wg, wu, wd, seq, final_g=None):
    t, d = h.shape
    d_ff = wg.shape[1]
    tm = FFN_TM
    row = lambda i: (i, 0)
    in_specs = [pl.BlockSpec((tm, d), row), _resident((1, d)),
                _mod_spec(chunk0, tm, seq, d), _mod_spec(chunk0 + 1, tm, seq, d),
                _mod_spec(chunk0 + 2, tm, seq, d),
                _resident((d, d_ff)), _resident((d, d_ff)), _resident((d_ff, d))]
    args = [h, norm_g.reshape(1, d), mod3, mod3, mod3, wg, wu, wd]
    if final_g is not None:
        in_specs.append(_resident((1, d)))
        args.append(final_g.reshape(1, d))
    return pl.pallas_call(
        functools.partial(_ffn_kernel, d_ff=d_ff, tf=FFN_TF, final_norm=final_g is not None),
        grid=(t // tm,),
        in_specs=in_specs,
        out_specs=pl.BlockSpec((tm, d), row),
        out_shape=jax.ShapeDtypeStruct((t, d), F32),
        scratch_shapes=[pltpu.VMEM((tm, d), BF16), pltpu.VMEM((tm, d_ff), BF16)],
        compiler_params=_params(1),
        name="ffn_final" if final_g is not None else "ffn",
    )(*args)


_C_QLAT = 0
_C_CKV = _C_QLAT + MLA_Q_RANK
_C_KPE = _C_CKV + MLA_KV_RANK
_C_XM = _C_KPE + HEAD_PAD
_C_VM = _C_XM + MLSTM_WIDTH
_C_OM = _C_VM + MLSTM_WIDTH
_C_IF = _C_OM + MLSTM_WIDTH
_C_GATE = _C_IF + V7X_LANES


def _rope_tables(pos_f32):
    lane = lax.broadcasted_iota(jnp.int32, (1, HEAD_PAD), 1)
    in_lo = (lane >= ROPE_LO) & (lane < ROPE_LO + ROPE_HALF)
    in_hi = (lane >= ROPE_LO + ROPE_HALF) & (lane < ROPE_LO + MLA_ROPE_DIM)
    idx = jnp.where(in_hi, lane - (ROPE_LO + ROPE_HALF), lane - ROPE_LO)
    idx = jnp.where(in_lo | in_hi, idx, 0).astype(F32)
    theta = jnp.full((1, HEAD_PAD), ROPE_THETA, F32)
    inv_freq = jnp.exp(-(idx / ROPE_HALF) * jnp.log(theta))
    ang = pos_f32 * inv_freq
    cos = jnp.cos(ang)
    sin = jnp.sin(ang)
    cos_m = jnp.where(lane < ROPE_LO, 1.0, jnp.where(in_lo | in_hi, cos, 0.0))
    sin_up = jnp.where(in_lo, -sin, 0.0)
    sin_dn = jnp.where(in_hi, sin, 0.0)
    return cos_m, sin_up, sin_dn


def _rope_group(x, tables):
    cos_m, sin_up, sin_dn = tables
    up = pltpu.roll(x, HEAD_PAD - ROPE_HALF, axis=1)
    dn = pltpu.roll(x, ROPE_HALF, axis=1)
    return x * cos_m + up * sin_up + dn * sin_dn


def _mixin_kernel(h_ref, pos_ref, g_ref, sh_ref, sc_ref, win_ref, qan_ref, wq_ref, kvn_ref,
                  wk_ref, wv_ref, bif_ref, cw_ref, cb_ref,
                  q_out, k_out, v_out, xc_out, vm_out, og_out, if_out, gate_out, u_scr, xext_scr,
                  *, tm, seq):
    x = h_ref[...]
    u_scr[...] = (_rms(x, g_ref[...]) * (1.0 + sc_ref[...]) + sh_ref[...]).astype(BF16)

    def proj(lo, width):
        return _dot(u_scr[...], win_ref[:, lo:lo + width])

    tables = _rope_tables(pos_ref[...].astype(F32))

    q_lat = proj(_C_QLAT, MLA_Q_RANK)
    q = _dot(_rms(q_lat, qan_ref[...]).astype(BF16), wq_ref[...])
    q_scale = MLA_QK_DIM ** -0.5 * LOG2_E
    for hd in range(MLA_HEADS):
        grp = slice(hd * HEAD_PAD, (hd + 1) * HEAD_PAD)
        q_out[:, grp] = (_rope_group(q[:, grp], tables) * q_scale).astype(BF16)

    c_kv = proj(_C_CKV, MLA_KV_RANK)
    c_n = _rms(c_kv, kvn_ref[...]).astype(BF16)
    k_nope = _dot(c_n, wk_ref[...])
    k_pe = _rope_group(proj(_C_KPE, HEAD_PAD), tables)
    for hd in range(MLA_HEADS):
        grp = slice(hd * HEAD_PAD, (hd + 1) * HEAD_PAD)
        k_out[:, grp] = (k_nope[:, grp] + k_pe).astype(BF16)
    v_out[...] = _dot(c_n, wv_ref[...]).astype(BF16)

    @pl.when((pl.program_id(0) * tm) % seq == 0)
    def _():
        xext_scr[0:CONV_HALO, :] = jnp.zeros((CONV_HALO, MLSTM_WIDTH), F32)

    xext_scr[CONV_HALO:CONV_HALO + tm, :] = proj(_C_XM, MLSTM_WIDTH)
    conv = cb_ref[...] + xext_scr[CONV_HALO:CONV_HALO + tm, :] * cw_ref[CONV_WIDTH - 1:CONV_WIDTH, :]
    for j in range(1, CONV_WIDTH):
        conv = conv + (xext_scr[CONV_HALO - j:CONV_HALO - j + tm, :]
                       * cw_ref[CONV_WIDTH - 1 - j:CONV_WIDTH - j, :])
    xext_scr[0:CONV_HALO, :] = xext_scr[tm:tm + CONV_HALO, :]
    xc_out[...] = (conv * jax.nn.sigmoid(conv)).astype(BF16)

    vm_out[...] = proj(_C_VM, MLSTM_WIDTH).astype(BF16)
    og_out[...] = jax.nn.sigmoid(proj(_C_OM, MLSTM_WIDTH)).astype(BF16)
    if_out[...] = (proj(_C_IF, V7X_LANES) + bif_ref[...]).T[0:GATE_ROWS, :]
    d2 = gate_out.shape[1]
    gate_out[...] = jax.nn.sigmoid(proj(_C_GATE, d2)).astype(BF16)


def _mix_in(h, pos, mod3, norm_g, win_r, qan, wq_r, kvn, wk_r, wv_r, bif, conv_w, conv_b, seq):
    t, d = h.shape
    tm = MIX_TM
    row = lambda i: (i, 0)
    n_in = win_r.shape[1]
    qk_w = MLA_HEADS * HEAD_PAD
    v_w = MLA_HEADS * MLA_V_DIM
    outs = [(qk_w, BF16), (qk_w, BF16), (v_w, BF16), (MLSTM_WIDTH, BF16), (MLSTM_WIDTH, BF16),
            (MLSTM_WIDTH, BF16), None, (N_BRANCHES * d, BF16)]
    out_specs = [pl.BlockSpec((tm, o[0]), row) if o else pl.BlockSpec((GATE_ROWS, tm), lambda i: (0, i))
                 for o in outs]
    out_shape = [jax.ShapeDtypeStruct((t, o[0]), o[1]) if o else jax.ShapeDtypeStruct((GATE_ROWS, t), F32)
                 for o in outs]
    return pl.pallas_call(
        functools.partial(_mixin_kernel, tm=tm, seq=seq),
        grid=(t // tm,),
        in_specs=[pl.BlockSpec((tm, d), row), pl.BlockSpec((tm, 1), row), _resident((1, d)),
                  _mod_spec(3, tm, seq, d), _mod_spec(4, tm, seq, d),
                  _resident((d, n_in)), _resident((1, MLA_Q_RANK)),
                  _resident((MLA_Q_RANK, qk_w)), _resident((1, MLA_KV_RANK)),
                  _resident((MLA_KV_RANK, qk_w)), _resident((MLA_KV_RANK, v_w)),
                  _resident((1, V7X_LANES)), _resident((CONV_WIDTH, MLSTM_WIDTH)),
                  _resident((1, MLSTM_WIDTH))],
        out_specs=out_specs,
        out_shape=out_shape,
        scratch_shapes=[pltpu.VMEM((tm, d), BF16), pltpu.VMEM((tm + CONV_HALO, MLSTM_WIDTH), F32)],
        compiler_params=_params(1),
        name="mix_in",
    )(h, pos, norm_g.reshape(1, d), mod3, mod3, win_r, qan.reshape(1, -1), wq_r,
      kvn.reshape(1, -1), wk_r, wv_r, bif, conv_w, conv_b.reshape(1, -1))


def _attn_kernel(q_ref, k_ref, v_ref, o_ref, vaug_scr, s_scr, p_scr, *, seq, tq):
    pair_v = 2 * MLA_V_DIM
    lane = lax.broadcasted_iota(jnp.int32, (1, pair_v), 1)
    vaug_scr[:, 0:pair_v] = v_ref[...]
    vaug_scr[:, pair_v:2 * pair_v] = jnp.where(
        lax.broadcasted_iota(jnp.int32, (seq, pair_v), 1) == 0, 1.0, 0.0).astype(BF16)
    r_i = lax.broadcasted_iota(jnp.int32, (tq, tq), 0)
    c_i = lax.broadcasted_iota(jnp.int32, (tq, tq), 1)
    causal = c_i <= r_i
    neg = jnp.finfo(F32).min
    for qi in range(seq // tq):
        rows = slice(qi * tq, (qi + 1) * tq)
        kv = (qi + 1) * tq
        outs = []
        for j in range(2):
            grp = slice(j * HEAD_PAD, (j + 1) * HEAD_PAD)
            qh = q_ref[rows, grp]
            m_vec = None
            for cj in range(qi + 1):
                cols = slice(cj * tq, (cj + 1) * tq)
                s = lax.dot_general(qh, k_ref[cols, grp], (((1,), (1,)), ((), ())),
                                    preferred_element_type=F32)
                if cj == qi:
                    s = jnp.where(causal, s, neg)
                s_scr[j, :, cols] = s
                for g in range(tq // V7X_LANES):
                    part = s[:, g * V7X_LANES:(g + 1) * V7X_LANES]
                    m_vec = part if m_vec is None else jnp.maximum(m_vec, part)
            m = jnp.max(m_vec, axis=-1, keepdims=True)
            for cj in range(qi + 1):
                cols = slice(cj * tq, (cj + 1) * tq)
                p_scr[j, :, cols] = jnp.exp2(s_scr[j, :, cols] - m).astype(BF16)
            o_aug = _dot(p_scr[j, :, 0:kv], vaug_scr[0:kv, :])
            outs.append(o_aug[:, 0:pair_v] * (1.0 / o_aug[:, pair_v:pair_v + 1]))
        o_ref[rows, :] = jnp.where(lane < MLA_V_DIM, outs[0], outs[1]).astype(BF16)


def _attention(q, k, v, bsz, seq):
    t = q.shape[0]
    pair_qk = 2 * HEAD_PAD
    pair_v = 2 * MLA_V_DIM
    blk = lambda b, g: (b, g)
    return pl.pallas_call(
        functools.partial(_attn_kernel, seq=seq, tq=ATTN_TQ),
        grid=(bsz, MLA_HEADS // 2),
        in_specs=[pl.BlockSpec((seq, pair_qk), blk), pl.BlockSpec((seq, pair_qk), blk),
                  pl.BlockSpec((seq, pair_v), blk)],
        out_specs=pl.BlockSpec((seq, pair_v), blk),
        out_shape=jax.ShapeDtypeStruct((t, MLA_HEADS * MLA_V_DIM), BF16),
        scratch_shapes=[pltpu.VMEM((seq, 2 * pair_v), BF16), pltpu.VMEM((2, ATTN_TQ, seq), F32),
                        pltpu.VMEM((2, ATTN_TQ, seq), BF16)],
        compiler_params=_params(2),
        name="mla_attn",
    )(q, k, v)


def _log_sigmoid(x):
    return jnp.minimum(x, 0.0) - jnp.log1p(jnp.exp(-jnp.abs(x)))


def _chunk_scan(x, lane_in_chunk, chunk, combine, fill):
    step = 1
    while step < chunk:
        x = combine(x, jnp.where(lane_in_chunk >= step, pltpu.roll(x, step, axis=1), fill))
        step *= 2
    return x


def _mlstm_kernel(xc_ref, vm_ref, og_ref, if_ref, wq_ref, wk_ref, wkt_ref, hn_ref, o_ref,
                  q_scr, k_scr, kt_scr, vaug_scr, ct_scr, b_scr, src_scr, mi_scr, *, seq, chunk):
    dh = MLSTM_HEAD_DIM
    nh = MLSTM_HEADS
    nc = seq // chunk
    ones_col = jnp.where(lax.broadcasted_iota(jnp.int32, (seq, dh), 1) == 0, 1.0, 0.0).astype(BF16)
    for h in range(nh):
        hs = slice(h * dh, (h + 1) * dh)
        xc = xc_ref[:, hs]
        q_scr[h] = _dot(xc, wq_ref[h]).astype(BF16)
        k_scr[h] = (_dot(xc, wk_ref[h]) * (dh ** -0.5)).astype(BF16)
        kt = lax.dot_general(wkt_ref[h], xc, (((1,), (1,)), ((), ())),
                             preferred_element_type=F32) * (dh ** -0.5)
        for c in range(nc):
            kt_scr[h * nc + c] = kt[:, c * chunk:(c + 1) * chunk]
        vaug_scr[h, :, 0:dh] = vm_ref[:, hs]
        vaug_scr[h, :, dh:2 * dh] = ones_col
        ct_scr[h] = jnp.zeros((dh, 2 * dh), F32)

    gates_t = if_ref[...]
    lane_in_chunk = lax.broadcasted_iota(jnp.int32, (GATE_ROWS, seq), 1) % chunk
    b_all = _chunk_scan(_log_sigmoid(gates_t) * LOG2_E, lane_in_chunk, chunk, jnp.add, 0.0)
    b_all = pltpu.roll(b_all, nh, axis=0)
    src_all = gates_t * LOG2_E - b_all
    mi_all = b_all + _chunk_scan(src_all, lane_in_chunk, chunk, jnp.maximum, -jnp.inf)
    for c in range(nc):
        cs = slice(c * chunk, (c + 1) * chunk)
        b_scr[c] = b_all[:, cs]
        src_scr[c] = src_all[:, cs]
        mi_scr[c] = mi_all[:, cs]

    r_i = lax.broadcasted_iota(jnp.int32, (chunk, chunk), 0)
    c_i = lax.broadcasted_iota(jnp.int32, (chunk, chunk), 1)
    eye = r_i == c_i
    tril = c_i <= r_i
    last_lane = lax.broadcasted_iota(jnp.int32, (GATE_ROWS, chunk), 1) == chunk - 1

    def to_col(row):
        return jnp.sum(jnp.where(eye, row, 0.0), axis=1, keepdims=True)

    def chunk_step(c, m_prev):
        r0 = pl.multiple_of(c * chunk, chunk)
        b_c = b_scr[c]
        src_c = src_scr[c]
        g = jnp.sum(jnp.where(last_lane, b_c, 0.0), axis=1, keepdims=True)
        m_new = jnp.maximum(g + m_prev, jnp.max(g + src_c, axis=1, keepdims=True))
        decay = jnp.exp2(g + m_prev - m_new)
        e_rows = jnp.exp2(g + src_c - m_new)
        m_t = jnp.maximum(b_c + m_prev, mi_scr[c])
        u_rows = b_c - m_t
        nm_rows = jnp.exp2(-m_t)
        for h in range(nh):
            hs = slice(h * dh, (h + 1) * dh)
            row = slice(h, h + 1)
            u_col = to_col(u_rows[row])
            q_c = q_scr[h, pl.ds(r0, chunk), :]
            v_c = vaug_scr[h, pl.ds(r0, chunk), :]
            s_qk = lax.dot_general(q_c, k_scr[h, pl.ds(r0, chunk), :], (((1,), (1,)), ((), ())),
                                   preferred_element_type=F32)
            d_mat = jnp.exp2(jnp.where(tril, u_col + src_c[row], -jnp.inf))
            q_in = (q_c.astype(F32) * jnp.exp2(u_col + m_prev[row])).astype(BF16)
            ct = ct_scr[h]
            tot = _dot(q_in, ct.astype(BF16)) + _dot((d_mat * s_qk).astype(BF16), v_c)
            den = jnp.maximum(jnp.abs(tot[:, dh:dh + 1]), to_col(nm_rows[row]))
            hh = tot[:, 0:dh] * (1.0 / den) * og_ref[pl.ds(r0, chunk), hs].astype(F32)
            o_ref[pl.ds(r0, chunk), hs] = _rms(hh, hn_ref[:, hs]).astype(BF16)
            c_loc = _dot((kt_scr[h * nc + c] * e_rows[row]).astype(BF16), v_c)
            ct_scr[h] = decay[row] * ct + c_loc
        return m_new

    lax.fori_loop(0, nc, chunk_step, jnp.zeros((GATE_ROWS, 1), F32), unroll=2)


def _mlstm(xc, vm, og, ifg, wq, wk, head_norm, bsz, seq):
    t = xc.shape[0]
    dh = MLSTM_HEAD_DIM
    nh = MLSTM_HEADS
    nc = seq // MLSTM_CHUNK
    blk = lambda b: (b, 0)
    return pl.pallas_call(
        functools.partial(_mlstm_kernel, seq=seq, chunk=MLSTM_CHUNK),
        grid=(bsz,),
        in_specs=[pl.BlockSpec((seq, MLSTM_WIDTH), blk), pl.BlockSpec((seq, MLSTM_WIDTH), blk),
                  pl.BlockSpec((seq, MLSTM_WIDTH), blk),
                  pl.BlockSpec((GATE_ROWS, seq), lambda b: (0, b)),
                  _resident((nh, dh, dh)), _resident((nh, dh, dh)), _resident((nh, dh, dh)),
                  _resident((1, MLSTM_WIDTH))],
        out_specs=pl.BlockSpec((seq, MLSTM_WIDTH), blk),
        out_shape=jax.ShapeDtypeStruct((t, MLSTM_WIDTH), BF16),
        scratch_shapes=[pltpu.VMEM((nh, seq, dh), BF16),
                        pltpu.VMEM((nh, seq, dh), BF16),
                        pltpu.VMEM((nh * nc, dh, MLSTM_CHUNK), F32),
                        pltpu.VMEM((nh, seq, 2 * dh), BF16),
                        pltpu.VMEM((nh, dh, 2 * dh), F32),
                        pltpu.VMEM((nc, GATE_ROWS, MLSTM_CHUNK), F32),
                        pltpu.VMEM((nc, GATE_ROWS, MLSTM_CHUNK), F32),
                        pltpu.VMEM((nc, GATE_ROWS, MLSTM_CHUNK), F32)],
        compiler_params=_params(1),
        name="mlstm",
    )(xc, vm, og, ifg, wq, wk, jnp.swapaxes(wk, 1, 2), head_norm.reshape(1, -1))


def _mixout_kernel(h_ref, oa_ref, ob_ref, gate_ref, gt_ref, wa_ref, wb_ref, wo_ref, o_ref):
    d = h_ref.shape[1]
    y_a = _dot(oa_ref[...], wa_ref[...])
    y_b = _dot(ob_ref[...], wb_ref[...])
    y = gate_ref[:, 0:d].astype(F32) * y_a + gate_ref[:, d:2 * d].astype(F32) * y_b
    o_ref[...] = h_ref[...] + gt_ref[...] * _dot(y.astype(BF16), wo_ref[...])


def _mix_out(h, oa, ob, gates, mod3, wa, wb, wo, seq):
    t, d = h.shape
    tm = MIX_TM
    row = lambda i: (i, 0)
    return pl.pallas_call(
        _mixout_kernel,
        grid=(t // tm,),
        in_specs=[pl.BlockSpec((tm, d), row), pl.BlockSpec((tm, oa.shape[1]), row),
                  pl.BlockSpec((tm, ob.shape[1]), row), pl.BlockSpec((tm, N_BRANCHES * d), row),
                  _mod_spec(5, tm, seq, d),
                  _resident(wa.shape), _resident(wb.shape), _resident(wo.shape)],
        out_specs=pl.BlockSpec((tm, d), row),
        out_shape=jax.ShapeDtypeStruct((t, d), F32),
        compiler_params=_params(1),
        name="mix_out",
    )(h, oa, ob, gates, mod3, wa, wb, wo)


def _pad_cols(w, lo, width):
    return jnp.pad(w, ((0, 0), (lo, width - lo - w.shape[1])))


def _layout_w_in(w_in):
    o = 0
    parts = {}
    for name, n in (("q", MLA_Q_RANK), ("ckv", MLA_KV_RANK), ("kpe", MLA_ROPE_DIM),
                    ("xm", MLSTM_WIDTH), ("vm", MLSTM_WIDTH), ("om", MLSTM_WIDTH),
                    ("i", MLSTM_HEADS), ("f", MLSTM_HEADS), ("gate", None)):
        n = w_in.shape[1] - o if n is None else n
        parts[name] = w_in[:, o:o + n]
        o += n
    gates_if = _pad_cols(jnp.concatenate([parts["i"], parts["f"]], axis=1), 0, V7X_LANES)
    return jnp.concatenate(
        [parts["q"], parts["ckv"], _pad_cols(parts["kpe"], ROPE_LO, HEAD_PAD), parts["xm"],
         parts["vm"], parts["om"], gates_if, parts["gate"]], axis=1).astype(BF16)


def _layout_w_q_b(w_q_b):
    w = w_q_b.reshape(MLA_Q_RANK, MLA_HEADS, MLA_QK_DIM)
    w = jnp.pad(w, ((0, 0), (0, 0), (0, HEAD_PAD - MLA_QK_DIM)))
    return w.reshape(MLA_Q_RANK, MLA_HEADS * HEAD_PAD).astype(BF16)


def _layout_w_kv_b(w_kv_b):
    w = w_kv_b.reshape(MLA_KV_RANK, MLA_HEADS, MLA_NOPE_DIM + MLA_V_DIM)
    wk = jnp.pad(w[:, :, :MLA_NOPE_DIM], ((0, 0), (0, 0), (0, HEAD_PAD - MLA_NOPE_DIM)))
    wv = w[:, :, MLA_NOPE_DIM:]
    return (wk.reshape(MLA_KV_RANK, MLA_HEADS * HEAD_PAD).astype(BF16),
            wv.reshape(MLA_KV_RANK, MLA_HEADS * MLA_V_DIM).astype(BF16))


def kernel(x, c, positions, w_ada, b_ada, norm_ff1, ff1_w_gate, ff1_w_up, ff1_w_down, norm_mix, w_in,
           q_a_norm, w_q_b, kv_a_norm, w_kv_b, conv_w, conv_b, w_q_m, w_k_m, b_i, b_f, mlstm_norm,
           w_mla_out, w_mlstm_out, w_o, norm_ff2, ff2_w_gate, ff2_w_up, ff2_w_down, norm_final):
    bsz, seq, d = x.shape
    t = bsz * seq
    depth = w_ada.shape[0]
    h = x.reshape(t, d)
    pos = positions.reshape(t, 1)
    for l in range(depth):
        mod3 = _adaln(c, w_ada[l], b_ada[l]).reshape(bsz, 1, N_MOD * d)
        h = _ffn(h, mod3, 0, norm_ff1[l], ff1_w_gate[l].astype(BF16), ff1_w_up[l].astype(BF16),
                 ff1_w_down[l].astype(BF16), seq)
        wk_r, wv_r = _layout_w_kv_b(w_kv_b[l])
        bif = _pad_cols(jnp.concatenate([b_i[l], b_f[l]]).reshape(1, -1), 0, V7X_LANES)
        q, k, v, xc, vm, og, ifg, gates = _mix_in(
            h, pos, mod3, norm_mix[l], _layout_w_in(w_in[l]), q_a_norm[l], _layout_w_q_b(w_q_b[l]),
            kv_a_norm[l], wk_r, wv_r, bif, conv_w[l], conv_b[l], seq)
        o_a = _attention(q, k, v, bsz, seq)
        o_b = _mlstm(xc, vm, og, ifg, w_q_m[l].astype(BF16), w_k_m[l].astype(BF16), mlstm_norm[l],
                     bsz, seq)
        h = _mix_out(h, o_a, o_b, gates, mod3, w_mla_out[l].astype(BF16),
                     w_mlstm_out[l].astype(BF16), w_o[l].astype(BF16), seq)
        final_g = norm_final if l == depth - 1 else None
        h = _ffn(h, mod3, 6, norm_ff2[l], ff2_w_gate[l].astype(BF16), ff2_w_up[l].astype(BF16),
                 ff2_w_down[l].astype(BF16), seq, final_g=final_g)
    return h.reshape(bsz, seq, d)
```

```python
import functools

import jax
import jax.numpy as jnp
from jax import lax
from jax.experimental import pallas as pl
from jax.experimental.pallas import tpu as pltpu

F32 = jnp.float32
BF16 = jnp.bfloat16

V7X_LANES = 128
MXU_COLS = 256
V7X_VMEM_BYTES = 64 * 1024 * 1024
VMEM_LIMIT_BYTES = V7X_VMEM_BYTES - 8 * 1024 * 1024

MLA_HEADS = 8
MLA_NOPE_DIM = 64
MLA_ROPE_DIM = 32
MLA_V_DIM = 64
MLA_QK_DIM = MLA_NOPE_DIM + MLA_ROPE_DIM
MLA_Q_RANK = 384
MLA_KV_RANK = 256
ROPE_THETA = 10000.0
MLSTM_HEADS = 4
MLSTM_HEAD_DIM = 128
MLSTM_WIDTH = MLSTM_HEADS * MLSTM_HEAD_DIM
CONV_WIDTH = 4
GATE_ROWS = 2 * MLSTM_HEADS
CONV_HALO = 8
N_BRANCHES = 2
N_MOD = 9
EPS = 1e-6
LOG2_E = 1.4426950408889634

HEAD_PAD = V7X_LANES
ROPE_LO = MLA_NOPE_DIM
ROPE_HALF = MLA_ROPE_DIM // 2

FFN_TM = 1024
FFN_TF = 256
MIX_TM = 512
MIX_LEAD_CHUNKS = 3
ATTN_TQ = 256
ATTN_DEPTH = 2
MLSTM_CHUNK = 128


def _dot(a, b):
    return jnp.dot(a, b, preferred_element_type=F32)


def _rms(x, gain):
    return x * lax.rsqrt(jnp.mean(x * x, axis=-1, keepdims=True) + EPS) * gain


def _params(n_axes, flags=None):
    return pltpu.CompilerParams(
        dimension_semantics=("arbitrary",) * n_axes, vmem_limit_bytes=VMEM_LIMIT_BYTES, flags=flags)


def _resident(shape):
    zeros = (0,) * len(shape)
    return pl.BlockSpec(shape, lambda *_: zeros, pipeline_mode=pl.Buffered(1))


def _mod_spec(chunk, tm, seq, d):
    return pl.BlockSpec((None, 1, d), lambda i: ((i * tm) // seq, 0, chunk))


def _adaln_kernel(c_ref, w_ref, b_ref, o_ref):
    c = c_ref[...]
    sc = (c * jax.nn.sigmoid(c)).astype(BF16)
    o_ref[...] = _dot(sc, w_ref[...].astype(BF16)) + b_ref[...]


def _adaln(c, w_ada, b_ada):
    bsz, d = c.shape
    n = w_ada.shape[1]
    tn = d
    return pl.pallas_call(
        _adaln_kernel,
        grid=(n // tn,),
        in_specs=[pl.BlockSpec((bsz, d), lambda j: (0, 0)),
                  pl.BlockSpec((d, tn), lambda j: (0, j)),
                  pl.BlockSpec((1, tn), lambda j: (0, j))],
        out_specs=pl.BlockSpec((bsz, tn), lambda j: (0, j)),
        out_shape=jax.ShapeDtypeStruct((bsz, n), F32),
        compiler_params=_params(1),
        name="adaln",
    )(c, w_ada, b_ada.reshape(1, n))


def _modulated_norm(x, gain, shift, scale):
    return _rms(x, gain) * (1.0 + scale) + shift


def _ffn_kernel(*refs, d_ff, tf, epilogue):
    x_ref, g_ref, sh_ref, sc_ref, gt_ref, wg_ref, wu_ref, wd_ref = refs[:8]
    if epilogue == "final_norm":
        gf_ref, o_ref, u_scr, a_scr = refs[8:]
    elif epilogue == "next_mod":
        gn_ref, shn_ref, scn_ref, o_ref, un_ref, u_scr, a_scr = refs[8:]
    else:
        o_ref, u_scr, a_scr = refs[8:]
    x = x_ref[...]
    u_scr[...] = _modulated_norm(x, g_ref[...], sh_ref[...], sc_ref[...]).astype(BF16)
    for j in range(d_ff // tf):
        cols = slice(j * tf, (j + 1) * tf)
        g = _dot(u_scr[...], wg_ref[:, cols])
        up = _dot(u_scr[...], wu_ref[:, cols])
        a_scr[:, cols] = (g * jax.nn.sigmoid(g) * up).astype(BF16)
    down = _dot(a_scr[...], wd_ref[...])
    out = x + (0.5 * gt_ref[...]) * down
    if epilogue == "final_norm":
        out = _rms(out, gf_ref[...])
    o_ref[...] = out
    if epilogue == "next_mod":
        un_ref[...] = _modulated_norm(out, gn_ref[...], shn_ref[...], scn_ref[...]).astype(BF16)


def _ffn(h, mod3, chunk0, norm_g, wg, wu, wd, seq, final_g=None, next_norm_g=None, next_chunk0=None):
    t, d = h.shape
    d_ff = wg.shape[1]
    tm = FFN_TM
    row = lambda i: (i, 0)
    epilogue = "final_norm" if final_g is not None else "next_mod" if next_norm_g is not None else "plain"
    in_specs = [pl.BlockSpec((tm, d), row), _resident((1, d)),
                _mod_spec(chunk0, tm, seq, d), _mod_spec(chunk0 + 1, tm, seq, d),
                _mod_spec(chunk0 + 2, tm, seq, d),
                _resident((d, d_ff)), _resident((d, d_ff)), _resident((d_ff, d))]
    args = [h, norm_g.reshape(1, d), mod3, mod3, mod3, wg, wu, wd]
    out_specs = [pl.BlockSpec((tm, d), row)]
    out_shape = [jax.ShapeDtypeStruct((t, d), F32)]
    if epilogue == "final_norm":
        in_specs.append(_resident((1, d)))
        args.append(final_g.reshape(1, d))
    elif epilogue == "next_mod":
        in_specs += [_resident((1, d)), _mod_spec(next_chunk0, tm, seq, d),
                     _mod_spec(next_chunk0 + 1, tm, seq, d)]
        args += [next_norm_g.reshape(1, d), mod3, mod3]
        out_specs.append(pl.BlockSpec((tm, d), row))
        out_shape.append(jax.ShapeDtypeStruct((t, d), BF16))
    return pl.pallas_call(
        functools.partial(_ffn_kernel, d_ff=d_ff, tf=FFN_TF, epilogue=epilogue),
        grid=(t // tm,),
        in_specs=in_specs,
        out_specs=out_specs,
        out_shape=out_shape,
        scratch_shapes=[pltpu.VMEM((tm, d), BF16), pltpu.VMEM((tm, d_ff), BF16)],
        compiler_params=_params(1),
        name="ffn_" + epilogue,
    )(*args)


_C_QLAT = 0
_C_CKV = _C_QLAT + MLA_Q_RANK
_C_KPE = _C_CKV + MLA_KV_RANK
_C_XM = _C_KPE + HEAD_PAD
_C_VM = _C_XM + MLSTM_WIDTH
_C_OM = _C_VM + MLSTM_WIDTH
_C_IF = _C_OM + MLSTM_WIDTH
_C_GATE = _C_IF + V7X_LANES


def _rope_tables(pos_f32):
    lane = lax.broadcasted_iota(jnp.int32, (1, HEAD_PAD), 1)
    in_lo = (lane >= ROPE_LO) & (lane < ROPE_LO + ROPE_HALF)
    in_hi = (lane >= ROPE_LO + ROPE_HALF) & (lane < ROPE_LO + MLA_ROPE_DIM)
    idx = jnp.where(in_hi, lane - (ROPE_LO + ROPE_HALF), lane - ROPE_LO)
    idx = jnp.where(in_lo | in_hi, idx, 0).astype(F32)
    theta = jnp.full((1, HEAD_PAD), ROPE_THETA, F32)
    inv_freq = jnp.exp(-(idx / ROPE_HALF) * jnp.log(theta))
    ang = pos_f32 * inv_freq
    cos = jnp.cos(ang)
    sin = jnp.sin(ang)
    cos_m = jnp.where(lane < ROPE_LO, 1.0, jnp.where(in_lo | in_hi, cos, 0.0))
    sin_up = jnp.where(in_lo, -sin, 0.0)
    sin_dn = jnp.where(in_hi, sin, 0.0)
    return cos_m, sin_up, sin_dn


def _rope_group(x, tables):
    cos_m, sin_up, sin_dn = tables
    up = pltpu.roll(x, HEAD_PAD - ROPE_HALF, axis=1)
    dn = pltpu.roll(x, ROPE_HALF, axis=1)
    return x * cos_m + up * sin_up + dn * sin_dn


def _mixin_kernel(u_ref, pos_ref, win_ref, qan_ref, wq_ref, kvn_ref,
                  wk_ref, wv_ref, bif_ref, cw_ref, cb_ref,
                  q_out, k_out, v_out, xc_out, vm_out, og_out, if_out, gate_out, u_scr, xext_scr,
                  *, tm, seq):
    def col_chunks(width):
        return [slice(lo, min(lo + MXU_COLS, width)) for lo in range(0, width, MXU_COLS)]

    u_scr[...] = u_ref[...]

    def proj(lo, cols):
        return _dot(u_scr[...], win_ref[:, lo + cols.start:lo + cols.stop])

    def gate_chunk(cols):
        gate_out[:, cols] = proj(_C_GATE, cols).astype(BF16)

    gate_cols = col_chunks(gate_out.shape[1])
    q_lat = jnp.concatenate([proj(_C_QLAT, cols) for cols in col_chunks(MLA_Q_RANK)], axis=1)
    c_kv = proj(_C_CKV, slice(0, MLA_KV_RANK))
    k_pe_raw = proj(_C_KPE, slice(0, HEAD_PAD))
    for cols in gate_cols[:MIX_LEAD_CHUNKS]:
        gate_chunk(cols)
    tables = _rope_tables(pos_ref[...].astype(F32))
    q_n = _rms(q_lat, qan_ref[...]).astype(BF16)
    c_n = _rms(c_kv, kvn_ref[...]).astype(BF16)
    k_pe = _rope_group(k_pe_raw, tables)

    q_scale = MLA_QK_DIM ** -0.5 * LOG2_E

    def q_chunk(cols):
        q = _dot(q_n, wq_ref[:, cols])
        for g in range(MXU_COLS // HEAD_PAD):
            grp = slice(g * HEAD_PAD, (g + 1) * HEAD_PAD)
            out = slice(cols.start + grp.start, cols.start + grp.stop)
            q_out[:, out] = (_rope_group(q[:, grp], tables) * q_scale).astype(BF16)

    def k_chunk(cols):
        k_nope = _dot(c_n, wk_ref[:, cols])
        for g in range(MXU_COLS // HEAD_PAD):
            grp = slice(g * HEAD_PAD, (g + 1) * HEAD_PAD)
            out = slice(cols.start + grp.start, cols.start + grp.stop)
            k_out[:, out] = (k_nope[:, grp] + k_pe).astype(BF16)

    def v_chunk(cols):
        v_out[:, cols] = _dot(c_n, wv_ref[:, cols]).astype(BF16)

    @pl.when((pl.program_id(0) * tm) % seq == 0)
    def _():
        xext_scr[0:CONV_HALO, :] = jnp.zeros((CONV_HALO, MLSTM_WIDTH), F32)

    def conv_chunk(cols):
        xext_scr[CONV_HALO:CONV_HALO + tm, cols] = proj(_C_XM, cols)
        conv = cb_ref[:, cols] + (xext_scr[CONV_HALO:CONV_HALO + tm, cols]
                                  * cw_ref[CONV_WIDTH - 1:CONV_WIDTH, cols])
        for j in range(1, CONV_WIDTH):
            conv = conv + (xext_scr[CONV_HALO - j:CONV_HALO - j + tm, cols]
                           * cw_ref[CONV_WIDTH - 1 - j:CONV_WIDTH - j, cols])
        xext_scr[0:CONV_HALO, cols] = xext_scr[tm:tm + CONV_HALO, cols]
        xc_out[:, cols] = (conv * jax.nn.sigmoid(conv)).astype(BF16)

    def vm_chunk(cols):
        vm_out[:, cols] = proj(_C_VM, cols).astype(BF16)

    def og_chunk(cols):
        og_out[:, cols] = jax.nn.sigmoid(proj(_C_OM, cols)).astype(BF16)

    light = ([(gate_chunk, c) for c in gate_cols[MIX_LEAD_CHUNKS:]]
             + [(vm_chunk, c) for c in col_chunks(MLSTM_WIDTH)]
             + [(v_chunk, c) for c in col_chunks(MLA_HEADS * MLA_V_DIM)])
    heavy = ([(q_chunk, c) for c in col_chunks(MLA_HEADS * HEAD_PAD)]
             + [(conv_chunk, c) for c in col_chunks(MLSTM_WIDTH)]
             + [(og_chunk, c) for c in col_chunks(MLSTM_WIDTH)]
             + [(k_chunk, c) for c in col_chunks(MLA_HEADS * HEAD_PAD)])
    for n in range(max(len(light), len(heavy))):
        for items in (light, heavy):
            if n < len(items):
                fn, cols = items[n]
                fn(cols)
    if_out[...] = (proj(_C_IF, slice(0, V7X_LANES)) + bif_ref[...]).T[0:GATE_ROWS, :]


def _mix_in(u, pos, win_r, qan, wq_r, kvn, wk_r, wv_r, bif, conv_w, conv_b, seq):
    t, d = u.shape
    tm = MIX_TM
    row = lambda i: (i, 0)
    n_in = win_r.shape[1]
    qk_w = MLA_HEADS * HEAD_PAD
    v_w = MLA_HEADS * MLA_V_DIM
    outs = [(qk_w, BF16), (qk_w, BF16), (v_w, BF16), (MLSTM_WIDTH, BF16), (MLSTM_WIDTH, BF16),
            (MLSTM_WIDTH, BF16), None, (N_BRANCHES * d, BF16)]
    out_specs = [pl.BlockSpec((tm, o[0]), row) if o else pl.BlockSpec((GATE_ROWS, tm), lambda i: (0, i))
                 for o in outs]
    out_shape = [jax.ShapeDtypeStruct((t, o[0]), o[1]) if o else jax.ShapeDtypeStruct((GATE_ROWS, t), F32)
                 for o in outs]
    return pl.pallas_call(
        functools.partial(_mixin_kernel, tm=tm, seq=seq),
        grid=(t // tm,),
        in_specs=[pl.BlockSpec((tm, d), row), pl.BlockSpec((tm, 1), row),
                  _resident((d, n_in)), _resident((1, MLA_Q_RANK)),
                  _resident((MLA_Q_RANK, qk_w)), _resident((1, MLA_KV_RANK)),
                  _resident((MLA_KV_RANK, qk_w)), _resident((MLA_KV_RANK, v_w)),
                  _resident((1, V7X_LANES)), _resident((CONV_WIDTH, MLSTM_WIDTH)),
                  _resident((1, MLSTM_WIDTH))],
        out_specs=out_specs,
        out_shape=out_shape,
        scratch_shapes=[pltpu.VMEM((tm, d), BF16), pltpu.VMEM((tm + CONV_HALO, MLSTM_WIDTH), F32)],
        compiler_params=_params(1),
        name="mix_in",
    )(u, pos, win_r, qan.reshape(1, -1), wq_r, kvn.reshape(1, -1), wk_r, wv_r, bif, conv_w,
      conv_b.reshape(1, -1))


def _attn_kernel(q_ref, k_ref, v_ref, o_ref, vaug_scr, s_scr, *, seq, tq):
    pair_v = 2 * MLA_V_DIM
    lane = lax.broadcasted_iota(jnp.int32, (1, pair_v), 1)
    vaug_scr[:, 0:pair_v] = v_ref[...]
    vaug_scr[:, pair_v:2 * pair_v] = jnp.where(
        lax.broadcasted_iota(jnp.int32, (seq, pair_v), 1) == 0, 1.0, 0.0).astype(BF16)
    r_i = lax.broadcasted_iota(jnp.int32, (tq, tq), 0)
    c_i = lax.broadcasted_iota(jnp.int32, (tq, tq), 1)
    causal = c_i <= r_i
    neg = jnp.finfo(F32).min
    def buffer(qi, j):
        return (qi % ATTN_DEPTH) * 2 + j

    def scores(qi):
        rows = slice(qi * tq, (qi + 1) * tq)
        maxima = []
        for j in range(2):
            grp = slice(j * HEAD_PAD, (j + 1) * HEAD_PAD)
            qh = q_ref[rows, grp]
            m_vec = None
            for cj in range(qi + 1):
                cols = slice(cj * tq, (cj + 1) * tq)
                s = lax.dot_general(qh, k_ref[cols, grp], (((1,), (1,)), ((), ())),
                                    preferred_element_type=F32)
                if cj == qi:
                    s = jnp.where(causal, s, neg)
                s_scr[buffer(qi, j), :, cols] = s
                for g in range(tq // V7X_LANES):
                    part = s[:, g * V7X_LANES:(g + 1) * V7X_LANES]
                    m_vec = part if m_vec is None else jnp.maximum(m_vec, part)
            maxima.append(jnp.max(m_vec, axis=-1, keepdims=True))
        return maxima

    def outputs(qi, maxima):
        outs = []
        for j in range(2):
            o_aug = None
            for cj in range(qi + 1):
                cols = slice(cj * tq, (cj + 1) * tq)
                p = jnp.exp2(s_scr[buffer(qi, j), :, cols] - maxima[j]).astype(BF16)
                part = _dot(p, vaug_scr[cols, :])
                o_aug = part if o_aug is None else o_aug + part
            outs.append(o_aug[:, 0:pair_v] * (1.0 / o_aug[:, pair_v:pair_v + 1]))
        o_ref[qi * tq:(qi + 1) * tq, :] = jnp.where(lane < MLA_V_DIM, outs[0], outs[1]).astype(BF16)

    order = list(reversed(range(seq // tq)))
    maxima = scores(order[0])
    for n, qi in enumerate(order):
        next_maxima = scores(order[n + 1]) if n + 1 < len(order) else None
        outputs(qi, maxima)
        maxima = next_maxima


def _attention(q, k, v, bsz, seq):
    t = q.shape[0]
    pair_qk = 2 * HEAD_PAD
    pair_v = 2 * MLA_V_DIM
    blk = lambda b, g: (b, g)
    return pl.pallas_call(
        functools.partial(_attn_kernel, seq=seq, tq=ATTN_TQ),
        grid=(bsz, MLA_HEADS // 2),
        in_specs=[pl.BlockSpec((seq, pair_qk), blk), pl.BlockSpec((seq, pair_qk), blk),
                  pl.BlockSpec((seq, pair_v), blk)],
        out_specs=pl.BlockSpec((seq, pair_v), blk),
        out_shape=jax.ShapeDtypeStruct((t, MLA_HEADS * MLA_V_DIM), BF16),
        scratch_shapes=[pltpu.VMEM((seq, 2 * pair_v), BF16),
                        pltpu.VMEM((2 * ATTN_DEPTH, ATTN_TQ, seq), F32)],
        compiler_params=_params(2),
        name="mla_attn",
    )(q, k, v)


def _log_sigmoid(x):
    return jnp.minimum(x, 0.0) - jnp.log1p(jnp.exp(-jnp.abs(x)))


def _chunk_scan(x, lane_in_chunk, chunk, combine, fill):
    step = 1
    while step < chunk:
        x = combine(x, jnp.where(lane_in_chunk >= step, pltpu.roll(x, step, axis=1), fill))
        step *= 2
    return x


def _mlstm_kernel(xc_ref, vm_ref, og_ref, if_ref, wq_ref, wk_ref, wkt_ref, hn_ref, o_ref,
                  q_scr, k_scr, kt_scr, vaug_scr, ct_scr, b_scr, src_scr, mi_scr, *, seq, chunk):
    dh = MLSTM_HEAD_DIM
    nh = MLSTM_HEADS
    nc = seq // chunk
    ones_col = jnp.where(lax.broadcasted_iota(jnp.int32, (seq, dh), 1) == 0, 1.0, 0.0).astype(BF16)
    for h in range(nh):
        hs = slice(h * dh, (h + 1) * dh)
        xc = xc_ref[:, hs]
        q_scr[h] = _dot(xc, wq_ref[h]).astype(BF16)
        k_scr[h] = (_dot(xc, wk_ref[h]) * (dh ** -0.5)).astype(BF16)
        kt = lax.dot_general(wkt_ref[h], xc, (((1,), (1,)), ((), ())),
                             preferred_element_type=F32) * (dh ** -0.5)
        for c in range(nc):
            kt_scr[h * nc + c] = kt[:, c * chunk:(c + 1) * chunk]
        vaug_scr[h, :, 0:dh] = vm_ref[:, hs]
        vaug_scr[h, :, dh:2 * dh] = ones_col
        ct_scr[h] = jnp.zeros((dh, 2 * dh), F32)

    gates_t = if_ref[...]
    lane_in_chunk = lax.broadcasted_iota(jnp.int32, (GATE_ROWS, seq), 1) % chunk
    b_all = _chunk_scan(_log_sigmoid(gates_t) * LOG2_E, lane_in_chunk, chunk, jnp.add, 0.0)
    b_all = pltpu.roll(b_all, nh, axis=0)
    src_all = gates_t * LOG2_E - b_all
    mi_all = b_all + _chunk_scan(src_all, lane_in_chunk, chunk, jnp.maximum, -jnp.inf)
    for c in range(nc):
        cs = slice(c * chunk, (c + 1) * chunk)
        b_scr[c] = b_all[:, cs]
        src_scr[c] = src_all[:, cs]
        mi_scr[c] = mi_all[:, cs]

    r_i = lax.broadcasted_iota(jnp.int32, (chunk, chunk), 0)
    c_i = lax.broadcasted_iota(jnp.int32, (chunk, chunk), 1)
    eye = r_i == c_i
    tril = c_i <= r_i
    last_lane = lax.broadcasted_iota(jnp.int32, (GATE_ROWS, chunk), 1) == chunk - 1

    def to_col(row):
        return jnp.sum(jnp.where(eye, row, 0.0), axis=1, keepdims=True)

    def chunk_step(c, m_prev):
        r0 = pl.multiple_of(c * chunk, chunk)
        b_c = b_scr[c]
        src_c = src_scr[c]
        g = jnp.sum(jnp.where(last_lane, b_c, 0.0), axis=1, keepdims=True)
        m_new = jnp.maximum(g + m_prev, jnp.max(g + src_c, axis=1, keepdims=True))
        decay = jnp.exp2(g + m_prev - m_new)
        e_rows = jnp.exp2(g + src_c - m_new)
        m_t = jnp.maximum(b_c + m_prev, mi_scr[c])
        u_rows = b_c - m_t
        nm_rows = jnp.exp2(-m_t)
        def intra(h):
            row = slice(h, h + 1)
            u_col = to_col(u_rows[row])
            q_c = q_scr[h, pl.ds(r0, chunk), :]
            s_qk = lax.dot_general(q_c, k_scr[h, pl.ds(r0, chunk), :], (((1,), (1,)), ((), ())),
                                   preferred_element_type=F32)
            d_mat = jnp.exp2(jnp.where(tril, u_col + src_c[row], -jnp.inf))
            q_in = (q_c.astype(F32) * jnp.exp2(u_col + m_prev[row])).astype(BF16)
            return q_in, (d_mat * s_qk).astype(BF16), to_col(nm_rows[row])

        def finish(h, q_in, s_mat, nm_col):
            hs = slice(h * dh, (h + 1) * dh)
            row = slice(h, h + 1)
            v_c = vaug_scr[h, pl.ds(r0, chunk), :]
            ct = ct_scr[h]
            tot = _dot(q_in, ct.astype(BF16)) + _dot(s_mat, v_c)
            den = jnp.maximum(jnp.abs(tot[:, dh:dh + 1]), nm_col)
            hh = tot[:, 0:dh] * (1.0 / den) * og_ref[pl.ds(r0, chunk), hs].astype(F32)
            o_ref[pl.ds(r0, chunk), hs] = _rms(hh, hn_ref[:, hs]).astype(BF16)
            c_loc = _dot((kt_scr[h * nc + c] * e_rows[row]).astype(BF16), v_c)
            ct_scr[h] = decay[row] * ct + c_loc

        staged = intra(0)
        for h in range(nh):
            ahead = intra(h + 1) if h + 1 < nh else None
            finish(h, *staged)
            staged = ahead
        return m_new

    lax.fori_loop(0, nc, chunk_step, jnp.zeros((GATE_ROWS, 1), F32), unroll=2)


def _mlstm(xc, vm, og, ifg, wq, wk, head_norm, bsz, seq):
    t = xc.shape[0]
    dh = MLSTM_HEAD_DIM
    nh = MLSTM_HEADS
    nc = seq // MLSTM_CHUNK
    blk = lambda b: (b, 0)
    return pl.pallas_call(
        functools.partial(_mlstm_kernel, seq=seq, chunk=MLSTM_CHUNK),
        grid=(bsz,),
        in_specs=[pl.BlockSpec((seq, MLSTM_WIDTH), blk), pl.BlockSpec((seq, MLSTM_WIDTH), blk),
                  pl.BlockSpec((seq, MLSTM_WIDTH), blk),
                  pl.BlockSpec((GATE_ROWS, seq), lambda b: (0, b)),
                  _resident((nh, dh, dh)), _resident((nh, dh, dh)), _resident((nh, dh, dh)),
                  _resident((1, MLSTM_WIDTH))],
        out_specs=pl.BlockSpec((seq, MLSTM_WIDTH), blk),
        out_shape=jax.ShapeDtypeStruct((t, MLSTM_WIDTH), BF16),
        scratch_shapes=[pltpu.VMEM((nh, seq, dh), BF16),
                        pltpu.VMEM((nh, seq, dh), BF16),
                        pltpu.VMEM((nh * nc, dh, MLSTM_CHUNK), F32),
                        pltpu.VMEM((nh, seq, 2 * dh), BF16),
                        pltpu.VMEM((nh, dh, 2 * dh), F32),
                        pltpu.VMEM((nc, GATE_ROWS, MLSTM_CHUNK), F32),
                        pltpu.VMEM((nc, GATE_ROWS, MLSTM_CHUNK), F32),
                        pltpu.VMEM((nc, GATE_ROWS, MLSTM_CHUNK), F32)],
        compiler_params=_params(1),
        name="mlstm",
    )(xc, vm, og, ifg, wq, wk, jnp.swapaxes(wk, 1, 2), head_norm.reshape(1, -1))


def _mixout_kernel(h_ref, oa_ref, ob_ref, gate_ref, gt_ref, wa_ref, wb_ref, wo_ref, o_ref):
    d = h_ref.shape[1]
    y_a = _dot(oa_ref[...], wa_ref[...])
    y_b = _dot(ob_ref[...], wb_ref[...])
    y = (jax.nn.sigmoid(gate_ref[:, 0:d].astype(F32)) * y_a
         + jax.nn.sigmoid(gate_ref[:, d:2 * d].astype(F32)) * y_b)
    o_ref[...] = h_ref[...] + gt_ref[...] * _dot(y.astype(BF16), wo_ref[...])


def _mix_out(h, oa, ob, gates, mod3, wa, wb, wo, seq):
    t, d = h.shape
    tm = MIX_TM
    row = lambda i: (i, 0)
    return pl.pallas_call(
        _mixout_kernel,
        grid=(t // tm,),
        in_specs=[pl.BlockSpec((tm, d), row), pl.BlockSpec((tm, oa.shape[1]), row),
                  pl.BlockSpec((tm, ob.shape[1]), row), pl.BlockSpec((tm, N_BRANCHES * d), row),
                  _mod_spec(5, tm, seq, d),
                  _resident(wa.shape), _resident(wb.shape), _resident(wo.shape)],
        out_specs=pl.BlockSpec((tm, d), row),
        out_shape=jax.ShapeDtypeStruct((t, d), F32),
        compiler_params=_params(1),
        name="mix_out",
    )(h, oa, ob, gates, mod3, wa, wb, wo)


def _pad_cols(w, lo, width):
    return jnp.pad(w, ((0, 0), (lo, width - lo - w.shape[1])))


def _layout_w_in(w_in):
    o = 0
    parts = {}
    for name, n in (("q", MLA_Q_RANK), ("ckv", MLA_KV_RANK), ("kpe", MLA_ROPE_DIM),
                    ("xm", MLSTM_WIDTH), ("vm", MLSTM_WIDTH), ("om", MLSTM_WIDTH),
                    ("i", MLSTM_HEADS), ("f", MLSTM_HEADS), ("gate", None)):
        n = w_in.shape[1] - o if n is None else n
        parts[name] = w_in[:, o:o + n]
        o += n
    gates_if = _pad_cols(jnp.concatenate([parts["i"], parts["f"]], axis=1), 0, V7X_LANES)
    return jnp.concatenate(
        [parts["q"], parts["ckv"], _pad_cols(parts["kpe"], ROPE_LO, HEAD_PAD), parts["xm"],
         parts["vm"], parts["om"], gates_if, parts["gate"]], axis=1).astype(BF16)


def _layout_w_q_b(w_q_b):
    w = w_q_b.reshape(MLA_Q_RANK, MLA_HEADS, MLA_QK_DIM)
    w = jnp.pad(w, ((0, 0), (0, 0), (0, HEAD_PAD - MLA_QK_DIM)))
    return w.reshape(MLA_Q_RANK, MLA_HEADS * HEAD_PAD).astype(BF16)


def _layout_w_kv_b(w_kv_b):
    w = w_kv_b.reshape(MLA_KV_RANK, MLA_HEADS, MLA_NOPE_DIM + MLA_V_DIM)
    wk = jnp.pad(w[:, :, :MLA_NOPE_DIM], ((0, 0), (0, 0), (0, HEAD_PAD - MLA_NOPE_DIM)))
    wv = w[:, :, MLA_NOPE_DIM:]
    return (wk.reshape(MLA_KV_RANK, MLA_HEADS * HEAD_PAD).astype(BF16),
            wv.reshape(MLA_KV_RANK, MLA_HEADS * MLA_V_DIM).astype(BF16))


def kernel(x, c, positions, w_ada, b_ada, norm_ff1, ff1_w_gate, ff1_w_up, ff1_w_down, norm_mix, w_in,
           q_a_norm, w_q_b, kv_a_norm, w_kv_b, conv_w, conv_b, w_q_m, w_k_m, b_i, b_f, mlstm_norm,
           w_mla_out, w_mlstm_out, w_o, norm_ff2, ff2_w_gate, ff2_w_up, ff2_w_down, norm_final):
    bsz, seq, d = x.shape
    t = bsz * seq
    depth = w_ada.shape[0]
    h = x.reshape(t, d)
    pos = positions.reshape(t, 1)
    for l in range(depth):
        mod3 = _adaln(c, w_ada[l], b_ada[l]).reshape(bsz, 1, N_MOD * d)
        h, u_mix = _ffn(h, mod3, 0, norm_ff1[l], ff1_w_gate[l].astype(BF16), ff1_w_up[l].astype(BF16),
                        ff1_w_down[l].astype(BF16), seq, next_norm_g=norm_mix[l], next_chunk0=3)
        wk_r, wv_r = _layout_w_kv_b(w_kv_b[l])
        bif = _pad_cols(jnp.concatenate([b_i[l], b_f[l]]).reshape(1, -1), 0, V7X_LANES)
        q, k, v, xc, vm, og, ifg, gates = _mix_in(
            u_mix, pos, _layout_w_in(w_in[l]), q_a_norm[l], _layout_w_q_b(w_q_b[l]),
            kv_a_norm[l], wk_r, wv_r, bif, conv_w[l], conv_b[l], seq)
        o_a = _attention(q, k, v, bsz, seq)
        o_b = _mlstm(xc, vm, og, ifg, w_q_m[l].astype(BF16), w_k_m[l].astype(BF16), mlstm_norm[l],
                     bsz, seq)
        h = _mix_out(h, o_a, o_b, gates, mod3, w_mla_out[l].astype(BF16),
                     w_mlstm_out[l].astype(BF16), w_o[l].astype(BF16), seq)
        final_g = norm_final if l == depth - 1 else None
        h = _ffn(h, mod3, 6, norm_ff2[l], ff2_w_gate[l].astype(BF16), ff2_w_up[l].astype(BF16),
                 ff2_w_down[l].astype(BF16), seq, final_g=final_g)[0]
    return h.reshape(bsz, seq, d)
```

```python
import functools

import jax
import jax.numpy as jnp
from jax import lax
from jax.experimental import pallas as pl
from jax.experimental.pallas import tpu as pltpu

F32 = jnp.float32
BF16 = jnp.bfloat16

V7X_LANES = 128
MXU_COLS = 256
V7X_VMEM_BYTES = 64 * 1024 * 1024
VMEM_LIMIT_BYTES = V7X_VMEM_BYTES - 8 * 1024 * 1024

MLA_HEADS = 8
MLA_NOPE_DIM = 64
MLA_ROPE_DIM = 32
MLA_V_DIM = 64
MLA_QK_DIM = MLA_NOPE_DIM + MLA_ROPE_DIM
MLA_Q_RANK = 384
MLA_KV_RANK = 256
ROPE_THETA = 10000.0
MLSTM_HEADS = 4
MLSTM_HEAD_DIM = 128
MLSTM_WIDTH = MLSTM_HEADS * MLSTM_HEAD_DIM
CONV_WIDTH = 4
GATE_ROWS = 2 * MLSTM_HEADS
CONV_HALO = 8
N_BRANCHES = 2
N_MOD = 9
EPS = 1e-6
LOG2_E = 1.4426950408889634

HEAD_PAD = V7X_LANES
ROPE_LO = MLA_NOPE_DIM
ROPE_HALF = MLA_ROPE_DIM // 2

FFN_TM = 1024
FFN_TM_MIXER = 512
FFN_TF = 256
MIX_TM = 512
MIX_LEAD_CHUNKS = 3
ATTN_TQ = 256
ATTN_DEPTH = 2
MLSTM_CHUNK = 128


def _dot(a, b):
    return jnp.dot(a, b, preferred_element_type=F32)


def _rms(x, gain):
    return x * lax.rsqrt(jnp.mean(x * x, axis=-1, keepdims=True) + EPS) * gain


def _params(n_axes, flags=None):
    return pltpu.CompilerParams(
        dimension_semantics=("arbitrary",) * n_axes, vmem_limit_bytes=VMEM_LIMIT_BYTES, flags=flags)


def _resident(shape):
    zeros = (0,) * len(shape)
    return pl.BlockSpec(shape, lambda *_: zeros, pipeline_mode=pl.Buffered(1))


def _mod_spec(chunk, tm, seq, d):
    return pl.BlockSpec((None, 1, d), lambda i: ((i * tm) // seq, 0, chunk))


def _adaln_kernel(c_ref, w_ref, b_ref, o_ref):
    c = c_ref[...]
    sc = (c * jax.nn.sigmoid(c)).astype(BF16)
    o_ref[...] = _dot(sc, w_ref[...].astype(BF16)) + b_ref[...]


def _adaln(c, w_ada, b_ada):
    bsz, d = c.shape
    n = w_ada.shape[1]
    tn = d
    return pl.pallas_call(
        _adaln_kernel,
        grid=(n // tn,),
        in_specs=[pl.BlockSpec((bsz, d), lambda j: (0, 0)),
                  pl.BlockSpec((d, tn), lambda j: (0, j)),
                  pl.BlockSpec((1, tn), lambda j: (0, j))],
        out_specs=pl.BlockSpec((bsz, tn), lambda j: (0, j)),
        out_shape=jax.ShapeDtypeStruct((bsz, n), F32),
        compiler_params=_params(1),
        name="adaln",
    )(c, w_ada, b_ada.reshape(1, n))


def _modulated_norm(x, gain, shift, scale):
    return _rms(x, gain) * (1.0 + scale) + shift


def _ffn_kernel(*refs, d_ff, tf, mixer_prologue, epilogue):
    refs = list(refs)
    if mixer_prologue:
        h_ref, oa_ref, ob_ref, bg_ref, gtm_ref, wa_ref, wb_ref, wo_ref = refs[:8]
        refs = refs[8:]
    else:
        x_ref = refs.pop(0)
    g_ref, sh_ref, sc_ref, gt_ref, wg_ref, wu_ref, wd_ref = refs[:7]
    refs = refs[7:]
    if epilogue == "final_norm":
        gf_ref = refs.pop(0)
    elif epilogue == "next_mod":
        gn_ref, shn_ref, scn_ref = refs[:3]
        refs = refs[3:]
    o_ref = refs.pop(0)
    if epilogue == "next_mod":
        un_ref = refs.pop(0)
    u_scr, a_scr = refs[:2]

    if mixer_prologue:
        d = h_ref.shape[1]
        y_a = _dot(oa_ref[...], wa_ref[...])
        y_b = _dot(ob_ref[...], wb_ref[...])
        y = (jax.nn.sigmoid(bg_ref[:, 0:d].astype(F32)) * y_a
             + jax.nn.sigmoid(bg_ref[:, d:2 * d].astype(F32)) * y_b)
        x_ref = refs[2]
        x_ref[...] = h_ref[...] + gtm_ref[...] * _dot(y.astype(BF16), wo_ref[...])

    u_scr[...] = _modulated_norm(x_ref[...], g_ref[...], sh_ref[...], sc_ref[...]).astype(BF16)
    for j in range(d_ff // tf):
        cols = slice(j * tf, (j + 1) * tf)
        g = _dot(u_scr[...], wg_ref[:, cols])
        up = _dot(u_scr[...], wu_ref[:, cols])
        a_scr[:, cols] = (g * jax.nn.sigmoid(g) * up).astype(BF16)
    down = _dot(a_scr[...], wd_ref[...])
    out = x_ref[...] + (0.5 * gt_ref[...]) * down
    if epilogue == "final_norm":
        out = _rms(out, gf_ref[...])
    o_ref[...] = out
    if epilogue == "next_mod":
        un_ref[...] = _modulated_norm(out, gn_ref[...], shn_ref[...], scn_ref[...]).astype(BF16)


def _ffn(h, mod3, chunk0, norm_g, wg, wu, wd, seq, *, mixer=None, final_g=None, next_norm_g=None,
         next_chunk0=None):
    t, d = h.shape
    d_ff = wg.shape[1]
    tm = FFN_TM_MIXER if mixer is not None else FFN_TM
    row = lambda i: (i, 0)
    epilogue = "final_norm" if final_g is not None else "next_mod" if next_norm_g is not None else "plain"
    in_specs = [pl.BlockSpec((tm, d), row)]
    args = [h]
    scratch = [pltpu.VMEM((tm, d), BF16), pltpu.VMEM((tm, d_ff), BF16)]
    if mixer is not None:
        o_a, o_b, branch_gates, gate_chunk, w_a, w_b, w_o = mixer
        in_specs += [pl.BlockSpec((tm, o_a.shape[1]), row), pl.BlockSpec((tm, o_b.shape[1]), row),
                     pl.BlockSpec((tm, branch_gates.shape[1]), row),
                     _mod_spec(gate_chunk, tm, seq, d),
                     _resident(w_a.shape), _resident(w_b.shape), _resident(w_o.shape)]
        args += [o_a, o_b, branch_gates, mod3, w_a, w_b, w_o]
        scratch.append(pltpu.VMEM((tm, d), F32))
    in_specs += [_resident((1, d)), _mod_spec(chunk0, tm, seq, d), _mod_spec(chunk0 + 1, tm, seq, d),
                 _mod_spec(chunk0 + 2, tm, seq, d),
                 _resident((d, d_ff)), _resident((d, d_ff)), _resident((d_ff, d))]
    args += [norm_g.reshape(1, d), mod3, mod3, mod3, wg, wu, wd]
    out_specs = [pl.BlockSpec((tm, d), row)]
    out_shape = [jax.ShapeDtypeStruct((t, d), F32)]
    if epilogue == "final_norm":
        in_specs.append(_resident((1, d)))
        args.append(final_g.reshape(1, d))
    elif epilogue == "next_mod":
        in_specs += [_resident((1, d)), _mod_spec(next_chunk0, tm, seq, d),
                     _mod_spec(next_chunk0 + 1, tm, seq, d)]
        args += [next_norm_g.reshape(1, d), mod3, mod3]
        out_specs.append(pl.BlockSpec((tm, d), row))
        out_shape.append(jax.ShapeDtypeStruct((t, d), BF16))
    return pl.pallas_call(
        functools.partial(_ffn_kernel, d_ff=d_ff, tf=FFN_TF, mixer_prologue=mixer is not None,
                          epilogue=epilogue),
        grid=(t // tm,),
        in_specs=in_specs,
        out_specs=out_specs,
        out_shape=out_shape,
        scratch_shapes=scratch,
        compiler_params=_params(1),
        name=("mixout_ffn_" if mixer is not None else "ffn_") + epilogue,
    )(*args)


_C_QLAT = 0
_C_CKV = _C_QLAT + MLA_Q_RANK
_C_KPE = _C_CKV + MLA_KV_RANK
_C_XM = _C_KPE + HEAD_PAD
_C_VM = _C_XM + MLSTM_WIDTH
_C_OM = _C_VM + MLSTM_WIDTH
_C_IF = _C_OM + MLSTM_WIDTH
_C_GATE = _C_IF + V7X_LANES


def _rope_tables(pos_f32):
    lane = lax.broadcasted_iota(jnp.int32, (1, HEAD_PAD), 1)
    in_lo = (lane >= ROPE_LO) & (lane < ROPE_LO + ROPE_HALF)
    in_hi = (lane >= ROPE_LO + ROPE_HALF) & (lane < ROPE_LO + MLA_ROPE_DIM)
    idx = jnp.where(in_hi, lane - (ROPE_LO + ROPE_HALF), lane - ROPE_LO)
    idx = jnp.where(in_lo | in_hi, idx, 0).astype(F32)
    theta = jnp.full((1, HEAD_PAD), ROPE_THETA, F32)
    inv_freq = jnp.exp(-(idx / ROPE_HALF) * jnp.log(theta))
    ang = pos_f32 * inv_freq
    cos = jnp.cos(ang)
    sin = jnp.sin(ang)
    cos_m = jnp.where(lane < ROPE_LO, 1.0, jnp.where(in_lo | in_hi, cos, 0.0))
    sin_up = jnp.where(in_lo, -sin, 0.0)
    sin_dn = jnp.where(in_hi, sin, 0.0)
    return cos_m, sin_up, sin_dn


def _rope_group(x, tables):
    cos_m, sin_up, sin_dn = tables
    up = pltpu.roll(x, HEAD_PAD - ROPE_HALF, axis=1)
    dn = pltpu.roll(x, ROPE_HALF, axis=1)
    return x * cos_m + up * sin_up + dn * sin_dn


def _mixin_kernel(u_ref, pos_ref, win_ref, qan_ref, wq_ref, kvn_ref,
                  wk_ref, wv_ref, bif_ref, cw_ref, cb_ref,
                  q_out, k_out, v_out, xc_out, vm_out, og_out, if_out, gate_out, u_scr, xext_scr,
                  *, tm, seq):
    def col_chunks(width):
        return [slice(lo, min(lo + MXU_COLS, width)) for lo in range(0, width, MXU_COLS)]

    u_scr[...] = u_ref[...]

    def proj(lo, cols):
        return _dot(u_scr[...], win_ref[:, lo + cols.start:lo + cols.stop])

    def gate_chunk(cols):
        gate_out[:, cols] = proj(_C_GATE, cols).astype(BF16)

    gate_cols = col_chunks(gate_out.shape[1])
    q_lat = jnp.concatenate([proj(_C_QLAT, cols) for cols in col_chunks(MLA_Q_RANK)], axis=1)
    c_kv = proj(_C_CKV, slice(0, MLA_KV_RANK))
    k_pe_raw = proj(_C_KPE, slice(0, HEAD_PAD))
    for cols in gate_cols[:MIX_LEAD_CHUNKS]:
        gate_chunk(cols)
    tables = _rope_tables(pos_ref[...].astype(F32))
    q_n = _rms(q_lat, qan_ref[...]).astype(BF16)
    c_n = _rms(c_kv, kvn_ref[...]).astype(BF16)
    k_pe = _rope_group(k_pe_raw, tables)

    q_scale = MLA_QK_DIM ** -0.5 * LOG2_E

    def q_chunk(cols):
        q = _dot(q_n, wq_ref[:, cols])
        for g in range(MXU_COLS // HEAD_PAD):
            grp = slice(g * HEAD_PAD, (g + 1) * HEAD_PAD)
            out = slice(cols.start + grp.start, cols.start + grp.stop)
            q_out[:, out] = (_rope_group(q[:, grp], tables) * q_scale).astype(BF16)

    def k_chunk(cols):
        k_nope = _dot(c_n, wk_ref[:, cols])
        for g in range(MXU_COLS // HEAD_PAD):
            grp = slice(g * HEAD_PAD, (g + 1) * HEAD_PAD)
            out = slice(cols.start + grp.start, cols.start + grp.stop)
            k_out[:, out] = (k_nope[:, grp] + k_pe).astype(BF16)

    def v_chunk(cols):
        v_out[:, cols] = _dot(c_n, wv_ref[:, cols]).astype(BF16)

    @pl.when((pl.program_id(0) * tm) % seq == 0)
    def _():
        xext_scr[0:CONV_HALO, :] = jnp.zeros((CONV_HALO, MLSTM_WIDTH), F32)

    def conv_chunk(cols):
        xext_scr[CONV_HALO:CONV_HALO + tm, cols] = proj(_C_XM, cols)
        conv = cb_ref[:, cols] + (xext_scr[CONV_HALO:CONV_HALO + tm, cols]
                                  * cw_ref[CONV_WIDTH - 1:CONV_WIDTH, cols])
        for j in range(1, CONV_WIDTH):
            conv = conv + (xext_scr[CONV_HALO - j:CONV_HALO - j + tm, cols]
                           * cw_ref[CONV_WIDTH - 1 - j:CONV_WIDTH - j, cols])
        xext_scr[0:CONV_HALO, cols] = xext_scr[tm:tm + CONV_HALO, cols]
        xc_out[:, cols] = (conv * jax.nn.sigmoid(conv)).astype(BF16)

    def vm_chunk(cols):
        vm_out[:, cols] = proj(_C_VM, cols).astype(BF16)

    def og_chunk(cols):
        og_out[:, cols] = jax.nn.sigmoid(proj(_C_OM, cols)).astype(BF16)

    light = ([(gate_chunk, c) for c in gate_cols[MIX_LEAD_CHUNKS:]]
             + [(vm_chunk, c) for c in col_chunks(MLSTM_WIDTH)]
             + [(v_chunk, c) for c in col_chunks(MLA_HEADS * MLA_V_DIM)])
    heavy = ([(q_chunk, c) for c in col_chunks(MLA_HEADS * HEAD_PAD)]
             + [(conv_chunk, c) for c in col_chunks(MLSTM_WIDTH)]
             + [(og_chunk, c) for c in col_chunks(MLSTM_WIDTH)]
             + [(k_chunk, c) for c in col_chunks(MLA_HEADS * HEAD_PAD)])
    for n in range(max(len(light), len(heavy))):
        for items in (light, heavy):
            if n < len(items):
                fn, cols = items[n]
                fn(cols)
    if_out[...] = (proj(_C_IF, slice(0, V7X_LANES)) + bif_ref[...]).T[0:GATE_ROWS, :]


def _mix_in(u, pos, win_r, qan, wq_r, kvn, wk_r, wv_r, bif, conv_w, conv_b, seq):
    t, d = u.shape
    tm = MIX_TM
    row = lambda i: (i, 0)
    n_in = win_r.shape[1]
    qk_w = MLA_HEADS * HEAD_PAD
    v_w = MLA_HEADS * MLA_V_DIM
    outs = [(qk_w, BF16), (qk_w, BF16), (v_w, BF16), (MLSTM_WIDTH, BF16), (MLSTM_WIDTH, BF16),
            (MLSTM_WIDTH, BF16), None, (N_BRANCHES * d, BF16)]
    out_specs = [pl.BlockSpec((tm, o[0]), row) if o else pl.BlockSpec((GATE_ROWS, tm), lambda i: (0, i))
                 for o in outs]
    out_shape = [jax.ShapeDtypeStruct((t, o[0]), o[1]) if o else jax.ShapeDtypeStruct((GATE_ROWS, t), F32)
                 for o in outs]
    return pl.pallas_call(
        functools.partial(_mixin_kernel, tm=tm, seq=seq),
        grid=(t // tm,),
        in_specs=[pl.BlockSpec((tm, d), row), pl.BlockSpec((tm, 1), row),
                  _resident((d, n_in)), _resident((1, MLA_Q_RANK)),
                  _resident((MLA_Q_RANK, qk_w)), _resident((1, MLA_KV_RANK)),
                  _resident((MLA_KV_RANK, qk_w)), _resident((MLA_KV_RANK, v_w)),
                  _resident((1, V7X_LANES)), _resident((CONV_WIDTH, MLSTM_WIDTH)),
                  _resident((1, MLSTM_WIDTH))],
        out_specs=out_specs,
        out_shape=out_shape,
        scratch_shapes=[pltpu.VMEM((tm, d), BF16), pltpu.VMEM((tm + CONV_HALO, MLSTM_WIDTH), F32)],
        compiler_params=_params(1),
        name="mix_in",
    )(u, pos, win_r, qan.reshape(1, -1), wq_r, kvn.reshape(1, -1), wk_r, wv_r, bif, conv_w,
      conv_b.reshape(1, -1))


def _attn_kernel(q_ref, k_ref, v_ref, o_ref, vaug_scr, s_scr, *, seq, tq):
    pair_v = 2 * MLA_V_DIM
    lane = lax.broadcasted_iota(jnp.int32, (1, pair_v), 1)
    vaug_scr[:, 0:pair_v] = v_ref[...]
    vaug_scr[:, pair_v:2 * pair_v] = jnp.where(
        lax.broadcasted_iota(jnp.int32, (seq, pair_v), 1) == 0, 1.0, 0.0).astype(BF16)
    r_i = lax.broadcasted_iota(jnp.int32, (tq, tq), 0)
    c_i = lax.broadcasted_iota(jnp.int32, (tq, tq), 1)
    causal = c_i <= r_i
    neg = jnp.finfo(F32).min
    def buffer(qi, j):
        return (qi % ATTN_DEPTH) * 2 + j

    def scores(qi):
        rows = slice(qi * tq, (qi + 1) * tq)
        maxima = []
        for j in range(2):
            grp = slice(j * HEAD_PAD, (j + 1) * HEAD_PAD)
            qh = q_ref[rows, grp]
            m_vec = None
            for cj in range(qi + 1):
                cols = slice(cj * tq, (cj + 1) * tq)
                s = lax.dot_general(qh, k_ref[cols, grp], (((1,), (1,)), ((), ())),
                                    preferred_element_type=F32)
                if cj == qi:
                    s = jnp.where(causal, s, neg)
                s_scr[buffer(qi, j), :, cols] = s
                for g in range(tq // V7X_LANES):
                    part = s[:, g * V7X_LANES:(g + 1) * V7X_LANES]
                    m_vec = part if m_vec is None else jnp.maximum(m_vec, part)
            maxima.append(jnp.max(m_vec, axis=-1, keepdims=True))
        return maxima

    def outputs(qi, maxima):
        outs = []
        for j in range(2):
            o_aug = None
            for cj in range(qi + 1):
                cols = slice(cj * tq, (cj + 1) * tq)
                p = jnp.exp2(s_scr[buffer(qi, j), :, cols] - maxima[j]).astype(BF16)
                part = _dot(p, vaug_scr[cols, :])
                o_aug = part if o_aug is None else o_aug + part
            outs.append(o_aug[:, 0:pair_v] * (1.0 / o_aug[:, pair_v:pair_v + 1]))
        o_ref[qi * tq:(qi + 1) * tq, :] = jnp.where(lane < MLA_V_DIM, outs[0], outs[1]).astype(BF16)

    order = list(reversed(range(seq // tq)))
    maxima = scores(order[0])
    for n, qi in enumerate(order):
        next_maxima = scores(order[n + 1]) if n + 1 < len(order) else None
        outputs(qi, maxima)
        maxima = next_maxima


def _attention(q, k, v, bsz, seq):
    t = q.shape[0]
    pair_qk = 2 * HEAD_PAD
    pair_v = 2 * MLA_V_DIM
    blk = lambda b, g: (b, g)
    return pl.pallas_call(
        functools.partial(_attn_kernel, seq=seq, tq=ATTN_TQ),
        grid=(bsz, MLA_HEADS // 2),
        in_specs=[pl.BlockSpec((seq, pair_qk), blk), pl.BlockSpec((seq, pair_qk), blk),
                  pl.BlockSpec((seq, pair_v), blk)],
        out_specs=pl.BlockSpec((seq, pair_v), blk),
        out_shape=jax.ShapeDtypeStruct((t, MLA_HEADS * MLA_V_DIM), BF16),
        scratch_shapes=[pltpu.VMEM((seq, 2 * pair_v), BF16),
                        pltpu.VMEM((2 * ATTN_DEPTH, ATTN_TQ, seq), F32)],
        compiler_params=_params(2),
        name="mla_attn",
    )(q, k, v)


def _log_sigmoid(x):
    return jnp.minimum(x, 0.0) - jnp.log1p(jnp.exp(-jnp.abs(x)))


def _chunk_scan(x, lane_in_chunk, chunk, combine, fill):
    step = 1
    while step < chunk:
        x = combine(x, jnp.where(lane_in_chunk >= step, pltpu.roll(x, step, axis=1), fill))
        step *= 2
    return x


def _mlstm_kernel(xc_ref, vm_ref, og_ref, if_ref, wq_ref, wk_ref, wkt_ref, hn_ref, o_ref,
                  q_scr, k_scr, kt_scr, vaug_scr, ct_scr, b_scr, src_scr, mi_scr, *, seq, chunk):
    dh = MLSTM_HEAD_DIM
    nh = MLSTM_HEADS
    nc = seq // chunk
    ones_col = jnp.where(lax.broadcasted_iota(jnp.int32, (seq, dh), 1) == 0, 1.0, 0.0).astype(BF16)
    for h in range(nh):
        hs = slice(h * dh, (h + 1) * dh)
        xc = xc_ref[:, hs]
        q_scr[h] = _dot(xc, wq_ref[h]).astype(BF16)
        k_scr[h] = (_dot(xc, wk_ref[h]) * (dh ** -0.5)).astype(BF16)
        kt = lax.dot_general(wkt_ref[h], xc, (((1,), (1,)), ((), ())),
                             preferred_element_type=F32) * (dh ** -0.5)
        for c in range(nc):
            kt_scr[h * nc + c] = kt[:, c * chunk:(c + 1) * chunk]
        vaug_scr[h, :, 0:dh] = vm_ref[:, hs]
        vaug_scr[h, :, dh:2 * dh] = ones_col
        ct_scr[h] = jnp.zeros((dh, 2 * dh), F32)

    gates_t = if_ref[...]
    lane_in_chunk = lax.broadcasted_iota(jnp.int32, (GATE_ROWS, seq), 1) % chunk
    b_all = _chunk_scan(_log_sigmoid(gates_t) * LOG2_E, lane_in_chunk, chunk, jnp.add, 0.0)
    b_all = pltpu.roll(b_all, nh, axis=0)
    src_all = gates_t * LOG2_E - b_all
    mi_all = b_all + _chunk_scan(src_all, lane_in_chunk, chunk, jnp.maximum, -jnp.inf)
    for c in range(nc):
        cs = slice(c * chunk, (c + 1) * chunk)
        b_scr[c] = b_all[:, cs]
        src_scr[c] = src_all[:, cs]
        mi_scr[c] = mi_all[:, cs]

    r_i = lax.broadcasted_iota(jnp.int32, (chunk, chunk), 0)
    c_i = lax.broadcasted_iota(jnp.int32, (chunk, chunk), 1)
    eye = r_i == c_i
    tril = c_i <= r_i
    last_lane = lax.broadcasted_iota(jnp.int32, (GATE_ROWS, chunk), 1) == chunk - 1

    def to_col(row):
        return jnp.sum(jnp.where(eye, row, 0.0), axis=1, keepdims=True)

    def chunk_step(c, m_prev):
        r0 = pl.multiple_of(c * chunk, chunk)
        b_c = b_scr[c]
        src_c = src_scr[c]
        g = jnp.sum(jnp.where(last_lane, b_c, 0.0), axis=1, keepdims=True)
        m_new = jnp.maximum(g + m_prev, jnp.max(g + src_c, axis=1, keepdims=True))
        decay = jnp.exp2(g + m_prev - m_new)
        e_rows = jnp.exp2(g + src_c - m_new)
        m_t = jnp.maximum(b_c + m_prev, mi_scr[c])
        u_rows = b_c - m_t
        nm_rows = jnp.exp2(-m_t)
        for h in range(nh):
            hs = slice(h * dh, (h + 1) * dh)
            row = slice(h, h + 1)
            u_col = to_col(u_rows[row])
            q_c = q_scr[h, pl.ds(r0, chunk), :]
            v_c = vaug_scr[h, pl.ds(r0, chunk), :]
            s_qk = lax.dot_general(q_c, k_scr[h, pl.ds(r0, chunk), :], (((1,), (1,)), ((), ())),
                                   preferred_element_type=F32)
            d_mat = jnp.exp2(jnp.where(tril, u_col + src_c[row], -jnp.inf))
            q_in = (q_c.astype(F32) * jnp.exp2(u_col + m_prev[row])).astype(BF16)
            ct = ct_scr[h]
            tot = _dot(q_in, ct.astype(BF16)) + _dot((d_mat * s_qk).astype(BF16), v_c)
            den = jnp.maximum(jnp.abs(tot[:, dh:dh + 1]), to_col(nm_rows[row]))
            hh = tot[:, 0:dh] * (1.0 / den) * og_ref[pl.ds(r0, chunk), hs].astype(F32)
            o_ref[pl.ds(r0, chunk), hs] = _rms(hh, hn_ref[:, hs]).astype(BF16)
            c_loc = _dot((kt_scr[h * nc + c] * e_rows[row]).astype(BF16), v_c)
            ct_scr[h] = decay[row] * ct + c_loc
        return m_new

    lax.fori_loop(0, nc, chunk_step, jnp.zeros((GATE_ROWS, 1), F32), unroll=2)


def _mlstm(xc, vm, og, ifg, wq, wk, head_norm, bsz, seq):
    t = xc.shape[0]
    dh = MLSTM_HEAD_DIM
    nh = MLSTM_HEADS
    nc = seq // MLSTM_CHUNK
    blk = lambda b: (b, 0)
    return pl.pallas_call(
        functools.partial(_mlstm_kernel, seq=seq, chunk=MLSTM_CHUNK),
        grid=(bsz,),
        in_specs=[pl.BlockSpec((seq, MLSTM_WIDTH), blk), pl.BlockSpec((seq, MLSTM_WIDTH), blk),
                  pl.BlockSpec((seq, MLSTM_WIDTH), blk),
                  pl.BlockSpec((GATE_ROWS, seq), lambda b: (0, b)),
                  _resident((nh, dh, dh)), _resident((nh, dh, dh)), _resident((nh, dh, dh)),
                  _resident((1, MLSTM_WIDTH))],
        out_specs=pl.BlockSpec((seq, MLSTM_WIDTH), blk),
        out_shape=jax.ShapeDtypeStruct((t, MLSTM_WIDTH), BF16),
        scratch_shapes=[pltpu.VMEM((nh, seq, dh), BF16),
                        pltpu.VMEM((nh, seq, dh), BF16),
                        pltpu.VMEM((nh * nc, dh, MLSTM_CHUNK), F32),
                        pltpu.VMEM((nh, seq, 2 * dh), BF16),
                        pltpu.VMEM((nh, dh, 2 * dh), F32),
                        pltpu.VMEM((nc, GATE_ROWS, MLSTM_CHUNK), F32),
                        pltpu.VMEM((nc, GATE_ROWS, MLSTM_CHUNK), F32),
                        pltpu.VMEM((nc, GATE_ROWS, MLSTM_CHUNK), F32)],
        compiler_params=_params(1),
        name="mlstm",
    )(xc, vm, og, ifg, wq, wk, jnp.swapaxes(wk, 1, 2), head_norm.reshape(1, -1))


def _pad_cols(w, lo, width):
    return jnp.pad(w, ((0, 0), (lo, width - lo - w.shape[1])))


def _layout_w_in(w_in):
    o = 0
    parts = {}
    for name, n in (("q", MLA_Q_RANK), ("ckv", MLA_KV_RANK), ("kpe", MLA_ROPE_DIM),
                    ("xm", MLSTM_WIDTH), ("vm", MLSTM_WIDTH), ("om", MLSTM_WIDTH),
                    ("i", MLSTM_HEADS), ("f", MLSTM_HEADS), ("gate", None)):
        n = w_in.shape[1] - o if n is None else n
        parts[name] = w_in[:, o:o + n]
        o += n
    gates_if = _pad_cols(jnp.concatenate([parts["i"], parts["f"]], axis=1), 0, V7X_LANES)
    return jnp.concatenate(
        [parts["q"], parts["ckv"], _pad_cols(parts["kpe"], ROPE_LO, HEAD_PAD), parts["xm"],
         parts["vm"], parts["om"], gates_if, parts["gate"]], axis=1).astype(BF16)


def _layout_w_q_b(w_q_b):
    w = w_q_b.reshape(MLA_Q_RANK, MLA_HEADS, MLA_QK_DIM)
    w = jnp.pad(w, ((0, 0), (0, 0), (0, HEAD_PAD - MLA_QK_DIM)))
    return w.reshape(MLA_Q_RANK, MLA_HEADS * HEAD_PAD).astype(BF16)


def _layout_w_kv_b(w_kv_b):
    w = w_kv_b.reshape(MLA_KV_RANK, MLA_HEADS, MLA_NOPE_DIM + MLA_V_DIM)
    wk = jnp.pad(w[:, :, :MLA_NOPE_DIM], ((0, 0), (0, 0), (0, HEAD_PAD - MLA_NOPE_DIM)))
    wv = w[:, :, MLA_NOPE_DIM:]
    return (wk.reshape(MLA_KV_RANK, MLA_HEADS * HEAD_PAD).astype(BF16),
            wv.reshape(MLA_KV_RANK, MLA_HEADS * MLA_V_DIM).astype(BF16))


def kernel(x, c, positions, w_ada, b_ada, norm_ff1, ff1_w_gate, ff1_w_up, ff1_w_down, norm_mix, w_in,
           q_a_norm, w_q_b, kv_a_norm, w_kv_b, conv_w, conv_b, w_q_m, w_k_m, b_i, b_f, mlstm_norm,
           w_mla_out, w_mlstm_out, w_o, norm_ff2, ff2_w_gate, ff2_w_up, ff2_w_down, norm_final):
    bsz, seq, d = x.shape
    t = bsz * seq
    depth = w_ada.shape[0]
    h = x.reshape(t, d)
    pos = positions.reshape(t, 1)
    for l in range(depth):
        mod3 = _adaln(c, w_ada[l], b_ada[l]).reshape(bsz, 1, N_MOD * d)
        h, u_mix = _ffn(h, mod3, 0, norm_ff1[l], ff1_w_gate[l].astype(BF16), ff1_w_up[l].astype(BF16),
                        ff1_w_down[l].astype(BF16), seq, next_norm_g=norm_mix[l], next_chunk0=3)
        wk_r, wv_r = _layout_w_kv_b(w_kv_b[l])
        bif = _pad_cols(jnp.concatenate([b_i[l], b_f[l]]).reshape(1, -1), 0, V7X_LANES)
        q, k, v, xc, vm, og, ifg, gates = _mix_in(
            u_mix, pos, _layout_w_in(w_in[l]), q_a_norm[l], _layout_w_q_b(w_q_b[l]),
            kv_a_norm[l], wk_r, wv_r, bif, conv_w[l], conv_b[l], seq)
        o_a = _attention(q, k, v, bsz, seq)
        o_b = _mlstm(xc, vm, og, ifg, w_q_m[l].astype(BF16), w_k_m[l].astype(BF16), mlstm_norm[l],
                     bsz, seq)
        mixer = (o_a, o_b, gates, 5, w_mla_out[l].astype(BF16), w_mlstm_out[l].astype(BF16),
                 w_o[l].astype(BF16))
        final_g = norm_final if l == depth - 1 else None
        h = _ffn(h, mod3, 6, norm_ff2[l], ff2_w_gate[l].astype(BF16), ff2_w_up[l].astype(BF16),
                 ff2_w_down[l].astype(BF16), seq, mixer=mixer, final_g=final_g)[0]
    return h.reshape(bsz, seq, d)
```

```python
import functools

import jax
import jax.numpy as jnp
from jax import lax
from jax.experimental import pallas as pl
from jax.experimental.pallas import tpu as pltpu

F32 = jnp.float32
BF16 = jnp.bfloat16

V7X_LANES = 128
MXU_COLS = 256
V7X_VMEM_BYTES = 64 * 1024 * 1024
VMEM_LIMIT_BYTES = V7X_VMEM_BYTES - 8 * 1024 * 1024

MLA_HEADS = 8
MLA_NOPE_DIM = 64
MLA_ROPE_DIM = 32
MLA_V_DIM = 64
MLA_QK_DIM = MLA_NOPE_DIM + MLA_ROPE_DIM
MLA_Q_RANK = 384
MLA_KV_RANK = 256
ROPE_THETA = 10000.0
MLSTM_HEADS = 4
MLSTM_HEAD_DIM = 128
MLSTM_WIDTH = MLSTM_HEADS * MLSTM_HEAD_DIM
CONV_WIDTH = 4
GATE_ROWS = 2 * MLSTM_HEADS
CONV_HALO = 8
N_BRANCHES = 2
N_MOD = 9
EPS = 1e-6
LOG2_E = 1.4426950408889634

HEAD_PAD = V7X_LANES
ROPE_LO = MLA_NOPE_DIM
ROPE_HALF = MLA_ROPE_DIM // 2

FFN_TM = 1024
FFN_TM_MIXER = 512
FFN_TF = 256
MIX_TM = 512
MIX_LEAD_CHUNKS = 3
ATTN_TQ = 256
ATTN_DEPTH = 2
MLSTM_CHUNK = 128


def _dot(a, b):
    return jnp.dot(a, b, preferred_element_type=F32)


def _rms(x, gain):
    return x * lax.rsqrt(jnp.mean(x * x, axis=-1, keepdims=True) + EPS) * gain


def _params(n_axes, flags=None):
    return pltpu.CompilerParams(
        dimension_semantics=("arbitrary",) * n_axes, vmem_limit_bytes=VMEM_LIMIT_BYTES, flags=flags)


def _resident(shape):
    zeros = (0,) * len(shape)
    return pl.BlockSpec(shape, lambda *_: zeros, pipeline_mode=pl.Buffered(1))


def _mod_spec(chunk, tm, seq, d):
    return pl.BlockSpec((None, 1, d), lambda i: ((i * tm) // seq, 0, chunk))


def _adaln_kernel(c_ref, w_ref, b_ref, o_ref):
    c = c_ref[...]
    sc = (c * jax.nn.sigmoid(c)).astype(BF16)
    o_ref[...] = _dot(sc, w_ref[...].astype(BF16)) + b_ref[...]


def _adaln(c, w_ada, b_ada):
    bsz, d = c.shape
    n = w_ada.shape[1]
    tn = d
    return pl.pallas_call(
        _adaln_kernel,
        grid=(n // tn,),
        in_specs=[pl.BlockSpec((bsz, d), lambda j: (0, 0)),
                  pl.BlockSpec((d, tn), lambda j: (0, j)),
                  pl.BlockSpec((1, tn), lambda j: (0, j))],
        out_specs=pl.BlockSpec((bsz, tn), lambda j: (0, j)),
        out_shape=jax.ShapeDtypeStruct((bsz, n), F32),
        compiler_params=_params(1),
        name="adaln",
    )(c, w_ada, b_ada.reshape(1, n))


def _modulated_norm(x, gain, shift, scale):
    return _rms(x, gain) * (1.0 + scale) + shift


def _ffn_kernel(*refs, d_ff, tf, mixer_prologue, epilogue):
    refs = list(refs)
    if mixer_prologue:
        h_ref, oa_ref, ob_ref, bg_ref, gtm_ref, wa_ref, wb_ref, wo_ref = refs[:8]
        refs = refs[8:]
    else:
        x_ref = refs.pop(0)
    g_ref, sh_ref, sc_ref, gt_ref, wg_ref, wu_ref, wd_ref = refs[:7]
    refs = refs[7:]
    if epilogue == "final_norm":
        gf_ref = refs.pop(0)
    elif epilogue == "next_mod":
        gn_ref, shn_ref, scn_ref = refs[:3]
        refs = refs[3:]
    o_ref = refs.pop(0)
    if epilogue == "next_mod":
        un_ref = refs.pop(0)
    u_scr, a_scr = refs[:2]

    if mixer_prologue:
        d = h_ref.shape[1]
        y_a = _dot(oa_ref[...], wa_ref[...])
        y_b = _dot(ob_ref[...], wb_ref[...])
        y = (jax.nn.sigmoid(bg_ref[:, 0:d].astype(F32)) * y_a
             + jax.nn.sigmoid(bg_ref[:, d:2 * d].astype(F32)) * y_b)
        x_ref = refs[2]
        x_ref[...] = h_ref[...] + gtm_ref[...] * _dot(y.astype(BF16), wo_ref[...])

    u_scr[...] = _modulated_norm(x_ref[...], g_ref[...], sh_ref[...], sc_ref[...]).astype(BF16)
    for j in range(d_ff // tf):
        cols = slice(j * tf, (j + 1) * tf)
        g = _dot(u_scr[...], wg_ref[:, cols])
        up = _dot(u_scr[...], wu_ref[:, cols])
        a_scr[:, cols] = (g * jax.nn.sigmoid(g) * up).astype(BF16)
    down = _dot(a_scr[...], wd_ref[...])
    out = x_ref[...] + (0.5 * gt_ref[...]) * down
    if epilogue == "final_norm":
        out = _rms(out, gf_ref[...])
    o_ref[...] = out
    if epilogue == "next_mod":
        un_ref[...] = _modulated_norm(out, gn_ref[...], shn_ref[...], scn_ref[...]).astype(BF16)


def _ffn(h, mod3, chunk0, norm_g, wg, wu, wd, seq, *, mixer=None, final_g=None, next_norm_g=None,
         next_chunk0=None):
    t, d = h.shape
    d_ff = wg.shape[1]
    tm = FFN_TM_MIXER if mixer is not None else FFN_TM
    row = lambda i: (i, 0)
    epilogue = "final_norm" if final_g is not None else "next_mod" if next_norm_g is not None else "plain"
    in_specs = [pl.BlockSpec((tm, d), row)]
    args = [h]
    scratch = [pltpu.VMEM((tm, d), BF16), pltpu.VMEM((tm, d_ff), BF16)]
    if mixer is not None:
        o_a, o_b, branch_gates, gate_chunk, w_a, w_b, w_o = mixer
        in_specs += [pl.BlockSpec((tm, o_a.shape[1]), row), pl.BlockSpec((tm, o_b.shape[1]), row),
                     pl.BlockSpec((tm, branch_gates.shape[1]), row),
                     _mod_spec(gate_chunk, tm, seq, d),
                     _resident(w_a.shape), _resident(w_b.shape), _resident(w_o.shape)]
        args += [o_a, o_b, branch_gates, mod3, w_a, w_b, w_o]
        scratch.append(pltpu.VMEM((tm, d), F32))
    in_specs += [_resident((1, d)), _mod_spec(chunk0, tm, seq, d), _mod_spec(chunk0 + 1, tm, seq, d),
                 _mod_spec(chunk0 + 2, tm, seq, d),
                 _resident((d, d_ff)), _resident((d, d_ff)), _resident((d_ff, d))]
    args += [norm_g.reshape(1, d), mod3, mod3, mod3, wg, wu, wd]
    out_specs = [pl.BlockSpec((tm, d), row)]
    out_shape = [jax.ShapeDtypeStruct((t, d), F32)]
    if epilogue == "final_norm":
        in_specs.append(_resident((1, d)))
        args.append(final_g.reshape(1, d))
    elif epilogue == "next_mod":
        in_specs += [_resident((1, d)), _mod_spec(next_chunk0, tm, seq, d),
                     _mod_spec(next_chunk0 + 1, tm, seq, d)]
        args += [next_norm_g.reshape(1, d), mod3, mod3]
        out_specs.append(pl.BlockSpec((tm, d), row))
        out_shape.append(jax.ShapeDtypeStruct((t, d), BF16))
    return pl.pallas_call(
        functools.partial(_ffn_kernel, d_ff=d_ff, tf=FFN_TF, mixer_prologue=mixer is not None,
                          epilogue=epilogue),
        grid=(t // tm,),
        in_specs=in_specs,
        out_specs=out_specs,
        out_shape=out_shape,
        scratch_shapes=scratch,
        compiler_params=_params(1),
        name=("mixout_ffn_" if mixer is not None else "ffn_") + epilogue,
    )(*args)


_C_QLAT = 0
_C_CKV = _C_QLAT + MLA_Q_RANK
_C_KPE = _C_CKV + MLA_KV_RANK
_C_XM = _C_KPE + HEAD_PAD
_C_VM = _C_XM + MLSTM_WIDTH
_C_OM = _C_VM + MLSTM_WIDTH
_C_IF = _C_OM + MLSTM_WIDTH
_C_GATE = _C_IF + V7X_LANES


def _rope_tables(pos_ref, tm):
    n_blk = HEAD_PAD // MLA_ROPE_DIM
    rows = tm // n_blk
    lane = lax.broadcasted_iota(jnp.int32, (1, HEAD_PAD), 1)
    blk = lane // MLA_ROPE_DIM
    idx = (lane % ROPE_HALF).astype(F32)
    theta = jnp.full((1, HEAD_PAD), ROPE_THETA, F32)
    inv_freq = jnp.exp(-(idx / ROPE_HALF) * jnp.log(theta))
    pos = pos_ref[0:rows, :].astype(F32)
    for b in range(1, n_blk):
        pos = jnp.where(blk == b, pos_ref[b * rows:(b + 1) * rows, :].astype(F32), pos)
    ang = pos * inv_freq
    cos_p = jnp.cos(ang)
    sin_p = jnp.sin(ang)

    in_lo = (lane >= ROPE_LO) & (lane < ROPE_LO + ROPE_HALF)
    in_hi = (lane >= ROPE_LO + ROPE_HALF) & (lane < ROPE_LO + MLA_ROPE_DIM)
    cos_m, sin_up, sin_dn = [], [], []
    for b in range(n_blk):
        shift = (ROPE_LO - b * MLA_ROPE_DIM) % HEAD_PAD
        cos = cos_p if shift == 0 else pltpu.roll(cos_p, shift, axis=1)
        sin = sin_p if shift == 0 else pltpu.roll(sin_p, shift, axis=1)
        cos_m.append(jnp.where(lane < ROPE_LO, 1.0, jnp.where(in_lo | in_hi, cos, 0.0)))
        sin_up.append(jnp.where(in_lo, -sin, 0.0))
        sin_dn.append(jnp.where(in_hi, sin, 0.0))
    return tuple(jnp.concatenate(t, axis=0) for t in (cos_m, sin_up, sin_dn))


def _rope_group(x, tables):
    cos_m, sin_up, sin_dn = tables
    up = pltpu.roll(x, HEAD_PAD - ROPE_HALF, axis=1)
    dn = pltpu.roll(x, ROPE_HALF, axis=1)
    return x * cos_m + up * sin_up + dn * sin_dn


def _mixin_kernel(u_ref, pos_ref, win_ref, qan_ref, wq_ref, kvn_ref,
                  wk_ref, wv_ref, bif_ref, cw_ref, cb_ref,
                  q_out, k_out, v_out, xc_out, vm_out, og_out, if_out, gate_out, u_scr, xext_scr,
                  *, tm, seq):
    def col_chunks(width):
        return [slice(lo, min(lo + MXU_COLS, width)) for lo in range(0, width, MXU_COLS)]

    u_scr[...] = u_ref[...]

    def proj(lo, cols):
        return _dot(u_scr[...], win_ref[:, lo + cols.start:lo + cols.stop])

    def gate_chunk(cols):
        gate_out[:, cols] = proj(_C_GATE, cols).astype(BF16)

    gate_cols = col_chunks(gate_out.shape[1])
    q_lat = jnp.concatenate([proj(_C_QLAT, cols) for cols in col_chunks(MLA_Q_RANK)], axis=1)
    c_kv = proj(_C_CKV, slice(0, MLA_KV_RANK))
    k_pe_raw = proj(_C_KPE, slice(0, HEAD_PAD))
    for cols in gate_cols[:MIX_LEAD_CHUNKS]:
        gate_chunk(cols)
    tables = _rope_tables(pos_ref, tm)
    q_n = _rms(q_lat, qan_ref[...]).astype(BF16)
    c_n = _rms(c_kv, kvn_ref[...]).astype(BF16)
    k_pe = _rope_group(k_pe_raw, tables)

    q_scale = MLA_QK_DIM ** -0.5 * LOG2_E

    def q_chunk(cols):
        q = _dot(q_n, wq_ref[:, cols])
        for g in range(MXU_COLS // HEAD_PAD):
            grp = slice(g * HEAD_PAD, (g + 1) * HEAD_PAD)
            out = slice(cols.start + grp.start, cols.start + grp.stop)
            q_out[:, out] = (_rope_group(q[:, grp], tables) * q_scale).astype(BF16)

    def k_chunk(cols):
        k_nope = _dot(c_n, wk_ref[:, cols])
        for g in range(MXU_COLS // HEAD_PAD):
            grp = slice(g * HEAD_PAD, (g + 1) * HEAD_PAD)
            out = slice(cols.start + grp.start, cols.start + grp.stop)
            k_out[:, out] = (k_nope[:, grp] + k_pe).astype(BF16)

    def v_chunk(cols):
        v_out[:, cols] = _dot(c_n, wv_ref[:, cols]).astype(BF16)

    @pl.when((pl.program_id(0) * tm) % seq == 0)
    def _():
        xext_scr[0:CONV_HALO, :] = jnp.zeros((CONV_HALO, MLSTM_WIDTH), F32)

    def conv_chunk(cols):
        xext_scr[CONV_HALO:CONV_HALO + tm, cols] = proj(_C_XM, cols)
        conv = cb_ref[:, cols] + (xext_scr[CONV_HALO:CONV_HALO + tm, cols]
                                  * cw_ref[CONV_WIDTH - 1:CONV_WIDTH, cols])
        for j in range(1, CONV_WIDTH):
            conv = conv + (xext_scr[CONV_HALO - j:CONV_HALO - j + tm, cols]
                           * cw_ref[CONV_WIDTH - 1 - j:CONV_WIDTH - j, cols])
        xext_scr[0:CONV_HALO, cols] = xext_scr[tm:tm + CONV_HALO, cols]
        xc_out[:, cols] = (conv * jax.nn.sigmoid(conv)).astype(BF16)

    def vm_chunk(cols):
        vm_out[:, cols] = proj(_C_VM, cols).astype(BF16)

    def og_chunk(cols):
        og_out[:, cols] = jax.nn.sigmoid(proj(_C_OM, cols)).astype(BF16)

    light = ([(gate_chunk, c) for c in gate_cols[MIX_LEAD_CHUNKS:]]
             + [(vm_chunk, c) for c in col_chunks(MLSTM_WIDTH)]
             + [(v_chunk, c) for c in col_chunks(MLA_HEADS * MLA_V_DIM)])
    heavy = ([(q_chunk, c) for c in col_chunks(MLA_HEADS * HEAD_PAD)]
             + [(conv_chunk, c) for c in col_chunks(MLSTM_WIDTH)]
             + [(og_chunk, c) for c in col_chunks(MLSTM_WIDTH)]
             + [(k_chunk, c) for c in col_chunks(MLA_HEADS * HEAD_PAD)])
    for n in range(max(len(light), len(heavy))):
        for items in (light, heavy):
            if n < len(items):
                fn, cols = items[n]
                fn(cols)
    if_out[...] = (proj(_C_IF, slice(0, V7X_LANES)) + bif_ref[...]).T[0:GATE_ROWS, :]


def _mix_in(u, pos, win_r, qan, wq_r, kvn, wk_r, wv_r, bif, conv_w, conv_b, seq):
    t, d = u.shape
    tm = MIX_TM
    row = lambda i: (i, 0)
    n_in = win_r.shape[1]
    qk_w = MLA_HEADS * HEAD_PAD
    v_w = MLA_HEADS * MLA_V_DIM
    outs = [(qk_w, BF16), (qk_w, BF16), (v_w, BF16), (MLSTM_WIDTH, BF16), (MLSTM_WIDTH, BF16),
            (MLSTM_WIDTH, BF16), None, (N_BRANCHES * d, BF16)]
    out_specs = [pl.BlockSpec((tm, o[0]), row) if o else pl.BlockSpec((GATE_ROWS, tm), lambda i: (0, i))
                 for o in outs]
    out_shape = [jax.ShapeDtypeStruct((t, o[0]), o[1]) if o else jax.ShapeDtypeStruct((GATE_ROWS, t), F32)
                 for o in outs]
    return pl.pallas_call(
        functools.partial(_mixin_kernel, tm=tm, seq=seq),
        grid=(t // tm,),
        in_specs=[pl.BlockSpec((tm, d), row), pl.BlockSpec((tm, 1), row),
                  _resident((d, n_in)), _resident((1, MLA_Q_RANK)),
                  _resident((MLA_Q_RANK, qk_w)), _resident((1, MLA_KV_RANK)),
                  _resident((MLA_KV_RANK, qk_w)), _resident((MLA_KV_RANK, v_w)),
                  _resident((1, V7X_LANES)), _resident((CONV_WIDTH, MLSTM_WIDTH)),
                  _resident((1, MLSTM_WIDTH))],
        out_specs=out_specs,
        out_shape=out_shape,
        scratch_shapes=[pltpu.VMEM((tm, d), BF16), pltpu.VMEM((tm + CONV_HALO, MLSTM_WIDTH), F32)],
        compiler_params=_params(1),
        name="mix_in",
    )(u, pos, win_r, qan.reshape(1, -1), wq_r, kvn.reshape(1, -1), wk_r, wv_r, bif, conv_w,
      conv_b.reshape(1, -1))


def _log_sigmoid(x):
    return jnp.minimum(x, 0.0) - jnp.log1p(jnp.exp(-jnp.abs(x)))


def _chunk_scan(x, lane_in_chunk, chunk, combine, fill):
    step = 1
    while step < chunk:
        x = combine(x, jnp.where(lane_in_chunk >= step, pltpu.roll(x, step, axis=1), fill))
        step *= 2
    return x


def _attention_stages(q_ref, k_ref, v_ref, o_ref, vaug_scr, s_scr, seq, tq):
    pair_v = 2 * MLA_V_DIM
    lane = lax.broadcasted_iota(jnp.int32, (1, pair_v), 1)
    vaug_scr[:, 0:pair_v] = v_ref[...]
    vaug_scr[:, pair_v:2 * pair_v] = jnp.where(
        lax.broadcasted_iota(jnp.int32, (seq, pair_v), 1) == 0, 1.0, 0.0).astype(BF16)
    r_i = lax.broadcasted_iota(jnp.int32, (tq, tq), 0)
    c_i = lax.broadcasted_iota(jnp.int32, (tq, tq), 1)
    causal = c_i <= r_i
    neg = jnp.finfo(F32).min
    def buffer(qi, j):
        return (qi % ATTN_DEPTH) * 2 + j

    def scores(qi):
        rows = slice(qi * tq, (qi + 1) * tq)
        maxima = []
        for j in range(2):
            grp = slice(j * HEAD_PAD, (j + 1) * HEAD_PAD)
            qh = q_ref[rows, grp]
            m_vec = None
            for cj in range(qi + 1):
                cols = slice(cj * tq, (cj + 1) * tq)
                s = lax.dot_general(qh, k_ref[cols, grp], (((1,), (1,)), ((), ())),
                                    preferred_element_type=F32)
                if cj == qi:
                    s = jnp.where(causal, s, neg)
                s_scr[buffer(qi, j), :, cols] = s
                for g in range(tq // V7X_LANES):
                    part = s[:, g * V7X_LANES:(g + 1) * V7X_LANES]
                    m_vec = part if m_vec is None else jnp.maximum(m_vec, part)
            maxima.append(jnp.max(m_vec, axis=-1, keepdims=True))
        return maxima

    def outputs(qi, maxima):
        outs = []
        for j in range(2):
            o_aug = None
            for cj in range(qi + 1):
                cols = slice(cj * tq, (cj + 1) * tq)
                p = jnp.exp2(s_scr[buffer(qi, j), :, cols] - maxima[j]).astype(BF16)
                part = _dot(p, vaug_scr[cols, :])
                o_aug = part if o_aug is None else o_aug + part
            outs.append(o_aug[:, 0:pair_v] * (1.0 / o_aug[:, pair_v:pair_v + 1]))
        o_ref[qi * tq:(qi + 1) * tq, :] = jnp.where(lane < MLA_V_DIM, outs[0], outs[1]).astype(BF16)

    return scores, outputs


def _mlstm_chunk_stage(head, xc_ref, vm_ref, og_ref, if_ref, wq_ref, wk_ref, wkt_ref, hn_ref, o_ref,
                       q_scr, k_scr, kt_scr, vaug_scr, ct_scr, seq, chunk):
    dh = MLSTM_HEAD_DIM
    xc = xc_ref[...]
    q_scr[...] = _dot(xc, wq_ref[...]).astype(BF16)
    k_scr[...] = (_dot(xc, wk_ref[...]) * (dh ** -0.5)).astype(BF16)
    kt_scr[...] = lax.dot_general(wkt_ref[...], xc, (((1,), (1,)), ((), ())),
                                  preferred_element_type=F32) * (dh ** -0.5)
    vaug_scr[:, 0:dh] = vm_ref[...]
    vaug_scr[:, dh:2 * dh] = jnp.where(
        lax.broadcasted_iota(jnp.int32, (seq, dh), 1) == 0, 1.0, 0.0).astype(BF16)
    ct_scr[...] = jnp.zeros((dh, 2 * dh), F32)

    gates_t = if_ref[...]
    lane_in_chunk = lax.broadcasted_iota(jnp.int32, (GATE_ROWS, seq), 1) % chunk
    sub = lax.broadcasted_iota(jnp.int32, (GATE_ROWS, seq), 0)

    def head_row(rows8, r):
        return jnp.sum(jnp.where(sub == r, rows8, 0.0), axis=0, keepdims=True)

    b_row = head_row(_chunk_scan(_log_sigmoid(gates_t) * LOG2_E, lane_in_chunk, chunk, jnp.add, 0.0),
                     head + MLSTM_HEADS)
    src_row = head_row(gates_t, head) * LOG2_E - b_row
    lane_1 = lax.broadcasted_iota(jnp.int32, (1, seq), 1) % chunk
    mi_row = b_row + _chunk_scan(src_row, lane_1, chunk, jnp.maximum, -jnp.inf)

    r_i = lax.broadcasted_iota(jnp.int32, (chunk, chunk), 0)
    c_i = lax.broadcasted_iota(jnp.int32, (chunk, chunk), 1)
    eye = r_i == c_i
    tril = c_i <= r_i
    last_lane = lax.broadcasted_iota(jnp.int32, (1, chunk), 1) == chunk - 1

    def to_col(row):
        return jnp.sum(jnp.where(eye, row, 0.0), axis=1, keepdims=True)

    def chunk_step(c, m_prev):
        cs = slice(c * chunk, (c + 1) * chunk)
        b_c = b_row[:, cs]
        src_c = src_row[:, cs]
        g = jnp.sum(jnp.where(last_lane, b_c, 0.0), axis=1, keepdims=True)
        m_new = jnp.maximum(g + m_prev, jnp.max(g + src_c, axis=1, keepdims=True))
        m_t = jnp.maximum(b_c + m_prev, mi_row[:, cs])
        u_col = to_col(b_c - m_t)
        q_c = q_scr[cs, :]
        v_c = vaug_scr[cs, :]
        s_qk = lax.dot_general(q_c, k_scr[cs, :], (((1,), (1,)), ((), ())),
                               preferred_element_type=F32)
        d_mat = jnp.exp2(jnp.where(tril, u_col + src_c, -jnp.inf))
        s_mat = (d_mat * s_qk).astype(BF16)
        q_in = (q_c.astype(F32) * jnp.exp2(u_col + m_prev)).astype(BF16)
        ct = ct_scr[...]
        tot = _dot(q_in, ct.astype(BF16)) + _dot(s_mat, v_c)
        den = jnp.maximum(jnp.abs(tot[:, dh:dh + 1]), to_col(jnp.exp2(-m_t)))
        hh = tot[:, 0:dh] * (1.0 / den) * og_ref[cs, :].astype(F32)
        o_ref[cs, :] = _rms(hh, hn_ref[...]).astype(BF16)
        c_loc = _dot((kt_scr[:, cs] * jnp.exp2(g + src_c - m_new)).astype(BF16), v_c)
        ct_scr[...] = jnp.exp2(g + m_prev - m_new) * ct + c_loc
        return m_new

    return chunk_step


def _mixers_kernel(q_ref, k_ref, v_ref, xc_ref, vm_ref, og_ref, if_ref, wq_ref, wk_ref, wkt_ref, hn_ref,
                   oa_ref, ob_ref, vaug_scr, s_scr, mq_scr, mk_scr, mkt_scr, mv_scr, ct_scr,
                   *, seq, tq, chunk):
    chunk_step = _mlstm_chunk_stage(pl.program_id(1), xc_ref, vm_ref, og_ref, if_ref, wq_ref, wk_ref,
                                    wkt_ref, hn_ref, ob_ref, mq_scr, mk_scr, mkt_scr, mv_scr, ct_scr,
                                    seq, chunk)
    scores, outputs = _attention_stages(q_ref, k_ref, v_ref, oa_ref, vaug_scr, s_scr, seq, tq)

    n_tiles = seq // tq
    n_chunks = seq // chunk
    order = list(reversed(range(n_tiles)))
    work = [qi + 1 for qi in order]
    quota, done = [], 0
    for n in range(n_tiles):
        target = round(n_chunks * sum(work[:n + 1]) / sum(work))
        quota.append(target - done)
        done = target

    m_prev = jnp.zeros((1, 1), F32)
    c = 0
    maxima = scores(order[0])
    for n, qi in enumerate(order):
        next_maxima = scores(order[n + 1]) if n + 1 < n_tiles else None
        for _ in range(quota[n]):
            m_prev = chunk_step(c, m_prev)
            c += 1
        outputs(qi, maxima)
        maxima = next_maxima


def _token_mixers(q, k, v, xc, vm, og, ifg, wq, wk, head_norm, bsz, seq):
    t = q.shape[0]
    dh = MLSTM_HEAD_DIM
    pair_qk = 2 * HEAD_PAD
    pair_v = 2 * MLA_V_DIM
    n_groups = MLA_HEADS // 2
    assert n_groups == MLSTM_HEADS and pair_v == dh
    blk = lambda b, g: (b, g)
    per_head = lambda b, g: (g, 0, 0)
    return pl.pallas_call(
        functools.partial(_mixers_kernel, seq=seq, tq=ATTN_TQ, chunk=MLSTM_CHUNK),
        grid=(bsz, n_groups),
        in_specs=[pl.BlockSpec((seq, pair_qk), blk), pl.BlockSpec((seq, pair_qk), blk),
                  pl.BlockSpec((seq, pair_v), blk),
                  pl.BlockSpec((seq, dh), blk), pl.BlockSpec((seq, dh), blk), pl.BlockSpec((seq, dh), blk),
                  pl.BlockSpec((GATE_ROWS, seq), lambda b, g: (0, b)),
                  pl.BlockSpec((None, dh, dh), per_head), pl.BlockSpec((None, dh, dh), per_head),
                  pl.BlockSpec((None, dh, dh), per_head), pl.BlockSpec((1, dh), lambda b, g: (0, g))],
        out_specs=[pl.BlockSpec((seq, pair_v), blk), pl.BlockSpec((seq, dh), blk)],
        out_shape=[jax.ShapeDtypeStruct((t, MLA_HEADS * MLA_V_DIM), BF16),
                   jax.ShapeDtypeStruct((t, MLSTM_WIDTH), BF16)],
        scratch_shapes=[pltpu.VMEM((seq, 2 * pair_v), BF16),
                        pltpu.VMEM((2 * ATTN_DEPTH, ATTN_TQ, seq), F32),
                        pltpu.VMEM((seq, dh), BF16), pltpu.VMEM((seq, dh), BF16),
                        pltpu.VMEM((dh, seq), F32),
                        pltpu.VMEM((seq, 2 * dh), BF16),
                        pltpu.VMEM((dh, 2 * dh), F32)],
        compiler_params=_params(2),
        name="token_mixers",
    )(q, k, v, xc, vm, og, ifg, wq, wk, jnp.swapaxes(wk, 1, 2), head_norm.reshape(1, -1))


def _pad_cols(w, lo, width):
    return jnp.pad(w, ((0, 0), (lo, width - lo - w.shape[1])))


def _layout_w_in(w_in):
    o = 0
    parts = {}
    for name, n in (("q", MLA_Q_RANK), ("ckv", MLA_KV_RANK), ("kpe", MLA_ROPE_DIM),
                    ("xm", MLSTM_WIDTH), ("vm", MLSTM_WIDTH), ("om", MLSTM_WIDTH),
                    ("i", MLSTM_HEADS), ("f", MLSTM_HEADS), ("gate", None)):
        n = w_in.shape[1] - o if n is None else n
        parts[name] = w_in[:, o:o + n]
        o += n
    gates_if = _pad_cols(jnp.concatenate([parts["i"], parts["f"]], axis=1), 0, V7X_LANES)
    return jnp.concatenate(
        [parts["q"], parts["ckv"], _pad_cols(parts["kpe"], ROPE_LO, HEAD_PAD), parts["xm"],
         parts["vm"], parts["om"], gates_if, parts["gate"]], axis=1).astype(BF16)


def _layout_w_q_b(w_q_b):
    w = w_q_b.reshape(MLA_Q_RANK, MLA_HEADS, MLA_QK_DIM)
    w = jnp.pad(w, ((0, 0), (0, 0), (0, HEAD_PAD - MLA_QK_DIM)))
    return w.reshape(MLA_Q_RANK, MLA_HEADS * HEAD_PAD).astype(BF16)


def _layout_w_kv_b(w_kv_b):
    w = w_kv_b.reshape(MLA_KV_RANK, MLA_HEADS, MLA_NOPE_DIM + MLA_V_DIM)
    wk = jnp.pad(w[:, :, :MLA_NOPE_DIM], ((0, 0), (0, 0), (0, HEAD_PAD - MLA_NOPE_DIM)))
    wv = w[:, :, MLA_NOPE_DIM:]
    return (wk.reshape(MLA_KV_RANK, MLA_HEADS * HEAD_PAD).astype(BF16),
            wv.reshape(MLA_KV_RANK, MLA_HEADS * MLA_V_DIM).astype(BF16))


def kernel(x, c, positions, w_ada, b_ada, norm_ff1, ff1_w_gate, ff1_w_up, ff1_w_down, norm_mix, w_in,
           q_a_norm, w_q_b, kv_a_norm, w_kv_b, conv_w, conv_b, w_q_m, w_k_m, b_i, b_f, mlstm_norm,
           w_mla_out, w_mlstm_out, w_o, norm_ff2, ff2_w_gate, ff2_w_up, ff2_w_down, norm_final):
    bsz, seq, d = x.shape
    t = bsz * seq
    depth = w_ada.shape[0]
    h = x.reshape(t, d)
    pos = positions.reshape(t, 1)
    for l in range(depth):
        mod3 = _adaln(c, w_ada[l], b_ada[l]).reshape(bsz, 1, N_MOD * d)
        h, u_mix = _ffn(h, mod3, 0, norm_ff1[l], ff1_w_gate[l].astype(BF16), ff1_w_up[l].astype(BF16),
                        ff1_w_down[l].astype(BF16), seq, next_norm_g=norm_mix[l], next_chunk0=3)
        wk_r, wv_r = _layout_w_kv_b(w_kv_b[l])
        bif = _pad_cols(jnp.concatenate([b_i[l], b_f[l]]).reshape(1, -1), 0, V7X_LANES)
        q, k, v, xc, vm, og, ifg, gates = _mix_in(
            u_mix, pos, _layout_w_in(w_in[l]), q_a_norm[l], _layout_w_q_b(w_q_b[l]),
            kv_a_norm[l], wk_r, wv_r, bif, conv_w[l], conv_b[l], seq)
        o_a, o_b = _token_mixers(q, k, v, xc, vm, og, ifg, w_q_m[l].astype(BF16),
                                 w_k_m[l].astype(BF16), mlstm_norm[l], bsz, seq)
        mixer = (o_a, o_b, gates, 5, w_mla_out[l].astype(BF16), w_mlstm_out[l].astype(BF16),
                 w_o[l].astype(BF16))
        final_g = norm_final if l == depth - 1 else None
        h = _ffn(h, mod3, 6, norm_ff2[l], ff2_w_gate[l].astype(BF16), ff2_w_up[l].astype(BF16),
                 ff2_w_down[l].astype(BF16), seq, mixer=mixer, final_g=final_g)[0]
    return h.reshape(bsz, seq, d)
```

```python
import functools

import jax
import jax.numpy as jnp
from jax import lax
from jax.experimental import pallas as pl
from jax.experimental.pallas import tpu as pltpu

F32 = jnp.float32
BF16 = jnp.bfloat16

V7X_LANES = 128
MXU_COLS = 256
V7X_VMEM_BYTES = 64 * 1024 * 1024
VMEM_LIMIT_BYTES = V7X_VMEM_BYTES - 8 * 1024 * 1024

MLA_HEADS = 8
MLA_NOPE_DIM = 64
MLA_ROPE_DIM = 32
MLA_V_DIM = 64
MLA_QK_DIM = MLA_NOPE_DIM + MLA_ROPE_DIM
MLA_Q_RANK = 384
MLA_KV_RANK = 256
ROPE_THETA = 10000.0
MLSTM_HEADS = 4
MLSTM_HEAD_DIM = 128
MLSTM_WIDTH = MLSTM_HEADS * MLSTM_HEAD_DIM
CONV_WIDTH = 4
GATE_ROWS = 2 * MLSTM_HEADS
CONV_HALO = 8
N_BRANCHES = 2
N_MOD = 9
EPS = 1e-6
LOG2_E = 1.4426950408889634

HEAD_PAD = V7X_LANES
ROPE_LO = MLA_NOPE_DIM
ROPE_HALF = MLA_ROPE_DIM // 2

FFN_TM = 1024
FFN_TM_MIXER = 512
FFN_TF = 256
MIX_TM = 1024
MIX_LEAD_CHUNKS = 3
ATTN_TQ = 256
ATTN_DEPTH = 2
MLSTM_CHUNK = 128


def _dot(a, b):
    return jnp.dot(a, b, preferred_element_type=F32)


def _rms(x, gain):
    return x * lax.rsqrt(jnp.mean(x * x, axis=-1, keepdims=True) + EPS) * gain


def _params(n_axes, flags=None):
    return pltpu.CompilerParams(
        dimension_semantics=("arbitrary",) * n_axes, vmem_limit_bytes=VMEM_LIMIT_BYTES, flags=flags)


def _resident(shape):
    zeros = (0,) * len(shape)
    return pl.BlockSpec(shape, lambda *_: zeros, pipeline_mode=pl.Buffered(1))


def _mod_spec(chunk, tm, seq, d):
    return pl.BlockSpec((None, 1, d), lambda i: ((i * tm) // seq, 0, chunk))


def _adaln_kernel(c_ref, w_ref, b_ref, o_ref):
    c = c_ref[...]
    sc = (c * jax.nn.sigmoid(c)).astype(BF16)
    o_ref[...] = _dot(sc, w_ref[...].astype(BF16)) + b_ref[...]


def _adaln(c, w_ada, b_ada):
    bsz, d = c.shape
    n = w_ada.shape[1]
    tn = d
    return pl.pallas_call(
        _adaln_kernel,
        grid=(n // tn,),
        in_specs=[pl.BlockSpec((bsz, d), lambda j: (0, 0)),
                  pl.BlockSpec((d, tn), lambda j: (0, j)),
                  pl.BlockSpec((1, tn), lambda j: (0, j))],
        out_specs=pl.BlockSpec((bsz, tn), lambda j: (0, j)),
        out_shape=jax.ShapeDtypeStruct((bsz, n), F32),
        compiler_params=_params(1),
        name="adaln",
    )(c, w_ada, b_ada.reshape(1, n))


def _modulated_norm(x, gain, shift, scale):
    return _rms(x, gain) * (1.0 + scale) + shift


def _ffn_kernel(*refs, d_ff, tf, mixer_prologue, epilogue):
    refs = list(refs)
    if mixer_prologue:
        h_ref, oa_ref, ob_ref, bg_ref, gtm_ref, wa_ref, wb_ref, wo_ref = refs[:8]
        refs = refs[8:]
    else:
        x_ref = refs.pop(0)
    g_ref, sh_ref, sc_ref, gt_ref, wg_ref, wu_ref, wd_ref = refs[:7]
    refs = refs[7:]
    if epilogue == "final_norm":
        gf_ref = refs.pop(0)
    elif epilogue == "next_mod":
        gn_ref, shn_ref, scn_ref = refs[:3]
        refs = refs[3:]
    o_ref = refs.pop(0)
    if epilogue == "next_mod":
        un_ref = refs.pop(0)
    u_scr, a_scr = refs[:2]

    if mixer_prologue:
        d = h_ref.shape[1]
        y_a = _dot(oa_ref[...], wa_ref[...])
        y_b = _dot(ob_ref[...], wb_ref[...])
        y = (jax.nn.sigmoid(bg_ref[:, 0:d].astype(F32)) * y_a
             + jax.nn.sigmoid(bg_ref[:, d:2 * d].astype(F32)) * y_b)
        x_ref = refs[2]
        x_ref[...] = h_ref[...] + gtm_ref[...] * _dot(y.astype(BF16), wo_ref[...])

    u_scr[...] = _modulated_norm(x_ref[...], g_ref[...], sh_ref[...], sc_ref[...]).astype(BF16)
    for j in range(d_ff // tf):
        cols = slice(j * tf, (j + 1) * tf)
        g = _dot(u_scr[...], wg_ref[:, cols])
        up = _dot(u_scr[...], wu_ref[:, cols])
        a_scr[:, cols] = (g * jax.nn.sigmoid(g) * up).astype(BF16)
    down = _dot(a_scr[...], wd_ref[...])
    out = x_ref[...] + (0.5 * gt_ref[...]) * down
    if epilogue == "final_norm":
        out = _rms(out, gf_ref[...])
    o_ref[...] = out
    if epilogue == "next_mod":
        un_ref[...] = _modulated_norm(out, gn_ref[...], shn_ref[...], scn_ref[...]).astype(BF16)


def _ffn(h, mod3, chunk0, norm_g, wg, wu, wd, seq, *, mixer=None, final_g=None, next_norm_g=None,
         next_chunk0=None):
    t, d = h.shape
    d_ff = wg.shape[1]
    tm = FFN_TM_MIXER if mixer is not None else FFN_TM
    row = lambda i: (i, 0)
    epilogue = "final_norm" if final_g is not None else "next_mod" if next_norm_g is not None else "plain"
    in_specs = [pl.BlockSpec((tm, d), row)]
    args = [h]
    scratch = [pltpu.VMEM((tm, d), BF16), pltpu.VMEM((tm, d_ff), BF16)]
    if mixer is not None:
        o_a, o_b, branch_gates, gate_chunk, w_a, w_b, w_o = mixer
        in_specs += [pl.BlockSpec((tm, o_a.shape[1]), row), pl.BlockSpec((tm, o_b.shape[1]), row),
                     pl.BlockSpec((tm, branch_gates.shape[1]), row),
                     _mod_spec(gate_chunk, tm, seq, d),
                     _resident(w_a.shape), _resident(w_b.shape), _resident(w_o.shape)]
        args += [o_a, o_b, branch_gates, mod3, w_a, w_b, w_o]
        scratch.append(pltpu.VMEM((tm, d), F32))
    in_specs += [_resident((1, d)), _mod_spec(chunk0, tm, seq, d), _mod_spec(chunk0 + 1, tm, seq, d),
                 _mod_spec(chunk0 + 2, tm, seq, d),
                 _resident((d, d_ff)), _resident((d, d_ff)), _resident((d_ff, d))]
    args += [norm_g.reshape(1, d), mod3, mod3, mod3, wg, wu, wd]
    out_specs = [pl.BlockSpec((tm, d), row)]
    out_shape = [jax.ShapeDtypeStruct((t, d), F32)]
    if epilogue == "final_norm":
        in_specs.append(_resident((1, d)))
        args.append(final_g.reshape(1, d))
    elif epilogue == "next_mod":
        in_specs += [_resident((1, d)), _mod_spec(next_chunk0, tm, seq, d),
                     _mod_spec(next_chunk0 + 1, tm, seq, d)]
        args += [next_norm_g.reshape(1, d), mod3, mod3]
        out_specs.append(pl.BlockSpec((tm, d), row))
        out_shape.append(jax.ShapeDtypeStruct((t, d), BF16))
    return pl.pallas_call(
        functools.partial(_ffn_kernel, d_ff=d_ff, tf=FFN_TF, mixer_prologue=mixer is not None,
                          epilogue=epilogue),
        grid=(t // tm,),
        in_specs=in_specs,
        out_specs=out_specs,
        out_shape=out_shape,
        scratch_shapes=scratch,
        compiler_params=_params(1),
        name=("mixout_ffn_" if mixer is not None else "ffn_") + epilogue,
    )(*args)


_C_QLAT = 0
_C_CKV = _C_QLAT + MLA_Q_RANK
_C_KPE = _C_CKV + MLA_KV_RANK
_C_XM = _C_KPE + HEAD_PAD
_C_VM = _C_XM + MLSTM_WIDTH
_C_OM = _C_VM + MLSTM_WIDTH
_C_IF = _C_OM + MLSTM_WIDTH
_C_GATE = _C_IF + V7X_LANES


def _rope_tables(pos_ref, tm):
    n_blk = HEAD_PAD // MLA_ROPE_DIM
    rows = tm // n_blk
    lane = lax.broadcasted_iota(jnp.int32, (1, HEAD_PAD), 1)
    blk = lane // MLA_ROPE_DIM
    idx = (lane % ROPE_HALF).astype(F32)
    theta = jnp.full((1, HEAD_PAD), ROPE_THETA, F32)
    inv_freq = jnp.exp(-(idx / ROPE_HALF) * jnp.log(theta))
    pos = pos_ref[0:rows, :].astype(F32)
    for b in range(1, n_blk):
        pos = jnp.where(blk == b, pos_ref[b * rows:(b + 1) * rows, :].astype(F32), pos)
    ang = pos * inv_freq
    cos_p = jnp.cos(ang)
    sin_p = jnp.sin(ang)

    in_lo = (lane >= ROPE_LO) & (lane < ROPE_LO + ROPE_HALF)
    in_hi = (lane >= ROPE_LO + ROPE_HALF) & (lane < ROPE_LO + MLA_ROPE_DIM)
    cos_m, sin_up, sin_dn = [], [], []
    for b in range(n_blk):
        shift = (ROPE_LO - b * MLA_ROPE_DIM) % HEAD_PAD
        cos = cos_p if shift == 0 else pltpu.roll(cos_p, shift, axis=1)
        sin = sin_p if shift == 0 else pltpu.roll(sin_p, shift, axis=1)
        cos_m.append(jnp.where(lane < ROPE_LO, 1.0, jnp.where(in_lo | in_hi, cos, 0.0)))
        sin_up.append(jnp.where(in_lo, -sin, 0.0))
        sin_dn.append(jnp.where(in_hi, sin, 0.0))
    return tuple(jnp.concatenate(t, axis=0) for t in (cos_m, sin_up, sin_dn))


def _rope_group(x, tables):
    cos_m, sin_up, sin_dn = tables
    up = pltpu.roll(x, HEAD_PAD - ROPE_HALF, axis=1)
    dn = pltpu.roll(x, ROPE_HALF, axis=1)
    return x * cos_m + up * sin_up + dn * sin_dn


def _mixin_kernel(u_ref, pos_ref, win_ref, qan_ref, wq_ref, kvn_ref,
                  wk_ref, wv_ref, bif_ref, cw_ref, cb_ref,
                  q_out, k_out, v_out, xc_out, vm_out, og_out, if_out, gate_out, u_scr, xext_scr,
                  *, tm, seq):
    def col_chunks(width):
        return [slice(lo, min(lo + MXU_COLS, width)) for lo in range(0, width, MXU_COLS)]

    u_scr[...] = u_ref[...]

    def proj(lo, cols):
        return _dot(u_scr[...], win_ref[:, lo + cols.start:lo + cols.stop])

    def gate_chunk(cols):
        gate_out[:, cols] = proj(_C_GATE, cols).astype(BF16)

    gate_cols = col_chunks(gate_out.shape[1])
    q_lat = jnp.concatenate([proj(_C_QLAT, cols) for cols in col_chunks(MLA_Q_RANK)], axis=1)
    c_kv = proj(_C_CKV, slice(0, MLA_KV_RANK))
    k_pe_raw = proj(_C_KPE, slice(0, HEAD_PAD))
    for cols in gate_cols[:MIX_LEAD_CHUNKS]:
        gate_chunk(cols)
    tables = _rope_tables(pos_ref, tm)
    q_n = _rms(q_lat, qan_ref[...]).astype(BF16)
    c_n = _rms(c_kv, kvn_ref[...]).astype(BF16)
    k_pe = _rope_group(k_pe_raw, tables)

    q_scale = MLA_QK_DIM ** -0.5 * LOG2_E

    def q_chunk(cols):
        q = _dot(q_n, wq_ref[:, cols])
        for g in range(MXU_COLS // HEAD_PAD):
            grp = slice(g * HEAD_PAD, (g + 1) * HEAD_PAD)
            out = slice(cols.start + grp.start, cols.start + grp.stop)
            q_out[:, out] = (_rope_group(q[:, grp], tables) * q_scale).astype(BF16)

    def k_chunk(cols):
        k_nope = _dot(c_n, wk_ref[:, cols])
        for g in range(MXU_COLS // HEAD_PAD):
            grp = slice(g * HEAD_PAD, (g + 1) * HEAD_PAD)
            out = slice(cols.start + grp.start, cols.start + grp.stop)
            k_out[:, out] = (k_nope[:, grp] + k_pe).astype(BF16)

    def v_chunk(cols):
        v_out[:, cols] = _dot(c_n, wv_ref[:, cols]).astype(BF16)

    @pl.when((pl.program_id(0) * tm) % seq == 0)
    def _():
        xext_scr[0:CONV_HALO, :] = jnp.zeros((CONV_HALO, MLSTM_WIDTH), F32)

    def conv_chunk(cols):
        xext_scr[CONV_HALO:CONV_HALO + tm, cols] = proj(_C_XM, cols)
        conv = cb_ref[:, cols] + (xext_scr[CONV_HALO:CONV_HALO + tm, cols]
                                  * cw_ref[CONV_WIDTH - 1:CONV_WIDTH, cols])
        for j in range(1, CONV_WIDTH):
            conv = conv + (xext_scr[CONV_HALO - j:CONV_HALO - j + tm, cols]
                           * cw_ref[CONV_WIDTH - 1 - j:CONV_WIDTH - j, cols])
        xext_scr[0:CONV_HALO, cols] = xext_scr[tm:tm + CONV_HALO, cols]
        xc_out[:, cols] = (conv * jax.nn.sigmoid(conv)).astype(BF16)

    def vm_chunk(cols):
        vm_out[:, cols] = proj(_C_VM, cols).astype(BF16)

    def og_chunk(cols):
        og_out[:, cols] = jax.nn.sigmoid(proj(_C_OM, cols)).astype(BF16)

    light = ([(gate_chunk, c) for c in gate_cols[MIX_LEAD_CHUNKS:]]
             + [(vm_chunk, c) for c in col_chunks(MLSTM_WIDTH)]
             + [(v_chunk, c) for c in col_chunks(MLA_HEADS * MLA_V_DIM)])
    heavy = ([(q_chunk, c) for c in col_chunks(MLA_HEADS * HEAD_PAD)]
             + [(conv_chunk, c) for c in col_chunks(MLSTM_WIDTH)]
             + [(og_chunk, c) for c in col_chunks(MLSTM_WIDTH)]
             + [(k_chunk, c) for c in col_chunks(MLA_HEADS * HEAD_PAD)])
    for n in range(max(len(light), len(heavy))):
        for items in (light, heavy):
            if n < len(items):
                fn, cols = items[n]
                fn(cols)
    if_out[...] = (proj(_C_IF, slice(0, V7X_LANES)) + bif_ref[...]).T[0:GATE_ROWS, :]


def _mix_in(u, pos, win_r, qan, wq_r, kvn, wk_r, wv_r, bif, conv_w, conv_b, seq):
    t, d = u.shape
    tm = MIX_TM
    row = lambda i: (i, 0)
    n_in = win_r.shape[1]
    qk_w = MLA_HEADS * HEAD_PAD
    v_w = MLA_HEADS * MLA_V_DIM
    outs = [(qk_w, BF16), (qk_w, BF16), (v_w, BF16), (MLSTM_WIDTH, BF16), (MLSTM_WIDTH, BF16),
            (MLSTM_WIDTH, BF16), None, (N_BRANCHES * d, BF16)]
    out_specs = [pl.BlockSpec((tm, o[0]), row) if o else pl.BlockSpec((GATE_ROWS, tm), lambda i: (0, i))
                 for o in outs]
    out_shape = [jax.ShapeDtypeStruct((t, o[0]), o[1]) if o else jax.ShapeDtypeStruct((GATE_ROWS, t), F32)
                 for o in outs]
    return pl.pallas_call(
        functools.partial(_mixin_kernel, tm=tm, seq=seq),
        grid=(t // tm,),
        in_specs=[pl.BlockSpec((tm, d), row), pl.BlockSpec((tm, 1), row),
                  _resident((d, n_in)), _resident((1, MLA_Q_RANK)),
                  _resident((MLA_Q_RANK, qk_w)), _resident((1, MLA_KV_RANK)),
                  _resident((MLA_KV_RANK, qk_w)), _resident((MLA_KV_RANK, v_w)),
                  _resident((1, V7X_LANES)), _resident((CONV_WIDTH, MLSTM_WIDTH)),
                  _resident((1, MLSTM_WIDTH))],
        out_specs=out_specs,
        out_shape=out_shape,
        scratch_shapes=[pltpu.VMEM((tm, d), BF16), pltpu.VMEM((tm + CONV_HALO, MLSTM_WIDTH), F32)],
        compiler_params=_params(1),
        name="mix_in",
    )(u, pos, win_r, qan.reshape(1, -1), wq_r, kvn.reshape(1, -1), wk_r, wv_r, bif, conv_w,
      conv_b.reshape(1, -1))


def _attn_kernel(q_ref, k_ref, v_ref, o_ref, vaug_scr, s_scr, *, seq, tq):
    pair_v = 2 * MLA_V_DIM
    lane = lax.broadcasted_iota(jnp.int32, (1, pair_v), 1)
    vaug_scr[:, 0:pair_v] = v_ref[...]
    vaug_scr[:, pair_v:2 * pair_v] = jnp.where(
        lax.broadcasted_iota(jnp.int32, (seq, pair_v), 1) == 0, 1.0, 0.0).astype(BF16)
    r_i = lax.broadcasted_iota(jnp.int32, (tq, tq), 0)
    c_i = lax.broadcasted_iota(jnp.int32, (tq, tq), 1)
    causal = c_i <= r_i
    neg = jnp.finfo(F32).min
    def buffer(qi, j):
        return (qi % ATTN_DEPTH) * 2 + j

    def scores(qi):
        rows = slice(qi * tq, (qi + 1) * tq)
        maxima = []
        for j in range(2):
            grp = slice(j * HEAD_PAD, (j + 1) * HEAD_PAD)
            qh = q_ref[rows, grp]
            m_vec = None
            for cj in range(qi + 1):
                cols = slice(cj * tq, (cj + 1) * tq)
                s = lax.dot_general(qh, k_ref[cols, grp], (((1,), (1,)), ((), ())),
                                    preferred_element_type=F32)
                if cj == qi:
                    s = jnp.where(causal, s, neg)
                s_scr[buffer(qi, j), :, cols] = s
                for g in range(tq // V7X_LANES):
                    part = s[:, g * V7X_LANES:(g + 1) * V7X_LANES]
                    m_vec = part if m_vec is None else jnp.maximum(m_vec, part)
            maxima.append(jnp.max(m_vec, axis=-1, keepdims=True))
        return maxima

    def outputs(qi, maxima):
        outs = []
        for j in range(2):
            o_aug = None
            for cj in range(qi + 1):
                cols = slice(cj * tq, (cj + 1) * tq)
                p = jnp.exp2(s_scr[buffer(qi, j), :, cols] - maxima[j]).astype(BF16)
                part = _dot(p, vaug_scr[cols, :])
                o_aug = part if o_aug is None else o_aug + part
            outs.append(o_aug[:, 0:pair_v] * (1.0 / o_aug[:, pair_v:pair_v + 1]))
        o_ref[qi * tq:(qi + 1) * tq, :] = jnp.where(lane < MLA_V_DIM, outs[0], outs[1]).astype(BF16)

    order = list(reversed(range(seq // tq)))
    maxima = scores(order[0])
    for n, qi in enumerate(order):
        next_maxima = scores(order[n + 1]) if n + 1 < len(order) else None
        outputs(qi, maxima)
        maxima = next_maxima


def _attention(q, k, v, bsz, seq):
    t = q.shape[0]
    pair_qk = 2 * HEAD_PAD
    pair_v = 2 * MLA_V_DIM
    blk = lambda b, g: (b, g)
    return pl.pallas_call(
        functools.partial(_attn_kernel, seq=seq, tq=ATTN_TQ),
        grid=(bsz, MLA_HEADS // 2),
        in_specs=[pl.BlockSpec((seq, pair_qk), blk), pl.BlockSpec((seq, pair_qk), blk),
                  pl.BlockSpec((seq, pair_v), blk)],
        out_specs=pl.BlockSpec((seq, pair_v), blk),
        out_shape=jax.ShapeDtypeStruct((t, MLA_HEADS * MLA_V_DIM), BF16),
        scratch_shapes=[pltpu.VMEM((seq, 2 * pair_v), BF16),
                        pltpu.VMEM((2 * ATTN_DEPTH, ATTN_TQ, seq), F32)],
        compiler_params=_params(2),
        name="mla_attn",
    )(q, k, v)


def _log_sigmoid(x):
    return jnp.minimum(x, 0.0) - jnp.log1p(jnp.exp(-jnp.abs(x)))


def _chunk_scan(x, lane_in_chunk, chunk, combine, fill):
    step = 1
    while step < chunk:
        x = combine(x, jnp.where(lane_in_chunk >= step, pltpu.roll(x, step, axis=1), fill))
        step *= 2
    return x


def _mlstm_kernel(xc_ref, vm_ref, og_ref, if_ref, wq_ref, wk_ref, wkt_ref, hn_ref, o_ref,
                  q_scr, k_scr, kt_scr, vaug_scr, ct_scr, b_scr, src_scr, mi_scr, *, seq, chunk):
    dh = MLSTM_HEAD_DIM
    nh = MLSTM_HEADS
    nc = seq // chunk
    ones_col = jnp.where(lax.broadcasted_iota(jnp.int32, (seq, dh), 1) == 0, 1.0, 0.0).astype(BF16)
    for h in range(nh):
        hs = slice(h * dh, (h + 1) * dh)
        xc = xc_ref[:, hs]
        q_scr[h] = _dot(xc, wq_ref[h]).astype(BF16)
        k_scr[h] = (_dot(xc, wk_ref[h]) * (dh ** -0.5)).astype(BF16)
        kt = lax.dot_general(wkt_ref[h], xc, (((1,), (1,)), ((), ())),
                             preferred_element_type=F32) * (dh ** -0.5)
        for c in range(nc):
            kt_scr[h * nc + c] = kt[:, c * chunk:(c + 1) * chunk]
        vaug_scr[h, :, 0:dh] = vm_ref[:, hs]
        vaug_scr[h, :, dh:2 * dh] = ones_col
        ct_scr[h] = jnp.zeros((dh, 2 * dh), F32)

    gates_t = if_ref[...]
    lane_in_chunk = lax.broadcasted_iota(jnp.int32, (GATE_ROWS, seq), 1) % chunk
    b_all = _chunk_scan(_log_sigmoid(gates_t) * LOG2_E, lane_in_chunk, chunk, jnp.add, 0.0)
    b_all = pltpu.roll(b_all, nh, axis=0)
    src_all = gates_t * LOG2_E - b_all
    mi_all = b_all + _chunk_scan(src_all, lane_in_chunk, chunk, jnp.maximum, -jnp.inf)
    for c in range(nc):
        cs = slice(c * chunk, (c + 1) * chunk)
        b_scr[c] = b_all[:, cs]
        src_scr[c] = src_all[:, cs]
        mi_scr[c] = mi_all[:, cs]

    r_i = lax.broadcasted_iota(jnp.int32, (chunk, chunk), 0)
    c_i = lax.broadcasted_iota(jnp.int32, (chunk, chunk), 1)
    eye = r_i == c_i
    tril = c_i <= r_i
    last_lane = lax.broadcasted_iota(jnp.int32, (GATE_ROWS, chunk), 1) == chunk - 1

    def to_col(row):
        return jnp.sum(jnp.where(eye, row, 0.0), axis=1, keepdims=True)

    def chunk_step(c, m_prev):
        r0 = pl.multiple_of(c * chunk, chunk)
        b_c = b_scr[c]
        src_c = src_scr[c]
        g = jnp.sum(jnp.where(last_lane, b_c, 0.0), axis=1, keepdims=True)
        m_new = jnp.maximum(g + m_prev, jnp.max(g + src_c, axis=1, keepdims=True))
        decay = jnp.exp2(g + m_prev - m_new)
        e_rows = jnp.exp2(g + src_c - m_new)
        m_t = jnp.maximum(b_c + m_prev, mi_scr[c])
        u_rows = b_c - m_t
        nm_rows = jnp.exp2(-m_t)
        for h in range(nh):
            hs = slice(h * dh, (h + 1) * dh)
            row = slice(h, h + 1)
            u_col = to_col(u_rows[row])
            q_c = q_scr[h, pl.ds(r0, chunk), :]
            v_c = vaug_scr[h, pl.ds(r0, chunk), :]
            s_qk = lax.dot_general(q_c, k_scr[h, pl.ds(r0, chunk), :], (((1,), (1,)), ((), ())),
                                   preferred_element_type=F32)
            d_mat = jnp.exp2(jnp.where(tril, u_col + src_c[row], -jnp.inf))
            q_in = (q_c.astype(F32) * jnp.exp2(u_col + m_prev[row])).astype(BF16)
            ct = ct_scr[h]
            tot = _dot(q_in, ct.astype(BF16)) + _dot((d_mat * s_qk).astype(BF16), v_c)
            den = jnp.maximum(jnp.abs(tot[:, dh:dh + 1]), to_col(nm_rows[row]))
            hh = tot[:, 0:dh] * (1.0 / den) * og_ref[pl.ds(r0, chunk), hs].astype(F32)
            o_ref[pl.ds(r0, chunk), hs] = _rms(hh, hn_ref[:, hs]).astype(BF16)
            c_loc = _dot((kt_scr[h * nc + c] * e_rows[row]).astype(BF16), v_c)
            ct_scr[h] = decay[row] * ct + c_loc
        return m_new

    lax.fori_loop(0, nc, chunk_step, jnp.zeros((GATE_ROWS, 1), F32), unroll=2)


def _mlstm(xc, vm, og, ifg, wq, wk, head_norm, bsz, seq):
    t = xc.shape[0]
    dh = MLSTM_HEAD_DIM
    nh = MLSTM_HEADS
    nc = seq // MLSTM_CHUNK
    blk = lambda b: (b, 0)
    return pl.pallas_call(
        functools.partial(_mlstm_kernel, seq=seq, chunk=MLSTM_CHUNK),
        grid=(bsz,),
        in_specs=[pl.BlockSpec((seq, MLSTM_WIDTH), blk), pl.BlockSpec((seq, MLSTM_WIDTH), blk),
                  pl.BlockSpec((seq, MLSTM_WIDTH), blk),
                  pl.BlockSpec((GATE_ROWS, seq), lambda b: (0, b)),
                  _resident((nh, dh, dh)), _resident((nh, dh, dh)), _resident((nh, dh, dh)),
                  _resident((1, MLSTM_WIDTH))],
        out_specs=pl.BlockSpec((seq, MLSTM_WIDTH), blk),
        out_shape=jax.ShapeDtypeStruct((t, MLSTM_WIDTH), BF16),
        scratch_shapes=[pltpu.VMEM((nh, seq, dh), BF16),
                        pltpu.VMEM((nh, seq, dh), BF16),
                        pltpu.VMEM((nh * nc, dh, MLSTM_CHUNK), F32),
                        pltpu.VMEM((nh, seq, 2 * dh), BF16),
                        pltpu.VMEM((nh, dh, 2 * dh), F32),
                        pltpu.VMEM((nc, GATE_ROWS, MLSTM_CHUNK), F32),
                        pltpu.VMEM((nc, GATE_ROWS, MLSTM_CHUNK), F32),
                        pltpu.VMEM((nc, GATE_ROWS, MLSTM_CHUNK), F32)],
        compiler_params=_params(1),
        name="mlstm",
    )(xc, vm, og, ifg, wq, wk, jnp.swapaxes(wk, 1, 2), head_norm.reshape(1, -1))


def _pad_cols(w, lo, width):
    return jnp.pad(w, ((0, 0), (lo, width - lo - w.shape[1])))


def _layout_w_in(w_in):
    o = 0
    parts = {}
    for name, n in (("q", MLA_Q_RANK), ("ckv", MLA_KV_RANK), ("kpe", MLA_ROPE_DIM),
                    ("xm", MLSTM_WIDTH), ("vm", MLSTM_WIDTH), ("om", MLSTM_WIDTH),
                    ("i", MLSTM_HEADS), ("f", MLSTM_HEADS), ("gate", None)):
        n = w_in.shape[1] - o if n is None else n
        parts[name] = w_in[:, o:o + n]
        o += n
    gates_if = _pad_cols(jnp.concatenate([parts["i"], parts["f"]], axis=1), 0, V7X_LANES)
    return jnp.concatenate(
        [parts["q"], parts["ckv"], _pad_cols(parts["kpe"], ROPE_LO, HEAD_PAD), parts["xm"],
         parts["vm"], parts["om"], gates_if, parts["gate"]], axis=1).astype(BF16)


def _layout_w_q_b(w_q_b):
    w = w_q_b.reshape(MLA_Q_RANK, MLA_HEADS, MLA_QK_DIM)
    w = jnp.pad(w, ((0, 0), (0, 0), (0, HEAD_PAD - MLA_QK_DIM)))
    return w.reshape(MLA_Q_RANK, MLA_HEADS * HEAD_PAD).astype(BF16)


def _layout_w_kv_b(w_kv_b):
    w = w_kv_b.reshape(MLA_KV_RANK, MLA_HEADS, MLA_NOPE_DIM + MLA_V_DIM)
    wk = jnp.pad(w[:, :, :MLA_NOPE_DIM], ((0, 0), (0, 0), (0, HEAD_PAD - MLA_NOPE_DIM)))
    wv = w[:, :, MLA_NOPE_DIM:]
    return (wk.reshape(MLA_KV_RANK, MLA_HEADS * HEAD_PAD).astype(BF16),
            wv.reshape(MLA_KV_RANK, MLA_HEADS * MLA_V_DIM).astype(BF16))


def kernel(x, c, positions, w_ada, b_ada, norm_ff1, ff1_w_gate, ff1_w_up, ff1_w_down, norm_mix, w_in,
           q_a_norm, w_q_b, kv_a_norm, w_kv_b, conv_w, conv_b, w_q_m, w_k_m, b_i, b_f, mlstm_norm,
           w_mla_out, w_mlstm_out, w_o, norm_ff2, ff2_w_gate, ff2_w_up, ff2_w_down, norm_final):
    bsz, seq, d = x.shape
    t = bsz * seq
    depth = w_ada.shape[0]
    h = x.reshape(t, d)
    pos = positions.reshape(t, 1)
    for l in range(depth):
        mod3 = _adaln(c, w_ada[l], b_ada[l]).reshape(bsz, 1, N_MOD * d)
        h, u_mix = _ffn(h, mod3, 0, norm_ff1[l], ff1_w_gate[l].astype(BF16), ff1_w_up[l].astype(BF16),
                        ff1_w_down[l].astype(BF16), seq, next_norm_g=norm_mix[l], next_chunk0=3)
        wk_r, wv_r = _layout_w_kv_b(w_kv_b[l])
        bif = _pad_cols(jnp.concatenate([b_i[l], b_f[l]]).reshape(1, -1), 0, V7X_LANES)
        q, k, v, xc, vm, og, ifg, gates = _mix_in(
            u_mix, pos, _layout_w_in(w_in[l]), q_a_norm[l], _layout_w_q_b(w_q_b[l]),
            kv_a_norm[l], wk_r, wv_r, bif, conv_w[l], conv_b[l], seq)
        o_a = _attention(q, k, v, bsz, seq)
        o_b = _mlstm(xc, vm, og, ifg, w_q_m[l].astype(BF16), w_k_m[l].astype(BF16), mlstm_norm[l],
                     bsz, seq)
        mixer = (o_a, o_b, gates, 5, w_mla_out[l].astype(BF16), w_mlstm_out[l].astype(BF16),
                 w_o[l].astype(BF16))
        final_g = norm_final if l == depth - 1 else None
        h = _ffn(h, mod3, 6, norm_ff2[l], ff2_w_gate[l].astype(BF16), ff2_w_up[l].astype(BF16),
                 ff2_w_down[l].astype(BF16), seq, mixer=mixer, final_g=final_g)[0]
    return h.reshape(bsz, seq, d)
```

```python
import functools

import jax
import jax.numpy as jnp
from jax import lax
from jax.experimental import pallas as pl
from jax.experimental.pallas import tpu as pltpu

F32 = jnp.float32
BF16 = jnp.bfloat16

V7X_LANES = 128
MXU_COLS = 256
V7X_VMEM_BYTES = 64 * 1024 * 1024
VMEM_LIMIT_BYTES = V7X_VMEM_BYTES - 8 * 1024 * 1024

MLA_HEADS = 8
MLA_NOPE_DIM = 64
MLA_ROPE_DIM = 32
MLA_V_DIM = 64
MLA_QK_DIM = MLA_NOPE_DIM + MLA_ROPE_DIM
MLA_Q_RANK = 384
MLA_KV_RANK = 256
ROPE_THETA = 10000.0
MLSTM_HEADS = 4
MLSTM_HEAD_DIM = 128
MLSTM_WIDTH = MLSTM_HEADS * MLSTM_HEAD_DIM
CONV_WIDTH = 4
GATE_ROWS = 2 * MLSTM_HEADS
CONV_HALO = 8
N_BRANCHES = 2
N_MOD = 9
EPS = 1e-6
LOG2_E = 1.4426950408889634

HEAD_PAD = V7X_LANES
ROPE_LO = MLA_NOPE_DIM
ROPE_HALF = MLA_ROPE_DIM // 2

FFN_TM = 1024
FFN_TM_MIXER = 512
FFN_TF = 256
MIX_TM = 1024
MIX_LEAD_CHUNKS = 3
ATTN_TQ = 256
ATTN_DEPTH = 4
MLSTM_CHUNK = 128


def _dot(a, b):
    return jnp.dot(a, b, preferred_element_type=F32)


def _rms(x, gain):
    return x * lax.rsqrt(jnp.mean(x * x, axis=-1, keepdims=True) + EPS) * gain


def _params(n_axes, flags=None):
    return pltpu.CompilerParams(
        dimension_semantics=("arbitrary",) * n_axes, vmem_limit_bytes=VMEM_LIMIT_BYTES, flags=flags)


def _resident(shape):
    zeros = (0,) * len(shape)
    return pl.BlockSpec(shape, lambda *_: zeros, pipeline_mode=pl.Buffered(1))


def _mod_spec(chunk, tm, seq, d):
    return pl.BlockSpec((None, 1, d), lambda i: ((i * tm) // seq, 0, chunk))


def _adaln_kernel(c_ref, w_ref, b_ref, o_ref):
    c = c_ref[...]
    sc = (c * jax.nn.sigmoid(c)).astype(BF16)
    o_ref[...] = _dot(sc, w_ref[...].astype(BF16)) + b_ref[...]


def _adaln(c, w_ada, b_ada):
    bsz, d = c.shape
    n = w_ada.shape[1]
    tn = d
    return pl.pallas_call(
        _adaln_kernel,
        grid=(n // tn,),
        in_specs=[pl.BlockSpec((bsz, d), lambda j: (0, 0)),
                  pl.BlockSpec((d, tn), lambda j: (0, j)),
                  pl.BlockSpec((1, tn), lambda j: (0, j))],
        out_specs=pl.BlockSpec((bsz, tn), lambda j: (0, j)),
        out_shape=jax.ShapeDtypeStruct((bsz, n), F32),
        compiler_params=_params(1),
        name="adaln",
    )(c, w_ada, b_ada.reshape(1, n))


def _modulated_norm(x, gain, shift, scale):
    return _rms(x, gain) * (1.0 + scale) + shift


def _ffn_kernel(*refs, d_ff, tf, mixer_prologue, epilogue):
    refs = list(refs)
    if mixer_prologue:
        h_ref, oa_ref, ob_ref, bg_ref, gtm_ref, wa_ref, wb_ref, wo_ref = refs[:8]
        refs = refs[8:]
    else:
        x_ref = refs.pop(0)
    g_ref, sh_ref, sc_ref, gt_ref, wg_ref, wu_ref, wd_ref = refs[:7]
    refs = refs[7:]
    if epilogue == "final_norm":
        gf_ref = refs.pop(0)
    elif epilogue == "next_mod":
        gn_ref, shn_ref, scn_ref = refs[:3]
        refs = refs[3:]
    o_ref = refs.pop(0)
    if epilogue == "next_mod":
        un_ref = refs.pop(0)
    u_scr, a_scr = refs[:2]

    if mixer_prologue:
        d = h_ref.shape[1]
        y_a = _dot(oa_ref[...], wa_ref[...])
        y_b = _dot(ob_ref[...], wb_ref[...])
        y = (jax.nn.sigmoid(bg_ref[:, 0:d].astype(F32)) * y_a
             + jax.nn.sigmoid(bg_ref[:, d:2 * d].astype(F32)) * y_b)
        x_ref = refs[2]
        x_ref[...] = h_ref[...] + gtm_ref[...] * _dot(y.astype(BF16), wo_ref[...])

    u_scr[...] = _modulated_norm(x_ref[...], g_ref[...], sh_ref[...], sc_ref[...]).astype(BF16)
    for j in range(d_ff // tf):
        cols = slice(j * tf, (j + 1) * tf)
        g = _dot(u_scr[...], wg_ref[:, cols])
        up = _dot(u_scr[...], wu_ref[:, cols])
        a_scr[:, cols] = (g * jax.nn.sigmoid(g) * up).astype(BF16)
    down = _dot(a_scr[...], wd_ref[...])
    out = x_ref[...] + (0.5 * gt_ref[...]) * down
    if epilogue == "final_norm":
        out = _rms(out, gf_ref[...])
    o_ref[...] = out
    if epilogue == "next_mod":
        un_ref[...] = _modulated_norm(out, gn_ref[...], shn_ref[...], scn_ref[...]).astype(BF16)


def _ffn(h, mod3, chunk0, norm_g, wg, wu, wd, seq, *, mixer=None, final_g=None, next_norm_g=None,
         next_chunk0=None):
    t, d = h.shape
    d_ff = wg.shape[1]
    tm = FFN_TM_MIXER if mixer is not None else FFN_TM
    row = lambda i: (i, 0)
    epilogue = "final_norm" if final_g is not None else "next_mod" if next_norm_g is not None else "plain"
    in_specs = [pl.BlockSpec((tm, d), row)]
    args = [h]
    scratch = [pltpu.VMEM((tm, d), BF16), pltpu.VMEM((tm, d_ff), BF16)]
    if mixer is not None:
        o_a, o_b, branch_gates, gate_chunk, w_a, w_b, w_o = mixer
        in_specs += [pl.BlockSpec((tm, o_a.shape[1]), row), pl.BlockSpec((tm, o_b.shape[1]), row),
                     pl.BlockSpec((tm, branch_gates.shape[1]), row),
                     _mod_spec(gate_chunk, tm, seq, d),
                     _resident(w_a.shape), _resident(w_b.shape), _resident(w_o.shape)]
        args += [o_a, o_b, branch_gates, mod3, w_a, w_b, w_o]
        scratch.append(pltpu.VMEM((tm, d), F32))
    in_specs += [_resident((1, d)), _mod_spec(chunk0, tm, seq, d), _mod_spec(chunk0 + 1, tm, seq, d),
                 _mod_spec(chunk0 + 2, tm, seq, d),
                 _resident((d, d_ff)), _resident((d, d_ff)), _resident((d_ff, d))]
    args += [norm_g.reshape(1, d), mod3, mod3, mod3, wg, wu, wd]
    out_specs = [pl.BlockSpec((tm, d), row)]
    out_shape = [jax.ShapeDtypeStruct((t, d), F32)]
    if epilogue == "final_norm":
        in_specs.append(_resident((1, d)))
        args.append(final_g.reshape(1, d))
    elif epilogue == "next_mod":
        in_specs += [_resident((1, d)), _mod_spec(next_chunk0, tm, seq, d),
                     _mod_spec(next_chunk0 + 1, tm, seq, d)]
        args += [next_norm_g.reshape(1, d), mod3, mod3]
        out_specs.append(pl.BlockSpec((tm, d), row))
        out_shape.append(jax.ShapeDtypeStruct((t, d), BF16))
    return pl.pallas_call(
        functools.partial(_ffn_kernel, d_ff=d_ff, tf=FFN_TF, mixer_prologue=mixer is not None,
                          epilogue=epilogue),
        grid=(t // tm,),
        in_specs=in_specs,
        out_specs=out_specs,
        out_shape=out_shape,
        scratch_shapes=scratch,
        compiler_params=_params(1),
        name=("mixout_ffn_" if mixer is not None else "ffn_") + epilogue,
    )(*args)


_C_QLAT = 0
_C_CKV = _C_QLAT + MLA_Q_RANK
_C_KPE = _C_CKV + MLA_KV_RANK
_C_XM = _C_KPE + HEAD_PAD
_C_VM = _C_XM + MLSTM_WIDTH
_C_OM = _C_VM + MLSTM_WIDTH
_C_IF = _C_OM + MLSTM_WIDTH
_C_GATE = _C_IF + V7X_LANES


def _rope_tables(pos_ref, tm):
    n_blk = HEAD_PAD // MLA_ROPE_DIM
    rows = tm // n_blk
    lane = lax.broadcasted_iota(jnp.int32, (1, HEAD_PAD), 1)
    blk = lane // MLA_ROPE_DIM
    idx = (lane % ROPE_HALF).astype(F32)
    theta = jnp.full((1, HEAD_PAD), ROPE_THETA, F32)
    inv_freq = jnp.exp(-(idx / ROPE_HALF) * jnp.log(theta))
    pos = pos_ref[0:rows, :].astype(F32)
    for b in range(1, n_blk):
        pos = jnp.where(blk == b, pos_ref[b * rows:(b + 1) * rows, :].astype(F32), pos)
    ang = pos * inv_freq
    cos_p = jnp.cos(ang)
    sin_p = jnp.sin(ang)

    in_lo = (lane >= ROPE_LO) & (lane < ROPE_LO + ROPE_HALF)
    in_hi = (lane >= ROPE_LO + ROPE_HALF) & (lane < ROPE_LO + MLA_ROPE_DIM)
    cos_m, sin_up, sin_dn = [], [], []
    for b in range(n_blk):
        shift = (ROPE_LO - b * MLA_ROPE_DIM) % HEAD_PAD
        cos = cos_p if shift == 0 else pltpu.roll(cos_p, shift, axis=1)
        sin = sin_p if shift == 0 else pltpu.roll(sin_p, shift, axis=1)
        cos_m.append(jnp.where(lane < ROPE_LO, 1.0, jnp.where(in_lo | in_hi, cos, 0.0)))
        sin_up.append(jnp.where(in_lo, -sin, 0.0))
        sin_dn.append(jnp.where(in_hi, sin, 0.0))
    return tuple(jnp.concatenate(t, axis=0) for t in (cos_m, sin_up, sin_dn))


def _rope_group(x, tables):
    cos_m, sin_up, sin_dn = tables
    up = pltpu.roll(x, HEAD_PAD - ROPE_HALF, axis=1)
    dn = pltpu.roll(x, ROPE_HALF, axis=1)
    return x * cos_m + up * sin_up + dn * sin_dn


def _mixin_kernel(u_ref, pos_ref, win_ref, qan_ref, wq_ref, kvn_ref,
                  wk_ref, wv_ref, bif_ref, cw_ref, cb_ref,
                  q_out, k_out, v_out, xc_out, vm_out, og_out, if_out, gate_out, u_scr, xext_scr,
                  *, tm, seq):
    def col_chunks(width):
        return [slice(lo, min(lo + MXU_COLS, width)) for lo in range(0, width, MXU_COLS)]

    u_scr[...] = u_ref[...]

    def proj(lo, cols):
        return _dot(u_scr[...], win_ref[:, lo + cols.start:lo + cols.stop])

    def gate_chunk(cols):
        gate_out[:, cols] = proj(_C_GATE, cols).astype(BF16)

    gate_cols = col_chunks(gate_out.shape[1])
    q_lat = jnp.concatenate([proj(_C_QLAT, cols) for cols in col_chunks(MLA_Q_RANK)], axis=1)
    c_kv = proj(_C_CKV, slice(0, MLA_KV_RANK))
    k_pe_raw = proj(_C_KPE, slice(0, HEAD_PAD))
    for cols in gate_cols[:MIX_LEAD_CHUNKS]:
        gate_chunk(cols)
    tables = _rope_tables(pos_ref, tm)
    q_n = _rms(q_lat, qan_ref[...]).astype(BF16)
    c_n = _rms(c_kv, kvn_ref[...]).astype(BF16)
    k_pe = _rope_group(k_pe_raw, tables)

    q_scale = MLA_QK_DIM ** -0.5 * LOG2_E

    def q_chunk(cols):
        q = _dot(q_n, wq_ref[:, cols])
        for g in range(MXU_COLS // HEAD_PAD):
            grp = slice(g * HEAD_PAD, (g + 1) * HEAD_PAD)
            out = slice(cols.start + grp.start, cols.start + grp.stop)
            q_out[:, out] = (_rope_group(q[:, grp], tables) * q_scale).astype(BF16)

    def k_chunk(cols):
        k_nope = _dot(c_n, wk_ref[:, cols])
        for g in range(MXU_COLS // HEAD_PAD):
            grp = slice(g * HEAD_PAD, (g + 1) * HEAD_PAD)
            out = slice(cols.start + grp.start, cols.start + grp.stop)
            k_out[:, out] = (k_nope[:, grp] + k_pe).astype(BF16)

    def v_chunk(cols):
        v_out[:, cols] = _dot(c_n, wv_ref[:, cols]).astype(BF16)

    @pl.when((pl.program_id(0) * tm) % seq == 0)
    def _():
        xext_scr[0:CONV_HALO, :] = jnp.zeros((CONV_HALO, MLSTM_WIDTH), F32)

    def conv_chunk(cols):
        xext_scr[CONV_HALO:CONV_HALO + tm, cols] = proj(_C_XM, cols)
        conv = cb_ref[:, cols] + (xext_scr[CONV_HALO:CONV_HALO + tm, cols]
                                  * cw_ref[CONV_WIDTH - 1:CONV_WIDTH, cols])
        for j in range(1, CONV_WIDTH):
            conv = conv + (xext_scr[CONV_HALO - j:CONV_HALO - j + tm, cols]
                           * cw_ref[CONV_WIDTH - 1 - j:CONV_WIDTH - j, cols])
        xext_scr[0:CONV_HALO, cols] = xext_scr[tm:tm + CONV_HALO, cols]
        xc_out[:, cols] = (conv * jax.nn.sigmoid(conv)).astype(BF16)

    def vm_chunk(cols):
        vm_out[:, cols] = proj(_C_VM, cols).astype(BF16)

    def og_chunk(cols):
        og_out[:, cols] = jax.nn.sigmoid(proj(_C_OM, cols)).astype(BF16)

    light = ([(gate_chunk, c) for c in gate_cols[MIX_LEAD_CHUNKS:]]
             + [(vm_chunk, c) for c in col_chunks(MLSTM_WIDTH)]
             + [(v_chunk, c) for c in col_chunks(MLA_HEADS * MLA_V_DIM)])
    heavy = ([(q_chunk, c) for c in col_chunks(MLA_HEADS * HEAD_PAD)]
             + [(conv_chunk, c) for c in col_chunks(MLSTM_WIDTH)]
             + [(og_chunk, c) for c in col_chunks(MLSTM_WIDTH)]
             + [(k_chunk, c) for c in col_chunks(MLA_HEADS * HEAD_PAD)])
    for n in range(max(len(light), len(heavy))):
        for items in (light, heavy):
            if n < len(items):
                fn, cols = items[n]
                fn(cols)
    if_out[...] = (proj(_C_IF, slice(0, V7X_LANES)) + bif_ref[...]).T[0:GATE_ROWS, :]


def _mix_in(u, pos, win_r, qan, wq_r, kvn, wk_r, wv_r, bif, conv_w, conv_b, seq):
    t, d = u.shape
    tm = MIX_TM
    row = lambda i: (i, 0)
    n_in = win_r.shape[1]
    qk_w = MLA_HEADS * HEAD_PAD
    v_w = MLA_HEADS * MLA_V_DIM
    outs = [(qk_w, BF16), (qk_w, BF16), (v_w, BF16), (MLSTM_WIDTH, BF16), (MLSTM_WIDTH, BF16),
            (MLSTM_WIDTH, BF16), None, (N_BRANCHES * d, BF16)]
    out_specs = [pl.BlockSpec((tm, o[0]), row) if o else pl.BlockSpec((GATE_ROWS, tm), lambda i: (0, i))
                 for o in outs]
    out_shape = [jax.ShapeDtypeStruct((t, o[0]), o[1]) if o else jax.ShapeDtypeStruct((GATE_ROWS, t), F32)
                 for o in outs]
    return pl.pallas_call(
        functools.partial(_mixin_kernel, tm=tm, seq=seq),
        grid=(t // tm,),
        in_specs=[pl.BlockSpec((tm, d), row), pl.BlockSpec((tm, 1), row),
                  _resident((d, n_in)), _resident((1, MLA_Q_RANK)),
                  _resident((MLA_Q_RANK, qk_w)), _resident((1, MLA_KV_RANK)),
                  _resident((MLA_KV_RANK, qk_w)), _resident((MLA_KV_RANK, v_w)),
                  _resident((1, V7X_LANES)), _resident((CONV_WIDTH, MLSTM_WIDTH)),
                  _resident((1, MLSTM_WIDTH))],
        out_specs=out_specs,
        out_shape=out_shape,
        scratch_shapes=[pltpu.VMEM((tm, d), BF16), pltpu.VMEM((tm + CONV_HALO, MLSTM_WIDTH), F32)],
        compiler_params=_params(1),
        name="mix_in",
    )(u, pos, win_r, qan.reshape(1, -1), wq_r, kvn.reshape(1, -1), wk_r, wv_r, bif, conv_w,
      conv_b.reshape(1, -1))


def _attn_kernel(q_ref, k_ref, v_ref, o_ref, vaug_scr, s_scr, *, seq, tq):
    pair_v = 2 * MLA_V_DIM
    lane = lax.broadcasted_iota(jnp.int32, (1, pair_v), 1)
    vaug_scr[:, 0:pair_v] = v_ref[...]
    vaug_scr[:, pair_v:2 * pair_v] = jnp.where(
        lax.broadcasted_iota(jnp.int32, (seq, pair_v), 1) == 0, 1.0, 0.0).astype(BF16)
    r_i = lax.broadcasted_iota(jnp.int32, (tq, tq), 0)
    c_i = lax.broadcasted_iota(jnp.int32, (tq, tq), 1)
    causal = c_i <= r_i
    neg = jnp.finfo(F32).min
    def buffer(qi, j):
        return (qi % ATTN_DEPTH) * 2 + j

    def scores(qi):
        rows = slice(qi * tq, (qi + 1) * tq)
        maxima = []
        for j in range(2):
            grp = slice(j * HEAD_PAD, (j + 1) * HEAD_PAD)
            qh = q_ref[rows, grp]
            m_vec = None
            for cj in range(qi + 1):
                cols = slice(cj * tq, (cj + 1) * tq)
                s = lax.dot_general(qh, k_ref[cols, grp], (((1,), (1,)), ((), ())),
                                    preferred_element_type=F32)
                if cj == qi:
                    s = jnp.where(causal, s, neg)
                s_scr[buffer(qi, j), :, cols] = s
                for g in range(tq // V7X_LANES):
                    part = s[:, g * V7X_LANES:(g + 1) * V7X_LANES]
                    m_vec = part if m_vec is None else jnp.maximum(m_vec, part)
            maxima.append(jnp.max(m_vec, axis=-1, keepdims=True))
        return maxima

    def outputs(qi, maxima):
        outs = []
        for j in range(2):
            o_aug = None
            for cj in range(qi + 1):
                cols = slice(cj * tq, (cj + 1) * tq)
                p = jnp.exp2(s_scr[buffer(qi, j), :, cols] - maxima[j]).astype(BF16)
                part = _dot(p, vaug_scr[cols, :])
                o_aug = part if o_aug is None else o_aug + part
            outs.append(o_aug[:, 0:pair_v] * (1.0 / o_aug[:, pair_v:pair_v + 1]))
        o_ref[qi * tq:(qi + 1) * tq, :] = jnp.where(lane < MLA_V_DIM, outs[0], outs[1]).astype(BF16)

    order = list(reversed(range(seq // tq)))
    lead = ATTN_DEPTH - 1
    pending = [scores(qi) for qi in order[:lead]]
    for n, qi in enumerate(order):
        if n + lead < len(order):
            pending.append(scores(order[n + lead]))
        outputs(qi, pending.pop(0))


def _attention(q, k, v, bsz, seq):
    t = q.shape[0]
    pair_qk = 2 * HEAD_PAD
    pair_v = 2 * MLA_V_DIM
    blk = lambda b, g: (b, g)
    return pl.pallas_call(
        functools.partial(_attn_kernel, seq=seq, tq=ATTN_TQ),
        grid=(bsz, MLA_HEADS // 2),
        in_specs=[pl.BlockSpec((seq, pair_qk), blk), pl.BlockSpec((seq, pair_qk), blk),
                  pl.BlockSpec((seq, pair_v), blk)],
        out_specs=pl.BlockSpec((seq, pair_v), blk),
        out_shape=jax.ShapeDtypeStruct((t, MLA_HEADS * MLA_V_DIM), BF16),
        scratch_shapes=[pltpu.VMEM((seq, 2 * pair_v), BF16),
                        pltpu.VMEM((2 * ATTN_DEPTH, ATTN_TQ, seq), F32)],
        compiler_params=_params(2),
        name="mla_attn",
    )(q, k, v)


def _log_sigmoid(x):
    return jnp.minimum(x, 0.0) - jnp.log1p(jnp.exp(-jnp.abs(x)))


def _chunk_scan(x, lane_in_chunk, chunk, combine, fill):
    step = 1
    while step < chunk:
        x = combine(x, jnp.where(lane_in_chunk >= step, pltpu.roll(x, step, axis=1), fill))
        step *= 2
    return x


def _mlstm_kernel(xc_ref, vm_ref, og_ref, if_ref, wq_ref, wk_ref, wkt_ref, hn_ref, o_ref,
                  q_scr, k_scr, kt_scr, vaug_scr, ct_scr, b_scr, src_scr, mi_scr, *, seq, chunk):
    dh = MLSTM_HEAD_DIM
    nh = MLSTM_HEADS
    nc = seq // chunk
    ones_col = jnp.where(lax.broadcasted_iota(jnp.int32, (seq, dh), 1) == 0, 1.0, 0.0).astype(BF16)
    for h in range(nh):
        hs = slice(h * dh, (h + 1) * dh)
        xc = xc_ref[:, hs]
        q_scr[h] = _dot(xc, wq_ref[h]).astype(BF16)
        k_scr[h] = (_dot(xc, wk_ref[h]) * (dh ** -0.5)).astype(BF16)
        kt = lax.dot_general(wkt_ref[h], xc, (((1,), (1,)), ((), ())),
                             preferred_element_type=F32) * (dh ** -0.5)
        for c in range(nc):
            kt_scr[h * nc + c] = kt[:, c * chunk:(c + 1) * chunk]
        vaug_scr[h, :, 0:dh] = vm_ref[:, hs]
        vaug_scr[h, :, dh:2 * dh] = ones_col
        ct_scr[h] = jnp.zeros((dh, 2 * dh), F32)

    gates_t = if_ref[...]
    lane_in_chunk = lax.broadcasted_iota(jnp.int32, (GATE_ROWS, seq), 1) % chunk
    b_all = _chunk_scan(_log_sigmoid(gates_t) * LOG2_E, lane_in_chunk, chunk, jnp.add, 0.0)
    b_all = pltpu.roll(b_all, nh, axis=0)
    src_all = gates_t * LOG2_E - b_all
    mi_all = b_all + _chunk_scan(src_all, lane_in_chunk, chunk, jnp.maximum, -jnp.inf)
    for c in range(nc):
        cs = slice(c * chunk, (c + 1) * chunk)
        b_scr[c] = b_all[:, cs]
        src_scr[c] = src_all[:, cs]
        mi_scr[c] = mi_all[:, cs]

    r_i = lax.broadcasted_iota(jnp.int32, (chunk, chunk), 0)
    c_i = lax.broadcasted_iota(jnp.int32, (chunk, chunk), 1)
    eye = r_i == c_i
    tril = c_i <= r_i
    last_lane = lax.broadcasted_iota(jnp.int32, (GATE_ROWS, chunk), 1) == chunk - 1

    def to_col(row):
        return jnp.sum(jnp.where(eye, row, 0.0), axis=1, keepdims=True)

    def chunk_step(c, m_prev):
        r0 = pl.multiple_of(c * chunk, chunk)
        b_c = b_scr[c]
        src_c = src_scr[c]
        g = jnp.sum(jnp.where(last_lane, b_c, 0.0), axis=1, keepdims=True)
        m_new = jnp.maximum(g + m_prev, jnp.max(g + src_c, axis=1, keepdims=True))
        decay = jnp.exp2(g + m_prev - m_new)
        e_rows = jnp.exp2(g + src_c - m_new)
        m_t = jnp.maximum(b_c + m_prev, mi_scr[c])
        u_rows = b_c - m_t
        nm_rows = jnp.exp2(-m_t)
        for h in range(nh):
            hs = slice(h * dh, (h + 1) * dh)
            row = slice(h, h + 1)
            u_col = to_col(u_rows[row])
            q_c = q_scr[h, pl.ds(r0, chunk), :]
            v_c = vaug_scr[h, pl.ds(r0, chunk), :]
            s_qk = lax.dot_general(q_c, k_scr[h, pl.ds(r0, chunk), :], (((1,), (1,)), ((), ())),
                                   preferred_element_type=F32)
            d_mat = jnp.exp2(jnp.where(tril, u_col + src_c[row], -jnp.inf))
            q_in = (q_c.astype(F32) * jnp.exp2(u_col + m_prev[row])).astype(BF16)
            ct = ct_scr[h]
            tot = _dot(q_in, ct.astype(BF16)) + _dot((d_mat * s_qk).astype(BF16), v_c)
            den = jnp.maximum(jnp.abs(tot[:, dh:dh + 1]), to_col(nm_rows[row]))
            hh = tot[:, 0:dh] * (1.0 / den) * og_ref[pl.ds(r0, chunk), hs].astype(F32)
            o_ref[pl.ds(r0, chunk), hs] = _rms(hh, hn_ref[:, hs]).astype(BF16)
            c_loc = _dot((kt_scr[h * nc + c] * e_rows[row]).astype(BF16), v_c)
            ct_scr[h] = decay[row] * ct + c_loc
        return m_new

    lax.fori_loop(0, nc, chunk_step, jnp.zeros((GATE_ROWS, 1), F32), unroll=2)


def _mlstm(xc, vm, og, ifg, wq, wk, head_norm, bsz, seq):
    t = xc.shape[0]
    dh = MLSTM_HEAD_DIM
    nh = MLSTM_HEADS
    nc = seq // MLSTM_CHUNK
    blk = lambda b: (b, 0)
    return pl.pallas_call(
        functools.partial(_mlstm_kernel, seq=seq, chunk=MLSTM_CHUNK),
        grid=(bsz,),
        in_specs=[pl.BlockSpec((seq, MLSTM_WIDTH), blk), pl.BlockSpec((seq, MLSTM_WIDTH), blk),
                  pl.BlockSpec((seq, MLSTM_WIDTH), blk),
                  pl.BlockSpec((GATE_ROWS, seq), lambda b: (0, b)),
                  _resident((nh, dh, dh)), _resident((nh, dh, dh)), _resident((nh, dh, dh)),
                  _resident((1, MLSTM_WIDTH))],
        out_specs=pl.BlockSpec((seq, MLSTM_WIDTH), blk),
        out_shape=jax.ShapeDtypeStruct((t, MLSTM_WIDTH), BF16),
        scratch_shapes=[pltpu.VMEM((nh, seq, dh), BF16),
                        pltpu.VMEM((nh, seq, dh), BF16),
                        pltpu.VMEM((nh * nc, dh, MLSTM_CHUNK), F32),
                        pltpu.VMEM((nh, seq, 2 * dh), BF16),
                        pltpu.VMEM((nh, dh, 2 * dh), F32),
                        pltpu.VMEM((nc, GATE_ROWS, MLSTM_CHUNK), F32),
                        pltpu.VMEM((nc, GATE_ROWS, MLSTM_CHUNK), F32),
                        pltpu.VMEM((nc, GATE_ROWS, MLSTM_CHUNK), F32)],
        compiler_params=_params(1),
        name="mlstm",
    )(xc, vm, og, ifg, wq, wk, jnp.swapaxes(wk, 1, 2), head_norm.reshape(1, -1))


def _pad_cols(w, lo, width):
    return jnp.pad(w, ((0, 0), (lo, width - lo - w.shape[1])))


def _layout_w_in(w_in):
    o = 0
    parts = {}
    for name, n in (("q", MLA_Q_RANK), ("ckv", MLA_KV_RANK), ("kpe", MLA_ROPE_DIM),
                    ("xm", MLSTM_WIDTH), ("vm", MLSTM_WIDTH), ("om", MLSTM_WIDTH),
                    ("i", MLSTM_HEADS), ("f", MLSTM_HEADS), ("gate", None)):
        n = w_in.shape[1] - o if n is None else n
        parts[name] = w_in[:, o:o + n]
        o += n
    gates_if = _pad_cols(jnp.concatenate([parts["i"], parts["f"]], axis=1), 0, V7X_LANES)
    return jnp.concatenate(
        [parts["q"], parts["ckv"], _pad_cols(parts["kpe"], ROPE_LO, HEAD_PAD), parts["xm"],
         parts["vm"], parts["om"], gates_if, parts["gate"]], axis=1).astype(BF16)


def _layout_w_q_b(w_q_b):
    w = w_q_b.reshape(MLA_Q_RANK, MLA_HEADS, MLA_QK_DIM)
    w = jnp.pad(w, ((0, 0), (0, 0), (0, HEAD_PAD - MLA_QK_DIM)))
    return w.reshape(MLA_Q_RANK, MLA_HEADS * HEAD_PAD).astype(BF16)


def _layout_w_kv_b(w_kv_b):
    w = w_kv_b.reshape(MLA_KV_RANK, MLA_HEADS, MLA_NOPE_DIM + MLA_V_DIM)
    wk = jnp.pad(w[:, :, :MLA_NOPE_DIM], ((0, 0), (0, 0), (0, HEAD_PAD - MLA_NOPE_DIM)))
    wv = w[:, :, MLA_NOPE_DIM:]
    return (wk.reshape(MLA_KV_RANK, MLA_HEADS * HEAD_PAD).astype(BF16),
            wv.reshape(MLA_KV_RANK, MLA_HEADS * MLA_V_DIM).astype(BF16))


def kernel(x, c, positions, w_ada, b_ada, norm_ff1, ff1_w_gate, ff1_w_up, ff1_w_down, norm_mix, w_in,
           q_a_norm, w_q_b, kv_a_norm, w_kv_b, conv_w, conv_b, w_q_m, w_k_m, b_i, b_f, mlstm_norm,
           w_mla_out, w_mlstm_out, w_o, norm_ff2, ff2_w_gate, ff2_w_up, ff2_w_down, norm_final):
    bsz, seq, d = x.shape
    t = bsz * seq
    depth = w_ada.shape[0]
    h = x.reshape(t, d)
    pos = positions.reshape(t, 1)
    for l in range(depth):
        mod3 = _adaln(c, w_ada[l], b_ada[l]).reshape(bsz, 1, N_MOD * d)
        h, u_mix = _ffn(h, mod3, 0, norm_ff1[l], ff1_w_gate[l].astype(BF16), ff1_w_up[l].astype(BF16),
                        ff1_w_down[l].astype(BF16), seq, next_norm_g=norm_mix[l], next_chunk0=3)
        wk_r, wv_r = _layout_w_kv_b(w_kv_b[l])
        bif = _pad_cols(jnp.concatenate([b_i[l], b_f[l]]).reshape(1, -1), 0, V7X_LANES)
        q, k, v, xc, vm, og, ifg, gates = _mix_in(
            u_mix, pos, _layout_w_in(w_in[l]), q_a_norm[l], _layout_w_q_b(w_q_b[l]),
            kv_a_norm[l], wk_r, wv_r, bif, conv_w[l], conv_b[l], seq)
        o_a = _attention(q, k, v, bsz, seq)
        o_b = _mlstm(xc, vm, og, ifg, w_q_m[l].astype(BF16), w_k_m[l].astype(BF16), mlstm_norm[l],
                     bsz, seq)
        mixer = (o_a, o_b, gates, 5, w_mla_out[l].astype(BF16), w_mlstm_out[l].astype(BF16),
                 w_o[l].astype(BF16))
        final_g = norm_final if l == depth - 1 else None
        h = _ffn(h, mod3, 6, norm_ff2[l], ff2_w_gate[l].astype(BF16), ff2_w_up[l].astype(BF16),
                 ff2_w_down[l].astype(BF16), seq, mixer=mixer, final_g=final_g)[0]
    return h.reshape(bsz, seq, d)
```

```python
import functools

import jax
import jax.numpy as jnp
from jax import lax
from jax.experimental import pallas as pl
from jax.experimental.pallas import tpu as pltpu

F32 = jnp.float32
BF16 = jnp.bfloat16

V7X_LANES = 128
MXU_COLS = 256
V7X_VMEM_BYTES = 64 * 1024 * 1024
VMEM_LIMIT_BYTES = V7X_VMEM_BYTES - 8 * 1024 * 1024

MLA_HEADS = 8
MLA_NOPE_DIM = 64
MLA_ROPE_DIM = 32
MLA_V_DIM = 64
MLA_QK_DIM = MLA_NOPE_DIM + MLA_ROPE_DIM
MLA_Q_RANK = 384
MLA_KV_RANK = 256
ROPE_THETA = 10000.0
MLSTM_HEADS = 4
MLSTM_HEAD_DIM = 128
MLSTM_WIDTH = MLSTM_HEADS * MLSTM_HEAD_DIM
CONV_WIDTH = 4
GATE_ROWS = 2 * MLSTM_HEADS
CONV_HALO = 8
N_BRANCHES = 2
N_MOD = 9
EPS = 1e-6
LOG2_E = 1.4426950408889634

HEAD_PAD = V7X_LANES
ROPE_LO = MLA_NOPE_DIM
ROPE_HALF = MLA_ROPE_DIM // 2

FFN_TM = 1024
FFN_TM_MIXER = 512
FFN_TF = 256
MIX_TM = 1024
MIX_LEAD_CHUNKS = 3
ATTN_TQ = 256
ATTN_DEPTH = 4
MLSTM_CHUNK = 128


def _dot(a, b):
    return jnp.dot(a, b, preferred_element_type=F32)


def _rms(x, gain):
    return x * lax.rsqrt(jnp.mean(x * x, axis=-1, keepdims=True) + EPS) * gain


def _params(n_axes, flags=None):
    return pltpu.CompilerParams(
        dimension_semantics=("arbitrary",) * n_axes, vmem_limit_bytes=VMEM_LIMIT_BYTES, flags=flags)


def _resident(shape):
    zeros = (0,) * len(shape)
    return pl.BlockSpec(shape, lambda *_: zeros, pipeline_mode=pl.Buffered(1))


def _mod_spec(chunk, tm, seq, d):
    return pl.BlockSpec((None, 1, d), lambda i: ((i * tm) // seq, 0, chunk))


def _adaln_kernel(c_ref, w_ref, b_ref, o_ref):
    c = c_ref[...]
    sc = (c * jax.nn.sigmoid(c)).astype(BF16)
    o_ref[...] = _dot(sc, w_ref[...].astype(BF16)) + b_ref[...]


def _adaln(c, w_ada, b_ada):
    bsz, d = c.shape
    n = w_ada.shape[1]
    tn = d
    return pl.pallas_call(
        _adaln_kernel,
        grid=(n // tn,),
        in_specs=[pl.BlockSpec((bsz, d), lambda j: (0, 0)),
                  pl.BlockSpec((d, tn), lambda j: (0, j)),
                  pl.BlockSpec((1, tn), lambda j: (0, j))],
        out_specs=pl.BlockSpec((bsz, tn), lambda j: (0, j)),
        out_shape=jax.ShapeDtypeStruct((bsz, n), F32),
        compiler_params=_params(1),
        name="adaln",
    )(c, w_ada, b_ada.reshape(1, n))


def _modulated_norm(x, gain, shift, scale):
    return _rms(x, gain) * (1.0 + scale) + shift


def _ffn_kernel(*refs, d_ff, tf, mixer_prologue, epilogue):
    refs = list(refs)
    if mixer_prologue:
        h_ref, oa_ref, ob_ref, bg_ref, gtm_ref, wa_ref, wb_ref, wo_ref = refs[:8]
        refs = refs[8:]
    else:
        x_ref = refs.pop(0)
    g_ref, sh_ref, sc_ref, gt_ref, wg_ref, wu_ref, wd_ref = refs[:7]
    refs = refs[7:]
    if epilogue == "final_norm":
        gf_ref = refs.pop(0)
    elif epilogue == "next_mod":
        gn_ref, shn_ref, scn_ref = refs[:3]
        refs = refs[3:]
    o_ref = refs.pop(0)
    if epilogue == "next_mod":
        un_ref = refs.pop(0)
    u_scr, a_scr = refs[:2]

    if mixer_prologue:
        d = h_ref.shape[1]
        y_a = _dot(oa_ref[...], wa_ref[...])
        y_b = _dot(ob_ref[...], wb_ref[...])
        y = (jax.nn.sigmoid(bg_ref[:, 0:d].astype(F32)) * y_a
             + jax.nn.sigmoid(bg_ref[:, d:2 * d].astype(F32)) * y_b)
        x_ref = refs[2]
        x_ref[...] = h_ref[...] + gtm_ref[...] * _dot(y.astype(BF16), wo_ref[...])

    u_scr[...] = _modulated_norm(x_ref[...], g_ref[...], sh_ref[...], sc_ref[...]).astype(BF16)
    for j in range(d_ff // tf):
        cols = slice(j * tf, (j + 1) * tf)
        g = _dot(u_scr[...], wg_ref[:, cols])
        up = _dot(u_scr[...], wu_ref[:, cols])
        a_scr[:, cols] = (g * jax.nn.sigmoid(g) * up).astype(BF16)
    down = _dot(a_scr[...], wd_ref[...])
    out = x_ref[...] + (0.5 * gt_ref[...]) * down
    if epilogue == "final_norm":
        out = _rms(out, gf_ref[...])
    o_ref[...] = out
    if epilogue == "next_mod":
        un_ref[...] = _modulated_norm(out, gn_ref[...], shn_ref[...], scn_ref[...]).astype(BF16)


def _ffn(h, mod3, chunk0, norm_g, wg, wu, wd, seq, *, mixer=None, final_g=None, next_norm_g=None,
         next_chunk0=None):
    t, d = h.shape
    d_ff = wg.shape[1]
    tm = FFN_TM_MIXER if mixer is not None else FFN_TM
    row = lambda i: (i, 0)
    epilogue = "final_norm" if final_g is not None else "next_mod" if next_norm_g is not None else "plain"
    in_specs = [pl.BlockSpec((tm, d), row)]
    args = [h]
    scratch = [pltpu.VMEM((tm, d), BF16), pltpu.VMEM((tm, d_ff), BF16)]
    if mixer is not None:
        o_a, o_b, branch_gates, gate_chunk, w_a, w_b, w_o = mixer
        in_specs += [pl.BlockSpec((tm, o_a.shape[1]), row), pl.BlockSpec((tm, o_b.shape[1]), row),
                     pl.BlockSpec((tm, branch_gates.shape[1]), row),
                     _mod_spec(gate_chunk, tm, seq, d),
                     _resident(w_a.shape), _resident(w_b.shape), _resident(w_o.shape)]
        args += [o_a, o_b, branch_gates, mod3, w_a, w_b, w_o]
        scratch.append(pltpu.VMEM((tm, d), F32))
    in_specs += [_resident((1, d)), _mod_spec(chunk0, tm, seq, d), _mod_spec(chunk0 + 1, tm, seq, d),
                 _mod_spec(chunk0 + 2, tm, seq, d),
                 _resident((d, d_ff)), _resident((d, d_ff)), _resident((d_ff, d))]
    args += [norm_g.reshape(1, d), mod3, mod3, mod3, wg, wu, wd]
    out_specs = [pl.BlockSpec((tm, d), row)]
    out_shape = [jax.ShapeDtypeStruct((t, d), F32)]
    if epilogue == "final_norm":
        in_specs.append(_resident((1, d)))
        args.append(final_g.reshape(1, d))
    elif epilogue == "next_mod":
        in_specs += [_resident((1, d)), _mod_spec(next_chunk0, tm, seq, d),
                     _mod_spec(next_chunk0 + 1, tm, seq, d)]
        args += [next_norm_g.reshape(1, d), mod3, mod3]
        out_specs.append(pl.BlockSpec((tm, d), row))
        out_shape.append(jax.ShapeDtypeStruct((t, d), BF16))
    return pl.pallas_call(
        functools.partial(_ffn_kernel, d_ff=d_ff, tf=FFN_TF, mixer_prologue=mixer is not None,
                          epilogue=epilogue),
        grid=(t // tm,),
        in_specs=in_specs,
        out_specs=out_specs,
        out_shape=out_shape,
        scratch_shapes=scratch,
        compiler_params=_params(1),
        name=("mixout_ffn_" if mixer is not None else "ffn_") + epilogue,
    )(*args)


_C_QLAT = 0
_C_CKV = _C_QLAT + MLA_Q_RANK
_C_KPE = _C_CKV + MLA_KV_RANK
_C_XM = _C_KPE + HEAD_PAD
_C_VM = _C_XM + MLSTM_WIDTH
_C_OM = _C_VM + MLSTM_WIDTH
_C_IF = _C_OM + MLSTM_WIDTH
_C_GATE = _C_IF + V7X_LANES


def _rope_tables(pos_ref, tm):
    n_blk = HEAD_PAD // MLA_ROPE_DIM
    rows = tm // n_blk
    lane = lax.broadcasted_iota(jnp.int32, (1, HEAD_PAD), 1)
    blk = lane // MLA_ROPE_DIM
    idx = (lane % ROPE_HALF).astype(F32)
    theta = jnp.full((1, HEAD_PAD), ROPE_THETA, F32)
    inv_freq = jnp.exp(-(idx / ROPE_HALF) * jnp.log(theta))
    pos = pos_ref[0:rows, :].astype(F32)
    for b in range(1, n_blk):
        pos = jnp.where(blk == b, pos_ref[b * rows:(b + 1) * rows, :].astype(F32), pos)
    ang = pos * inv_freq
    cos_p = jnp.cos(ang)
    sin_p = jnp.sin(ang)

    in_lo = (lane >= ROPE_LO) & (lane < ROPE_LO + ROPE_HALF)
    in_hi = (lane >= ROPE_LO + ROPE_HALF) & (lane < ROPE_LO + MLA_ROPE_DIM)
    cos_m, sin_up, sin_dn = [], [], []
    for b in range(n_blk):
        shift = (ROPE_LO - b * MLA_ROPE_DIM) % HEAD_PAD
        cos = cos_p if shift == 0 else pltpu.roll(cos_p, shift, axis=1)
        sin = sin_p if shift == 0 else pltpu.roll(sin_p, shift, axis=1)
        cos_m.append(jnp.where(lane < ROPE_LO, 1.0, jnp.where(in_lo | in_hi, cos, 0.0)))
        sin_up.append(jnp.where(in_lo, -sin, 0.0))
        sin_dn.append(jnp.where(in_hi, sin, 0.0))
    return tuple(jnp.concatenate(t, axis=0) for t in (cos_m, sin_up, sin_dn))


def _rope_group(x, tables):
    cos_m, sin_up, sin_dn = tables
    up = pltpu.roll(x, HEAD_PAD - ROPE_HALF, axis=1)
    dn = pltpu.roll(x, ROPE_HALF, axis=1)
    return x * cos_m + up * sin_up + dn * sin_dn


def _mixin_kernel(u_ref, pos_ref, win_ref, qan_ref, wq_ref, kvn_ref,
                  wk_ref, wv_ref, bif_ref, cw_ref, cb_ref,
                  q_out, k_out, v_out, xc_out, vm_out, og_out, if_out, gate_out, u_scr, xext_scr,
                  *, tm, seq):
    def col_chunks(width):
        return [slice(lo, min(lo + MXU_COLS, width)) for lo in range(0, width, MXU_COLS)]

    u_scr[...] = u_ref[...]

    def proj(lo, cols):
        return _dot(u_scr[...], win_ref[:, lo + cols.start:lo + cols.stop])

    def gate_chunk(cols):
        gate_out[:, cols] = proj(_C_GATE, cols).astype(BF16)

    gate_cols = col_chunks(gate_out.shape[1])
    q_lat = jnp.concatenate([proj(_C_QLAT, cols) for cols in col_chunks(MLA_Q_RANK)], axis=1)
    c_kv = proj(_C_CKV, slice(0, MLA_KV_RANK))
    k_pe_raw = proj(_C_KPE, slice(0, HEAD_PAD))
    for cols in gate_cols[:MIX_LEAD_CHUNKS]:
        gate_chunk(cols)
    tables = _rope_tables(pos_ref, tm)
    q_n = _rms(q_lat, qan_ref[...]).astype(BF16)
    c_n = _rms(c_kv, kvn_ref[...]).astype(BF16)
    k_pe = _rope_group(k_pe_raw, tables)

    q_scale = MLA_QK_DIM ** -0.5 * LOG2_E

    def q_chunk(cols):
        q = _dot(q_n, wq_ref[:, cols])
        for g in range(MXU_COLS // HEAD_PAD):
            grp = slice(g * HEAD_PAD, (g + 1) * HEAD_PAD)
            out = slice(cols.start + grp.start, cols.start + grp.stop)
            q_out[:, out] = (_rope_group(q[:, grp], tables) * q_scale).astype(BF16)

    def k_chunk(cols):
        k_nope = _dot(c_n, wk_ref[:, cols])
        for g in range(MXU_COLS // HEAD_PAD):
            grp = slice(g * HEAD_PAD, (g + 1) * HEAD_PAD)
            out = slice(cols.start + grp.start, cols.start + grp.stop)
            k_out[:, out] = (k_nope[:, grp] + k_pe).astype(BF16)

    def v_chunk(cols):
        v_out[:, cols] = _dot(c_n, wv_ref[:, cols]).astype(BF16)

    @pl.when((pl.program_id(0) * tm) % seq == 0)
    def _():
        xext_scr[0:CONV_HALO, :] = jnp.zeros((CONV_HALO, MLSTM_WIDTH), F32)

    def conv_chunk(cols):
        xext_scr[CONV_HALO:CONV_HALO + tm, cols] = proj(_C_XM, cols)
        conv = cb_ref[:, cols] + (xext_scr[CONV_HALO:CONV_HALO + tm, cols]
                                  * cw_ref[CONV_WIDTH - 1:CONV_WIDTH, cols])
        for j in range(1, CONV_WIDTH):
            conv = conv + (xext_scr[CONV_HALO - j:CONV_HALO - j + tm, cols]
                           * cw_ref[CONV_WIDTH - 1 - j:CONV_WIDTH - j, cols])
        xext_scr[0:CONV_HALO, cols] = xext_scr[tm:tm + CONV_HALO, cols]
        xc_out[:, cols] = (conv * jax.nn.sigmoid(conv)).astype(BF16)

    def vm_chunk(cols):
        vm_out[:, cols] = proj(_C_VM, cols).astype(BF16)

    def og_chunk(cols):
        og_out[:, cols] = jax.nn.sigmoid(proj(_C_OM, cols)).astype(BF16)

    light = ([(gate_chunk, c) for c in gate_cols[MIX_LEAD_CHUNKS:]]
             + [(vm_chunk, c) for c in col_chunks(MLSTM_WIDTH)]
             + [(v_chunk, c) for c in col_chunks(MLA_HEADS * MLA_V_DIM)])
    heavy = ([(q_chunk, c) for c in col_chunks(MLA_HEADS * HEAD_PAD)]
             + [(conv_chunk, c) for c in col_chunks(MLSTM_WIDTH)]
             + [(og_chunk, c) for c in col_chunks(MLSTM_WIDTH)]
             + [(k_chunk, c) for c in col_chunks(MLA_HEADS * HEAD_PAD)])
    for n in range(max(len(light), len(heavy))):
        for items in (light, heavy):
            if n < len(items):
                fn, cols = items[n]
                fn(cols)
    if_out[...] = (proj(_C_IF, slice(0, V7X_LANES)) + bif_ref[...]).T[0:GATE_ROWS, :]


def _mix_in(u, pos, win_r, qan, wq_r, kvn, wk_r, wv_r, bif, conv_w, conv_b, seq):
    t, d = u.shape
    tm = MIX_TM
    row = lambda i: (i, 0)
    n_in = win_r.shape[1]
    qk_w = MLA_HEADS * HEAD_PAD
    v_w = MLA_HEADS * MLA_V_DIM
    outs = [(qk_w, BF16), (qk_w, BF16), (v_w, BF16), (MLSTM_WIDTH, BF16), (MLSTM_WIDTH, BF16),
            (MLSTM_WIDTH, BF16), None, (N_BRANCHES * d, BF16)]
    out_specs = [pl.BlockSpec((tm, o[0]), row) if o else pl.BlockSpec((GATE_ROWS, tm), lambda i: (0, i))
                 for o in outs]
    out_shape = [jax.ShapeDtypeStruct((t, o[0]), o[1]) if o else jax.ShapeDtypeStruct((GATE_ROWS, t), F32)
                 for o in outs]
    return pl.pallas_call(
        functools.partial(_mixin_kernel, tm=tm, seq=seq),
        grid=(t // tm,),
        in_specs=[pl.BlockSpec((tm, d), row), pl.BlockSpec((tm, 1), row),
                  _resident((d, n_in)), _resident((1, MLA_Q_RANK)),
                  _resident((MLA_Q_RANK, qk_w)), _resident((1, MLA_KV_RANK)),
                  _resident((MLA_KV_RANK, qk_w)), _resident((MLA_KV_RANK, v_w)),
                  _resident((1, V7X_LANES)), _resident((CONV_WIDTH, MLSTM_WIDTH)),
                  _resident((1, MLSTM_WIDTH))],
        out_specs=out_specs,
        out_shape=out_shape,
        scratch_shapes=[pltpu.VMEM((tm, d), BF16), pltpu.VMEM((tm + CONV_HALO, MLSTM_WIDTH), F32)],
        compiler_params=_params(1),
        name="mix_in",
    )(u, pos, win_r, qan.reshape(1, -1), wq_r, kvn.reshape(1, -1), wk_r, wv_r, bif, conv_w,
      conv_b.reshape(1, -1))


def _attn_kernel(q_ref, k_ref, v_ref, o_ref, vaug_scr, s_scr, *, seq, tq):
    pair_v = 2 * MLA_V_DIM
    lane = lax.broadcasted_iota(jnp.int32, (1, pair_v), 1)
    vaug_scr[:, 0:pair_v] = v_ref[...]
    vaug_scr[:, pair_v:2 * pair_v] = jnp.where(
        lax.broadcasted_iota(jnp.int32, (seq, pair_v), 1) == 0, 1.0, 0.0).astype(BF16)
    r_i = lax.broadcasted_iota(jnp.int32, (tq, tq), 0)
    c_i = lax.broadcasted_iota(jnp.int32, (tq, tq), 1)
    causal = c_i <= r_i
    neg = jnp.finfo(F32).min
    def buffer(qi, j):
        return (qi % ATTN_DEPTH) * 2 + j

    def scores(qi):
        rows = slice(qi * tq, (qi + 1) * tq)
        maxima = []
        for j in range(2):
            grp = slice(j * HEAD_PAD, (j + 1) * HEAD_PAD)
            qh = q_ref[rows, grp]
            m_vec = None
            for cj in range(qi + 1):
                cols = slice(cj * tq, (cj + 1) * tq)
                s = lax.dot_general(qh, k_ref[cols, grp], (((1,), (1,)), ((), ())),
                                    preferred_element_type=F32)
                if cj == qi:
                    s = jnp.where(causal, s, neg)
                s_scr[buffer(qi, j), :, cols] = s
                for g in range(tq // V7X_LANES):
                    part = s[:, g * V7X_LANES:(g + 1) * V7X_LANES]
                    m_vec = part if m_vec is None else jnp.maximum(m_vec, part)
            maxima.append(jnp.max(m_vec, axis=-1, keepdims=True))
        return maxima

    def outputs(qi, maxima):
        outs = []
        for j in range(2):
            o_aug = None
            for cj in range(qi + 1):
                cols = slice(cj * tq, (cj + 1) * tq)
                p = jnp.exp2(s_scr[buffer(qi, j), :, cols] - maxima[j]).astype(BF16)
                part = _dot(p, vaug_scr[cols, :])
                o_aug = part if o_aug is None else o_aug + part
            outs.append(o_aug[:, 0:pair_v] * (1.0 / o_aug[:, pair_v:pair_v + 1]))
        o_ref[qi * tq:(qi + 1) * tq, :] = jnp.where(lane < MLA_V_DIM, outs[0], outs[1]).astype(BF16)

    order = list(reversed(range(seq // tq)))
    lead = ATTN_DEPTH - 1
    pending = [scores(qi) for qi in order[:lead]]
    for n, qi in enumerate(order):
        if n + lead < len(order):
            pending.append(scores(order[n + lead]))
        outputs(qi, pending.pop(0))


def _attention(q, k, v, bsz, seq):
    t = q.shape[0]
    pair_qk = 2 * HEAD_PAD
    pair_v = 2 * MLA_V_DIM
    blk = lambda b, g: (b, g)
    return pl.pallas_call(
        functools.partial(_attn_kernel, seq=seq, tq=ATTN_TQ),
        grid=(bsz, MLA_HEADS // 2),
        in_specs=[pl.BlockSpec((seq, pair_qk), blk), pl.BlockSpec((seq, pair_qk), blk),
                  pl.BlockSpec((seq, pair_v), blk)],
        out_specs=pl.BlockSpec((seq, pair_v), blk),
        out_shape=jax.ShapeDtypeStruct((t, MLA_HEADS * MLA_V_DIM), BF16),
        scratch_shapes=[pltpu.VMEM((seq, 2 * pair_v), BF16),
                        pltpu.VMEM((2 * ATTN_DEPTH, ATTN_TQ, seq), F32)],
        compiler_params=_params(2),
        name="mla_attn",
    )(q, k, v)


def _log_sigmoid(x):
    return jnp.minimum(x, 0.0) - jnp.log1p(jnp.exp(-jnp.abs(x)))


def _chunk_scan(x, lane_in_chunk, chunk, combine, fill):
    step = 1
    while step < chunk:
        x = combine(x, jnp.where(lane_in_chunk >= step, pltpu.roll(x, step, axis=1), fill))
        step *= 2
    return x


def _mlstm_kernel(xc_ref, vm_ref, og_ref, if_ref, wq_ref, wk_ref, wkt_ref, hn_ref, o_ref,
                  q_scr, k_scr, kt_scr, vaug_scr, ct_scr, b_scr, src_scr, mi_scr, *, seq, chunk):
    dh = MLSTM_HEAD_DIM
    nh = MLSTM_HEADS
    nc = seq // chunk
    ones_col = jnp.where(lax.broadcasted_iota(jnp.int32, (seq, dh), 1) == 0, 1.0, 0.0).astype(BF16)
    for h in range(nh):
        hs = slice(h * dh, (h + 1) * dh)
        xc = xc_ref[:, hs]
        q_scr[h] = _dot(xc, wq_ref[h]).astype(BF16)
        k_scr[h] = (_dot(xc, wk_ref[h]) * (dh ** -0.5)).astype(BF16)
        kt = lax.dot_general(wkt_ref[h], xc, (((1,), (1,)), ((), ())),
                             preferred_element_type=F32) * (dh ** -0.5)
        for c in range(nc):
            kt_scr[h * nc + c] = kt[:, c * chunk:(c + 1) * chunk]
        vaug_scr[h, :, 0:dh] = vm_ref[:, hs]
        vaug_scr[h, :, dh:2 * dh] = ones_col
        ct_scr[h] = jnp.zeros((dh, 2 * dh), F32)

    gates_t = if_ref[...]
    lane_in_chunk = lax.broadcasted_iota(jnp.int32, (GATE_ROWS, seq), 1) % chunk
    b_all = _chunk_scan(_log_sigmoid(gates_t) * LOG2_E, lane_in_chunk, chunk, jnp.add, 0.0)
    b_all = pltpu.roll(b_all, nh, axis=0)
    src_all = gates_t * LOG2_E - b_all
    mi_all = b_all + _chunk_scan(src_all, lane_in_chunk, chunk, jnp.maximum, -jnp.inf)
    for c in range(nc):
        cs = slice(c * chunk, (c + 1) * chunk)
        b_scr[c] = b_all[:, cs]
        src_scr[c] = src_all[:, cs]
        mi_scr[c] = mi_all[:, cs]

    r_i = lax.broadcasted_iota(jnp.int32, (chunk, chunk), 0)
    c_i = lax.broadcasted_iota(jnp.int32, (chunk, chunk), 1)
    eye = r_i == c_i
    tril = c_i <= r_i
    last_lane = lax.broadcasted_iota(jnp.int32, (GATE_ROWS, chunk), 1) == chunk - 1

    def to_col(row):
        return jnp.sum(jnp.where(eye, row, 0.0), axis=1, keepdims=True)

    def chunk_step(c, m_prev):
        r0 = pl.multiple_of(c * chunk, chunk)
        b_c = b_scr[c]
        src_c = src_scr[c]
        g = jnp.sum(jnp.where(last_lane, b_c, 0.0), axis=1, keepdims=True)
        m_new = jnp.maximum(g + m_prev, jnp.max(g + src_c, axis=1, keepdims=True))
        decay = jnp.exp2(g + m_prev - m_new)
        e_rows = jnp.exp2(g + src_c - m_new)
        m_t = jnp.maximum(b_c + m_prev, mi_scr[c])
        u_rows = b_c - m_t
        nm_rows = jnp.exp2(-m_t)
        for h in range(nh):
            hs = slice(h * dh, (h + 1) * dh)
            row = slice(h, h + 1)
            u_col = to_col(u_rows[row])
            q_c = q_scr[h, pl.ds(r0, chunk), :]
            v_c = vaug_scr[h, pl.ds(r0, chunk), :]
            s_qk = lax.dot_general(q_c, k_scr[h, pl.ds(r0, chunk), :], (((1,), (1,)), ((), ())),
                                   preferred_element_type=F32)
            d_mat = jnp.exp2(jnp.where(tril, u_col + src_c[row], -jnp.inf))
            q_in = (q_c.astype(F32) * jnp.exp2(u_col + m_prev[row])).astype(BF16)
            ct = ct_scr[h]
            tot = _dot(q_in, ct.astype(BF16)) + _dot((d_mat * s_qk).astype(BF16), v_c)
            den = jnp.maximum(jnp.abs(tot[:, dh:dh + 1]), to_col(nm_rows[row]))
            hh = tot[:, 0:dh] * (1.0 / den) * og_ref[pl.ds(r0, chunk), hs].astype(F32)
            o_ref[pl.ds(r0, chunk), hs] = _rms(hh, hn_ref[:, hs]).astype(BF16)
            c_loc = _dot((kt_scr[h * nc + c] * e_rows[row]).astype(BF16), v_c)
            ct_scr[h] = decay[row] * ct + c_loc
        return m_new

    lax.fori_loop(0, nc, chunk_step, jnp.zeros((GATE_ROWS, 1), F32), unroll=2)


def _mlstm(xc, vm, og, ifg, wq, wk, head_norm, bsz, seq):
    t = xc.shape[0]
    dh = MLSTM_HEAD_DIM
    nh = MLSTM_HEADS
    nc = seq // MLSTM_CHUNK
    blk = lambda b: (b, 0)
    return pl.pallas_call(
        functools.partial(_mlstm_kernel, seq=seq, chunk=MLSTM_CHUNK),
        grid=(bsz,),
        in_specs=[pl.BlockSpec((seq, MLSTM_WIDTH), blk), pl.BlockSpec((seq, MLSTM_WIDTH), blk),
                  pl.BlockSpec((seq, MLSTM_WIDTH), blk),
                  pl.BlockSpec((GATE_ROWS, seq), lambda b: (0, b)),
                  _resident((nh, dh, dh)), _resident((nh, dh, dh)), _resident((nh, dh, dh)),
                  _resident((1, MLSTM_WIDTH))],
        out_specs=pl.BlockSpec((seq, MLSTM_WIDTH), blk),
        out_shape=jax.ShapeDtypeStruct((t, MLSTM_WIDTH), BF16),
        scratch_shapes=[pltpu.VMEM((nh, seq, dh), BF16),
                        pltpu.VMEM((nh, seq, dh), BF16),
                        pltpu.VMEM((nh * nc, dh, MLSTM_CHUNK), F32),
                        pltpu.VMEM((nh, seq, 2 * dh), BF16),
                        pltpu.VMEM((nh, dh, 2 * dh), F32),
                        pltpu.VMEM((nc, GATE_ROWS, MLSTM_CHUNK), F32),
                        pltpu.VMEM((nc, GATE_ROWS, MLSTM_CHUNK), F32),
                        pltpu.VMEM((nc, GATE_ROWS, MLSTM_CHUNK), F32)],
        compiler_params=_params(1),
        name="mlstm",
    )(xc, vm, og, ifg, wq, wk, jnp.swapaxes(wk, 1, 2), head_norm.reshape(1, -1))


def _pad_cols(w, lo, width):
    return jnp.pad(w, ((0, 0), (lo, width - lo - w.shape[1])))


def _w_in_moves(d_in):
    src_kpe = MLA_Q_RANK + MLA_KV_RANK
    src_xm = src_kpe + MLA_ROPE_DIM
    src_if = src_xm + 3 * MLSTM_WIDTH
    src_gate = src_if + 2 * MLSTM_HEADS
    return ((0, _C_QLAT, src_kpe),
            (src_kpe, _C_KPE + ROPE_LO, MLA_ROPE_DIM),
            (src_xm, _C_XM, 3 * MLSTM_WIDTH),
            (src_if, _C_IF, 2 * MLSTM_HEADS),
            (src_gate, _C_GATE, d_in - src_gate))


def _layout_w_in_kernel(w_ref, o_ref):
    o_ref[...] = jnp.zeros(o_ref.shape, BF16)
    for src, dst, width in _w_in_moves(w_ref.shape[1]):
        o_ref[:, dst:dst + width] = w_ref[:, src:src + width].astype(BF16)


def _layout_w_in(w_in):
    rows, d_in = w_in.shape
    d_out = _C_GATE + (d_in - _w_in_moves(d_in)[-1][0])
    tr = rows // 4
    return pl.pallas_call(
        _layout_w_in_kernel,
        grid=(rows // tr,),
        in_specs=[pl.BlockSpec((tr, d_in), lambda i: (i, 0))],
        out_specs=pl.BlockSpec((tr, d_out), lambda i: (i, 0)),
        out_shape=jax.ShapeDtypeStruct((rows, d_out), BF16),
        compiler_params=_params(1),
        name="layout_w_in",
    )(w_in)


def _layout_w_q_b(w_q_b):
    w = w_q_b.reshape(MLA_Q_RANK, MLA_HEADS, MLA_QK_DIM)
    w = jnp.pad(w, ((0, 0), (0, 0), (0, HEAD_PAD - MLA_QK_DIM)))
    return w.reshape(MLA_Q_RANK, MLA_HEADS * HEAD_PAD).astype(BF16)


def _layout_w_kv_b(w_kv_b):
    w = w_kv_b.reshape(MLA_KV_RANK, MLA_HEADS, MLA_NOPE_DIM + MLA_V_DIM)
    wk = jnp.pad(w[:, :, :MLA_NOPE_DIM], ((0, 0), (0, 0), (0, HEAD_PAD - MLA_NOPE_DIM)))
    wv = w[:, :, MLA_NOPE_DIM:]
    return (wk.reshape(MLA_KV_RANK, MLA_HEADS * HEAD_PAD).astype(BF16),
            wv.reshape(MLA_KV_RANK, MLA_HEADS * MLA_V_DIM).astype(BF16))


def kernel(x, c, positions, w_ada, b_ada, norm_ff1, ff1_w_gate, ff1_w_up, ff1_w_down, norm_mix, w_in,
           q_a_norm, w_q_b, kv_a_norm, w_kv_b, conv_w, conv_b, w_q_m, w_k_m, b_i, b_f, mlstm_norm,
           w_mla_out, w_mlstm_out, w_o, norm_ff2, ff2_w_gate, ff2_w_up, ff2_w_down, norm_final):
    bsz, seq, d = x.shape
    t = bsz * seq
    depth = w_ada.shape[0]
    h = x.reshape(t, d)
    pos = positions.reshape(t, 1)
    for l in range(depth):
        mod3 = _adaln(c, w_ada[l], b_ada[l]).reshape(bsz, 1, N_MOD * d)
        h, u_mix = _ffn(h, mod3, 0, norm_ff1[l], ff1_w_gate[l].astype(BF16), ff1_w_up[l].astype(BF16),
                        ff1_w_down[l].astype(BF16), seq, next_norm_g=norm_mix[l], next_chunk0=3)
        wk_r, wv_r = _layout_w_kv_b(w_kv_b[l])
        bif = _pad_cols(jnp.concatenate([b_i[l], b_f[l]]).reshape(1, -1), 0, V7X_LANES)
        q, k, v, xc, vm, og, ifg, gates = _mix_in(
            u_mix, pos, _layout_w_in(w_in[l]), q_a_norm[l], _layout_w_q_b(w_q_b[l]),
            kv_a_norm[l], wk_r, wv_r, bif, conv_w[l], conv_b[l], seq)
        o_a = _attention(q, k, v, bsz, seq)
        o_b = _mlstm(xc, vm, og, ifg, w_q_m[l].astype(BF16), w_k_m[l].astype(BF16), mlstm_norm[l],
                     bsz, seq)
        mixer = (o_a, o_b, gates, 5, w_mla_out[l].astype(BF16), w_mlstm_out[l].astype(BF16),
                 w_o[l].astype(BF16))
        final_g = norm_final if l == depth - 1 else None
        h = _ffn(h, mod3, 6, norm_ff2[l], ff2_w_gate[l].astype(BF16), ff2_w_up[l].astype(BF16),
                 ff2_w_down[l].astype(BF16), seq, mixer=mixer, final_g=final_g)[0]
    return h.reshape(bsz, seq, d)
```

```python
import functools

import jax
import jax.numpy as jnp
from jax import lax
from jax.experimental import pallas as pl
from jax.experimental.pallas import tpu as pltpu

F32 = jnp.float32
BF16 = jnp.bfloat16

V7X_LANES = 128
MXU_COLS = 256
V7X_VMEM_BYTES = 64 * 1024 * 1024
VMEM_LIMIT_BYTES = V7X_VMEM_BYTES - 8 * 1024 * 1024

MLA_HEADS = 8
MLA_NOPE_DIM = 64
MLA_ROPE_DIM = 32
MLA_V_DIM = 64
MLA_QK_DIM = MLA_NOPE_DIM + MLA_ROPE_DIM
MLA_Q_RANK = 384
MLA_KV_RANK = 256
ROPE_THETA = 10000.0
MLSTM_HEADS = 4
MLSTM_HEAD_DIM = 128
MLSTM_WIDTH = MLSTM_HEADS * MLSTM_HEAD_DIM
CONV_WIDTH = 4
GATE_ROWS = 2 * MLSTM_HEADS
CONV_HALO = 8
N_BRANCHES = 2
N_MOD = 9
EPS = 1e-6
LOG2_E = 1.4426950408889634

HEAD_PAD = V7X_LANES
ROPE_LO = MLA_NOPE_DIM
ROPE_HALF = MLA_ROPE_DIM // 2

FFN_TM = 1024
FFN_TM_MIXER = 512
FFN_TF = 256
MIX_TM = 1024
MIX_LEAD_CHUNKS = 3
ATTN_TQ = 256
ATTN_DEPTH = 4
MLSTM_CHUNK = 128


def _dot(a, b):
    return jnp.dot(a, b, preferred_element_type=F32)


def _rms(x, gain):
    return x * lax.rsqrt(jnp.mean(x * x, axis=-1, keepdims=True) + EPS) * gain


def _params(n_axes, flags=None):
    return pltpu.CompilerParams(
        dimension_semantics=("arbitrary",) * n_axes, vmem_limit_bytes=VMEM_LIMIT_BYTES, flags=flags)


def _resident(shape):
    zeros = (0,) * len(shape)
    return pl.BlockSpec(shape, lambda *_: zeros, pipeline_mode=pl.Buffered(1))


def _mod_spec(chunk, tm, seq, d):
    return pl.BlockSpec((None, 1, d), lambda i: ((i * tm) // seq, 0, chunk))


def _adaln_kernel(c_ref, w_ref, b_ref, o_ref):
    c = c_ref[...]
    sc = (c * jax.nn.sigmoid(c)).astype(BF16)
    o_ref[...] = _dot(sc, w_ref[...].astype(BF16)) + b_ref[...]


def _adaln(c, w_ada, b_ada):
    bsz, d = c.shape
    n = w_ada.shape[1]
    tn = d
    return pl.pallas_call(
        _adaln_kernel,
        grid=(n // tn,),
        in_specs=[pl.BlockSpec((bsz, d), lambda j: (0, 0)),
                  pl.BlockSpec((d, tn), lambda j: (0, j)),
                  pl.BlockSpec((1, tn), lambda j: (0, j))],
        out_specs=pl.BlockSpec((bsz, tn), lambda j: (0, j)),
        out_shape=jax.ShapeDtypeStruct((bsz, n), F32),
        compiler_params=_params(1),
        name="adaln",
    )(c, w_ada, b_ada.reshape(1, n))


def _modulated_norm(x, gain, shift, scale):
    return _rms(x, gain) * (1.0 + scale) + shift


def _ffn_kernel(*refs, d_ff, tf, mixer_prologue, epilogue):
    refs = list(refs)
    if mixer_prologue:
        h_ref, oa_ref, ob_ref, bg_ref, gtm_ref, wa_ref, wb_ref, wo_ref = refs[:8]
        refs = refs[8:]
    else:
        x_ref = refs.pop(0)
    g_ref, sh_ref, sc_ref, gt_ref, wg_ref, wu_ref, wd_ref = refs[:7]
    refs = refs[7:]
    if epilogue == "final_norm":
        gf_ref = refs.pop(0)
    elif epilogue == "next_mod":
        gn_ref, shn_ref, scn_ref = refs[:3]
        refs = refs[3:]
    o_ref = refs.pop(0)
    if epilogue == "next_mod":
        un_ref = refs.pop(0)
    u_scr, a_scr = refs[:2]

    if mixer_prologue:
        d = h_ref.shape[1]
        y_a = _dot(oa_ref[...], wa_ref[...])
        y_b = _dot(ob_ref[...], wb_ref[...])
        y = (jax.nn.sigmoid(bg_ref[:, 0:d].astype(F32)) * y_a
             + jax.nn.sigmoid(bg_ref[:, d:2 * d].astype(F32)) * y_b)
        x_ref = refs[2]
        x_ref[...] = h_ref[...] + gtm_ref[...] * _dot(y.astype(BF16), wo_ref[...])

    u_scr[...] = _modulated_norm(x_ref[...], g_ref[...], sh_ref[...], sc_ref[...]).astype(BF16)
    for j in range(d_ff // tf):
        cols = slice(j * tf, (j + 1) * tf)
        g = _dot(u_scr[...], wg_ref[:, cols])
        up = _dot(u_scr[...], wu_ref[:, cols])
        a_scr[:, cols] = (g * jax.nn.sigmoid(g) * up).astype(BF16)
    down = _dot(a_scr[...], wd_ref[...])
    out = x_ref[...] + (0.5 * gt_ref[...]) * down
    if epilogue == "final_norm":
        out = _rms(out, gf_ref[...])
    o_ref[...] = out
    if epilogue == "next_mod":
        un_ref[...] = _modulated_norm(out, gn_ref[...], shn_ref[...], scn_ref[...]).astype(BF16)


def _ffn(h, mod3, chunk0, norm_g, wg, wu, wd, seq, *, mixer=None, final_g=None, next_norm_g=None,
         next_chunk0=None):
    t, d = h.shape
    d_ff = wg.shape[1]
    tm = FFN_TM_MIXER if mixer is not None else FFN_TM
    row = lambda i: (i, 0)
    epilogue = "final_norm" if final_g is not None else "next_mod" if next_norm_g is not None else "plain"
    in_specs = [pl.BlockSpec((tm, d), row)]
    args = [h]
    scratch = [pltpu.VMEM((tm, d), BF16), pltpu.VMEM((tm, d_ff), BF16)]
    if mixer is not None:
        o_a, o_b, branch_gates, gate_chunk, w_a, w_b, w_o = mixer
        in_specs += [pl.BlockSpec((tm, o_a.shape[1]), row), pl.BlockSpec((tm, o_b.shape[1]), row),
                     pl.BlockSpec((tm, branch_gates.shape[1]), row),
                     _mod_spec(gate_chunk, tm, seq, d),
                     _resident(w_a.shape), _resident(w_b.shape), _resident(w_o.shape)]
        args += [o_a, o_b, branch_gates, mod3, w_a, w_b, w_o]
        scratch.append(pltpu.VMEM((tm, d), F32))
    in_specs += [_resident((1, d)), _mod_spec(chunk0, tm, seq, d), _mod_spec(chunk0 + 1, tm, seq, d),
                 _mod_spec(chunk0 + 2, tm, seq, d),
                 _resident((d, d_ff)), _resident((d, d_ff)), _resident((d_ff, d))]
    args += [norm_g.reshape(1, d), mod3, mod3, mod3, wg, wu, wd]
    out_specs = [pl.BlockSpec((tm, d), row)]
    out_shape = [jax.ShapeDtypeStruct((t, d), F32)]
    if epilogue == "final_norm":
        in_specs.append(_resident((1, d)))
        args.append(final_g.reshape(1, d))
    elif epilogue == "next_mod":
        in_specs += [_resident((1, d)), _mod_spec(next_chunk0, tm, seq, d),
                     _mod_spec(next_chunk0 + 1, tm, seq, d)]
        args += [next_norm_g.reshape(1, d), mod3, mod3]
        out_specs.append(pl.BlockSpec((tm, d), row))
        out_shape.append(jax.ShapeDtypeStruct((t, d), BF16))
    return pl.pallas_call(
        functools.partial(_ffn_kernel, d_ff=d_ff, tf=FFN_TF, mixer_prologue=mixer is not None,
                          epilogue=epilogue),
        grid=(t // tm,),
        in_specs=in_specs,
        out_specs=out_specs,
        out_shape=out_shape,
        scratch_shapes=scratch,
        compiler_params=_params(1),
        name=("mixout_ffn_" if mixer is not None else "ffn_") + epilogue,
    )(*args)


_C_QLAT = 0
_C_CKV = _C_QLAT + MLA_Q_RANK
_C_KPE = _C_CKV + MLA_KV_RANK
_C_XM = _C_KPE + HEAD_PAD
_C_VM = _C_XM + MLSTM_WIDTH
_C_OM = _C_VM + MLSTM_WIDTH
_C_IF = _C_OM + MLSTM_WIDTH
_C_GATE = _C_IF + V7X_LANES


def _rope_tables(pos_ref, tm):
    n_blk = HEAD_PAD // MLA_ROPE_DIM
    rows = tm // n_blk
    lane = lax.broadcasted_iota(jnp.int32, (1, HEAD_PAD), 1)
    blk = lane // MLA_ROPE_DIM
    idx = (lane % ROPE_HALF).astype(F32)
    theta = jnp.full((1, HEAD_PAD), ROPE_THETA, F32)
    inv_freq = jnp.exp(-(idx / ROPE_HALF) * jnp.log(theta))
    pos = pos_ref[0:rows, :].astype(F32)
    for b in range(1, n_blk):
        pos = jnp.where(blk == b, pos_ref[b * rows:(b + 1) * rows, :].astype(F32), pos)
    ang = pos * inv_freq
    cos_p = jnp.cos(ang)
    sin_p = jnp.sin(ang)

    in_lo = (lane >= ROPE_LO) & (lane < ROPE_LO + ROPE_HALF)
    in_hi = (lane >= ROPE_LO + ROPE_HALF) & (lane < ROPE_LO + MLA_ROPE_DIM)
    cos_m, sin_up, sin_dn = [], [], []
    for b in range(n_blk):
        shift = (ROPE_LO - b * MLA_ROPE_DIM) % HEAD_PAD
        cos = cos_p if shift == 0 else pltpu.roll(cos_p, shift, axis=1)
        sin = sin_p if shift == 0 else pltpu.roll(sin_p, shift, axis=1)
        cos_m.append(jnp.where(lane < ROPE_LO, 1.0, jnp.where(in_lo | in_hi, cos, 0.0)))
        sin_up.append(jnp.where(in_lo, -sin, 0.0))
        sin_dn.append(jnp.where(in_hi, sin, 0.0))
    return tuple(jnp.concatenate(t, axis=0) for t in (cos_m, sin_up, sin_dn))


def _rope_group(x, tables):
    cos_m, sin_up, sin_dn = tables
    up = pltpu.roll(x, HEAD_PAD - ROPE_HALF, axis=1)
    dn = pltpu.roll(x, ROPE_HALF, axis=1)
    return x * cos_m + up * sin_up + dn * sin_dn


def _mixin_kernel(u_ref, pos_ref, win_ref, qan_ref, wq_ref, kvn_ref,
                  wk_ref, wv_ref, bif_ref, cw_ref, cb_ref,
                  q_out, k_out, v_out, xc_out, vm_out, og_out, if_out, gate_out, u_scr, xext_scr,
                  *, tm, seq):
    def col_chunks(width):
        return [slice(lo, min(lo + MXU_COLS, width)) for lo in range(0, width, MXU_COLS)]

    u_scr[...] = u_ref[...]

    def proj(lo, cols):
        return _dot(u_scr[...], win_ref[:, lo + cols.start:lo + cols.stop])

    def gate_chunk(cols):
        gate_out[:, cols] = proj(_C_GATE, cols).astype(BF16)

    gate_cols = col_chunks(gate_out.shape[1])
    q_lat = jnp.concatenate([proj(_C_QLAT, cols) for cols in col_chunks(MLA_Q_RANK)], axis=1)
    c_kv = proj(_C_CKV, slice(0, MLA_KV_RANK))
    k_pe_raw = proj(_C_KPE, slice(0, HEAD_PAD))
    for cols in gate_cols[:MIX_LEAD_CHUNKS]:
        gate_chunk(cols)
    tables = _rope_tables(pos_ref, tm)
    q_n = _rms(q_lat, qan_ref[...]).astype(BF16)
    c_n = _rms(c_kv, kvn_ref[...]).astype(BF16)
    k_pe = _rope_group(k_pe_raw, tables)

    q_scale = MLA_QK_DIM ** -0.5 * LOG2_E

    def q_chunk(cols):
        q = _dot(q_n, wq_ref[:, cols])
        for g in range(MXU_COLS // HEAD_PAD):
            grp = slice(g * HEAD_PAD, (g + 1) * HEAD_PAD)
            out = slice(cols.start + grp.start, cols.start + grp.stop)
            q_out[:, out] = (_rope_group(q[:, grp], tables) * q_scale).astype(BF16)

    def k_chunk(cols):
        k_nope = _dot(c_n, wk_ref[:, cols])
        for g in range(MXU_COLS // HEAD_PAD):
            grp = slice(g * HEAD_PAD, (g + 1) * HEAD_PAD)
            out = slice(cols.start + grp.start, cols.start + grp.stop)
            k_out[:, out] = (k_nope[:, grp] + k_pe).astype(BF16)

    def v_chunk(cols):
        v_out[:, cols] = _dot(c_n, wv_ref[:, cols]).astype(BF16)

    @pl.when((pl.program_id(0) * tm) % seq == 0)
    def _():
        xext_scr[0:CONV_HALO, :] = jnp.zeros((CONV_HALO, MLSTM_WIDTH), F32)

    def conv_chunk(cols):
        xext_scr[CONV_HALO:CONV_HALO + tm, cols] = proj(_C_XM, cols)
        conv = cb_ref[:, cols] + (xext_scr[CONV_HALO:CONV_HALO + tm, cols]
                                  * cw_ref[CONV_WIDTH - 1:CONV_WIDTH, cols])
        for j in range(1, CONV_WIDTH):
            conv = conv + (xext_scr[CONV_HALO - j:CONV_HALO - j + tm, cols]
                           * cw_ref[CONV_WIDTH - 1 - j:CONV_WIDTH - j, cols])
        xext_scr[0:CONV_HALO, cols] = xext_scr[tm:tm + CONV_HALO, cols]
        xc_out[:, cols] = (conv * jax.nn.sigmoid(conv)).astype(BF16)

    def vm_chunk(cols):
        vm_out[:, cols] = proj(_C_VM, cols).astype(BF16)

    def og_chunk(cols):
        og_out[:, cols] = jax.nn.sigmoid(proj(_C_OM, cols)).astype(BF16)

    light = ([(gate_chunk, c) for c in gate_cols[MIX_LEAD_CHUNKS:]]
             + [(vm_chunk, c) for c in col_chunks(MLSTM_WIDTH)]
             + [(v_chunk, c) for c in col_chunks(MLA_HEADS * MLA_V_DIM)])
    heavy = ([(q_chunk, c) for c in col_chunks(MLA_HEADS * HEAD_PAD)]
             + [(conv_chunk, c) for c in col_chunks(MLSTM_WIDTH)]
             + [(og_chunk, c) for c in col_chunks(MLSTM_WIDTH)]
             + [(k_chunk, c) for c in col_chunks(MLA_HEADS * HEAD_PAD)])
    for n in range(max(len(light), len(heavy))):
        for items in (light, heavy):
            if n < len(items):
                fn, cols = items[n]
                fn(cols)
    if_out[...] = (proj(_C_IF, slice(0, V7X_LANES)) + bif_ref[...]).T[0:GATE_ROWS, :]


def _mix_in(u, pos, win_r, qan, wq_r, kvn, wk_r, wv_r, bif, conv_w, conv_b, seq):
    t, d = u.shape
    tm = MIX_TM
    row = lambda i: (i, 0)
    n_in = win_r.shape[1]
    qk_w = MLA_HEADS * HEAD_PAD
    v_w = MLA_HEADS * MLA_V_DIM
    outs = [(qk_w, BF16), (qk_w, BF16), (v_w, BF16), (MLSTM_WIDTH, BF16), (MLSTM_WIDTH, BF16),
            (MLSTM_WIDTH, BF16), None, (N_BRANCHES * d, BF16)]
    out_specs = [pl.BlockSpec((tm, o[0]), row) if o else pl.BlockSpec((GATE_ROWS, tm), lambda i: (0, i))
                 for o in outs]
    out_shape = [jax.ShapeDtypeStruct((t, o[0]), o[1]) if o else jax.ShapeDtypeStruct((GATE_ROWS, t), F32)
                 for o in outs]
    return pl.pallas_call(
        functools.partial(_mixin_kernel, tm=tm, seq=seq),
        grid=(t // tm,),
        in_specs=[pl.BlockSpec((tm, d), row), pl.BlockSpec((tm, 1), row),
                  _resident((d, n_in)), _resident((1, MLA_Q_RANK)),
                  _resident((MLA_Q_RANK, qk_w)), _resident((1, MLA_KV_RANK)),
                  _resident((MLA_KV_RANK, qk_w)), _resident((MLA_KV_RANK, v_w)),
                  _resident((1, V7X_LANES)), _resident((CONV_WIDTH, MLSTM_WIDTH)),
                  _resident((1, MLSTM_WIDTH))],
        out_specs=out_specs,
        out_shape=out_shape,
        scratch_shapes=[pltpu.VMEM((tm, d), BF16), pltpu.VMEM((tm + CONV_HALO, MLSTM_WIDTH), F32)],
        compiler_params=_params(1),
        name="mix_in",
    )(u, pos, win_r, qan.reshape(1, -1), wq_r, kvn.reshape(1, -1), wk_r, wv_r, bif, conv_w,
      conv_b.reshape(1, -1))


def _attn_kernel(q_ref, k_ref, v_ref, o_ref, vaug_scr, s_scr, *, seq, tq):
    pair_v = 2 * MLA_V_DIM
    lane = lax.broadcasted_iota(jnp.int32, (1, pair_v), 1)
    vaug_scr[:, 0:pair_v] = v_ref[...]
    vaug_scr[:, pair_v:2 * pair_v] = jnp.where(
        lax.broadcasted_iota(jnp.int32, (seq, pair_v), 1) == 0, 1.0, 0.0).astype(BF16)
    r_i = lax.broadcasted_iota(jnp.int32, (tq, tq), 0)
    c_i = lax.broadcasted_iota(jnp.int32, (tq, tq), 1)
    causal = c_i <= r_i
    neg = jnp.finfo(F32).min
    def buffer(qi, j):
        return (qi % ATTN_DEPTH) * 2 + j

    def scores(qi):
        rows = slice(qi * tq, (qi + 1) * tq)
        maxima = []
        for j in range(2):
            grp = slice(j * HEAD_PAD, (j + 1) * HEAD_PAD)
            qh = q_ref[rows, grp]
            m_vec = None
            for cj in range(qi + 1):
                cols = slice(cj * tq, (cj + 1) * tq)
                s = lax.dot_general(qh, k_ref[cols, grp], (((1,), (1,)), ((), ())),
                                    preferred_element_type=F32)
                if cj == qi:
                    s = jnp.where(causal, s, neg)
                s_scr[buffer(qi, j), :, cols] = s
                for g in range(tq // V7X_LANES):
                    part = s[:, g * V7X_LANES:(g + 1) * V7X_LANES]
                    m_vec = part if m_vec is None else jnp.maximum(m_vec, part)
            maxima.append(jnp.max(m_vec, axis=-1, keepdims=True))
        return maxima

    def outputs(qi, maxima):
        outs = []
        for j in range(2):
            o_aug = None
            for cj in range(qi + 1):
                cols = slice(cj * tq, (cj + 1) * tq)
                p = jnp.exp2(s_scr[buffer(qi, j), :, cols] - maxima[j]).astype(BF16)
                part = _dot(p, vaug_scr[cols, :])
                o_aug = part if o_aug is None else o_aug + part
            outs.append(o_aug[:, 0:pair_v] * (1.0 / o_aug[:, pair_v:pair_v + 1]))
        o_ref[qi * tq:(qi + 1) * tq, :] = jnp.where(lane < MLA_V_DIM, outs[0], outs[1]).astype(BF16)

    order = list(reversed(range(seq // tq)))
    lead = ATTN_DEPTH - 1
    pending = [scores(qi) for qi in order[:lead]]
    for n, qi in enumerate(order):
        if n + lead < len(order):
            pending.append(scores(order[n + lead]))
        outputs(qi, pending.pop(0))


def _attention(q, k, v, bsz, seq):
    t = q.shape[0]
    pair_qk = 2 * HEAD_PAD
    pair_v = 2 * MLA_V_DIM
    blk = lambda b, g: (b, g)
    return pl.pallas_call(
        functools.partial(_attn_kernel, seq=seq, tq=ATTN_TQ),
        grid=(bsz, MLA_HEADS // 2),
        in_specs=[pl.BlockSpec((seq, pair_qk), blk), pl.BlockSpec((seq, pair_qk), blk),
                  pl.BlockSpec((seq, pair_v), blk)],
        out_specs=pl.BlockSpec((seq, pair_v), blk),
        out_shape=jax.ShapeDtypeStruct((t, MLA_HEADS * MLA_V_DIM), BF16),
        scratch_shapes=[pltpu.VMEM((seq, 2 * pair_v), BF16),
                        pltpu.VMEM((2 * ATTN_DEPTH, ATTN_TQ, seq), F32)],
        compiler_params=_params(2),
        name="mla_attn",
    )(q, k, v)


def _log_sigmoid(x):
    return jnp.minimum(x, 0.0) - jnp.log1p(jnp.exp(-jnp.abs(x)))


def _chunk_scan(x, lane_in_chunk, chunk, combine, fill):
    step = 1
    while step < chunk:
        x = combine(x, jnp.where(lane_in_chunk >= step, pltpu.roll(x, step, axis=1), fill))
        step *= 2
    return x


def _mlstm_kernel(xc_ref, vm_ref, og_ref, if_ref, wq_ref, wk_ref, wkt_ref, hn_ref, o_ref,
                  q_scr, k_scr, kt_scr, vaug_scr, ct_scr, b_scr, src_scr, mi_scr, *, seq, chunk):
    dh = MLSTM_HEAD_DIM
    nh = MLSTM_HEADS
    nc = seq // chunk
    ones_col = jnp.where(lax.broadcasted_iota(jnp.int32, (seq, dh), 1) == 0, 1.0, 0.0).astype(BF16)
    for h in range(nh):
        hs = slice(h * dh, (h + 1) * dh)
        xc = xc_ref[:, hs]
        q_scr[h] = _dot(xc, wq_ref[h]).astype(BF16)
        k_scr[h] = (_dot(xc, wk_ref[h]) * (dh ** -0.5)).astype(BF16)
        kt = lax.dot_general(wkt_ref[h], xc, (((1,), (1,)), ((), ())),
                             preferred_element_type=F32) * (dh ** -0.5)
        for c in range(nc):
            kt_scr[h * nc + c] = kt[:, c * chunk:(c + 1) * chunk]
        vaug_scr[h, :, 0:dh] = vm_ref[:, hs]
        vaug_scr[h, :, dh:2 * dh] = ones_col
        ct_scr[h] = jnp.zeros((dh, 2 * dh), F32)

    gates_t = if_ref[...]
    lane_in_chunk = lax.broadcasted_iota(jnp.int32, (GATE_ROWS, seq), 1) % chunk
    b_all = _chunk_scan(_log_sigmoid(gates_t) * LOG2_E, lane_in_chunk, chunk, jnp.add, 0.0)
    b_all = pltpu.roll(b_all, nh, axis=0)
    src_all = gates_t * LOG2_E - b_all
    mi_all = b_all + _chunk_scan(src_all, lane_in_chunk, chunk, jnp.maximum, -jnp.inf)
    for c in range(nc):
        cs = slice(c * chunk, (c + 1) * chunk)
        b_scr[c] = b_all[:, cs]
        src_scr[c] = src_all[:, cs]
        mi_scr[c] = mi_all[:, cs]

    r_i = lax.broadcasted_iota(jnp.int32, (chunk, chunk), 0)
    c_i = lax.broadcasted_iota(jnp.int32, (chunk, chunk), 1)
    eye = r_i == c_i
    tril = c_i <= r_i
    last_lane = lax.broadcasted_iota(jnp.int32, (GATE_ROWS, chunk), 1) == chunk - 1

    def to_col(row):
        return jnp.sum(jnp.where(eye, row, 0.0), axis=1, keepdims=True)

    def chunk_step(c, m_prev):
        r0 = pl.multiple_of(c * chunk, chunk)
        b_c = b_scr[c]
        src_c = src_scr[c]
        g = jnp.sum(jnp.where(last_lane, b_c, 0.0), axis=1, keepdims=True)
        m_new = jnp.maximum(g + m_prev, jnp.max(g + src_c, axis=1, keepdims=True))
        decay = jnp.exp2(g + m_prev - m_new)
        e_rows = jnp.exp2(g + src_c - m_new)
        m_t = jnp.maximum(b_c + m_prev, mi_scr[c])
        u_rows = b_c - m_t
        nm_rows = jnp.exp2(-m_t)
        for h in range(nh):
            hs = slice(h * dh, (h + 1) * dh)
            row = slice(h, h + 1)
            u_col = to_col(u_rows[row])
            q_c = q_scr[h, pl.ds(r0, chunk), :]
            v_c = vaug_scr[h, pl.ds(r0, chunk), :]
            s_qk = lax.dot_general(q_c, k_scr[h, pl.ds(r0, chunk), :], (((1,), (1,)), ((), ())),
                                   preferred_element_type=F32)
            d_mat = jnp.exp2(jnp.where(tril, u_col + src_c[row], -jnp.inf))
            q_in = (q_c.astype(F32) * jnp.exp2(u_col + m_prev[row])).astype(BF16)
            ct = ct_scr[h]
            tot = _dot(q_in, ct.astype(BF16)) + _dot((d_mat * s_qk).astype(BF16), v_c)
            den = jnp.maximum(jnp.abs(tot[:, dh:dh + 1]), to_col(nm_rows[row]))
            hh = tot[:, 0:dh] * (1.0 / den) * og_ref[pl.ds(r0, chunk), hs].astype(F32)
            o_ref[pl.ds(r0, chunk), hs] = _rms(hh, hn_ref[:, hs]).astype(BF16)
            c_loc = _dot((kt_scr[h * nc + c] * e_rows[row]).astype(BF16), v_c)
            ct_scr[h] = decay[row] * ct + c_loc
        return m_new

    lax.fori_loop(0, nc, chunk_step, jnp.zeros((GATE_ROWS, 1), F32), unroll=2)


def _mlstm(xc, vm, og, ifg, wq, wk, head_norm, bsz, seq):
    t = xc.shape[0]
    dh = MLSTM_HEAD_DIM
    nh = MLSTM_HEADS
    nc = seq // MLSTM_CHUNK
    blk = lambda b: (b, 0)
    return pl.pallas_call(
        functools.partial(_mlstm_kernel, seq=seq, chunk=MLSTM_CHUNK),
        grid=(bsz,),
        in_specs=[pl.BlockSpec((seq, MLSTM_WIDTH), blk), pl.BlockSpec((seq, MLSTM_WIDTH), blk),
                  pl.BlockSpec((seq, MLSTM_WIDTH), blk),
                  pl.BlockSpec((GATE_ROWS, seq), lambda b: (0, b)),
                  _resident((nh, dh, dh)), _resident((nh, dh, dh)), _resident((nh, dh, dh)),
                  _resident((1, MLSTM_WIDTH))],
        out_specs=pl.BlockSpec((seq, MLSTM_WIDTH), blk),
        out_shape=jax.ShapeDtypeStruct((t, MLSTM_WIDTH), BF16),
        scratch_shapes=[pltpu.VMEM((nh, seq, dh), BF16),
                        pltpu.VMEM((nh, seq, dh), BF16),
                        pltpu.VMEM((nh * nc, dh, MLSTM_CHUNK), F32),
                        pltpu.VMEM((nh, seq, 2 * dh), BF16),
                        pltpu.VMEM((nh, dh, 2 * dh), F32),
                        pltpu.VMEM((nc, GATE_ROWS, MLSTM_CHUNK), F32),
                        pltpu.VMEM((nc, GATE_ROWS, MLSTM_CHUNK), F32),
                        pltpu.VMEM((nc, GATE_ROWS, MLSTM_CHUNK), F32)],
        compiler_params=_params(1),
        name="mlstm",
    )(xc, vm, og, ifg, wq, wk, jnp.swapaxes(wk, 1, 2), head_norm.reshape(1, -1))


def _pad_cols(w, lo, width):
    return jnp.pad(w, ((0, 0), (lo, width - lo - w.shape[1])))


def _w_in_moves(d_in):
    src_kpe = MLA_Q_RANK + MLA_KV_RANK
    src_xm = src_kpe + MLA_ROPE_DIM
    src_if = src_xm + 3 * MLSTM_WIDTH
    src_gate = src_if + 2 * MLSTM_HEADS
    return ((0, _C_QLAT, src_kpe),
            (src_kpe, _C_KPE + ROPE_LO, MLA_ROPE_DIM),
            (src_xm, _C_XM, 3 * MLSTM_WIDTH),
            (src_if, _C_IF, 2 * MLSTM_HEADS),
            (src_gate, _C_GATE, d_in - src_gate))


def _layout_w_in_kernel(w_ref, o_ref):
    o_ref[...] = jnp.zeros(o_ref.shape, BF16)
    for src, dst, width in _w_in_moves(w_ref.shape[1]):
        o_ref[:, dst:dst + width] = w_ref[:, src:src + width].astype(BF16)


def _layout_w_in(w_in, layer):
    _, rows, d_in = w_in.shape
    d_out = _C_GATE + (d_in - _w_in_moves(d_in)[-1][0])
    tr = rows // 4
    return pl.pallas_call(
        _layout_w_in_kernel,
        grid=(rows // tr,),
        in_specs=[pl.BlockSpec((None, tr, d_in), lambda i: (layer, i, 0))],
        out_specs=pl.BlockSpec((tr, d_out), lambda i: (i, 0)),
        out_shape=jax.ShapeDtypeStruct((rows, d_out), BF16),
        compiler_params=_params(1),
        name="layout_w_in",
    )(w_in)


def _layout_w_q_b(w_q_b):
    w = w_q_b.reshape(MLA_Q_RANK, MLA_HEADS, MLA_QK_DIM)
    w = jnp.pad(w, ((0, 0), (0, 0), (0, HEAD_PAD - MLA_QK_DIM)))
    return w.reshape(MLA_Q_RANK, MLA_HEADS * HEAD_PAD).astype(BF16)


def _layout_w_kv_b(w_kv_b):
    w = w_kv_b.reshape(MLA_KV_RANK, MLA_HEADS, MLA_NOPE_DIM + MLA_V_DIM)
    wk = jnp.pad(w[:, :, :MLA_NOPE_DIM], ((0, 0), (0, 0), (0, HEAD_PAD - MLA_NOPE_DIM)))
    wv = w[:, :, MLA_NOPE_DIM:]
    return (wk.reshape(MLA_KV_RANK, MLA_HEADS * HEAD_PAD).astype(BF16),
            wv.reshape(MLA_KV_RANK, MLA_HEADS * MLA_V_DIM).astype(BF16))


def kernel(x, c, positions, w_ada, b_ada, norm_ff1, ff1_w_gate, ff1_w_up, ff1_w_down, norm_mix, w_in,
           q_a_norm, w_q_b, kv_a_norm, w_kv_b, conv_w, conv_b, w_q_m, w_k_m, b_i, b_f, mlstm_norm,
           w_mla_out, w_mlstm_out, w_o, norm_ff2, ff2_w_gate, ff2_w_up, ff2_w_down, norm_final):
    bsz, seq, d = x.shape
    t = bsz * seq
    depth = w_ada.shape[0]
    h = x.reshape(t, d)
    pos = positions.reshape(t, 1)
    for l in range(depth):
        mod3 = _adaln(c, w_ada[l], b_ada[l]).reshape(bsz, 1, N_MOD * d)
        h, u_mix = _ffn(h, mod3, 0, norm_ff1[l], ff1_w_gate[l].astype(BF16), ff1_w_up[l].astype(BF16),
                        ff1_w_down[l].astype(BF16), seq, next_norm_g=norm_mix[l], next_chunk0=3)
        wk_r, wv_r = _layout_w_kv_b(w_kv_b[l])
        bif = _pad_cols(jnp.concatenate([b_i[l], b_f[l]]).reshape(1, -1), 0, V7X_LANES)
        q, k, v, xc, vm, og, ifg, gates = _mix_in(
            u_mix, pos, _layout_w_in(w_in, l), q_a_norm[l], _layout_w_q_b(w_q_b[l]),
            kv_a_norm[l], wk_r, wv_r, bif, conv_w[l], conv_b[l], seq)
        o_a = _attention(q, k, v, bsz, seq)
        o_b = _mlstm(xc, vm, og, ifg, w_q_m[l].astype(BF16), w_k_m[l].astype(BF16), mlstm_norm[l],
                     bsz, seq)
        mixer = (o_a, o_b, gates, 5, w_mla_out[l].astype(BF16), w_mlstm_out[l].astype(BF16),
                 w_o[l].astype(BF16))
        final_g = norm_final if l == depth - 1 else None
        h = _ffn(h, mod3, 6, norm_ff2[l], ff2_w_gate[l].astype(BF16), ff2_w_up[l].astype(BF16),
                 ff2_w_down[l].astype(BF16), seq, mixer=mixer, final_g=final_g)[0]
    return h.reshape(bsz, seq, d)
```

```python
import functools

import jax
import jax.numpy as jnp
from jax import lax
from jax.experimental import pallas as pl
from jax.experimental.pallas import tpu as pltpu

F32 = jnp.float32
BF16 = jnp.bfloat16

V7X_LANES = 128
MXU_COLS = 256
V7X_VMEM_BYTES = 64 * 1024 * 1024
VMEM_LIMIT_BYTES = V7X_VMEM_BYTES - 8 * 1024 * 1024

MLA_HEADS = 8
MLA_NOPE_DIM = 64
MLA_ROPE_DIM = 32
MLA_V_DIM = 64
MLA_QK_DIM = MLA_NOPE_DIM + MLA_ROPE_DIM
MLA_Q_RANK = 384
MLA_KV_RANK = 256
ROPE_THETA = 10000.0
MLSTM_HEADS = 4
MLSTM_HEAD_DIM = 128
MLSTM_WIDTH = MLSTM_HEADS * MLSTM_HEAD_DIM
CONV_WIDTH = 4
GATE_ROWS = 2 * MLSTM_HEADS
CONV_HALO = 8
N_BRANCHES = 2
N_MOD = 9
EPS = 1e-6
LOG2_E = 1.4426950408889634

HEAD_PAD = V7X_LANES
ROPE_LO = MLA_NOPE_DIM
ROPE_HALF = MLA_ROPE_DIM // 2

FFN_TM = 1024
FFN_TM_MIXER = 512
FFN_TF = 256
MIX_TM = 1024
MIX_LEAD_CHUNKS = 3
ATTN_TQ = 256
ATTN_DEPTH = 4
MLSTM_CHUNK = 128


def _dot(a, b):
    return jnp.dot(a, b, preferred_element_type=F32)


def _rms(x, gain):
    return x * lax.rsqrt(jnp.mean(x * x, axis=-1, keepdims=True) + EPS) * gain


def _params(n_axes, flags=None):
    return pltpu.CompilerParams(
        dimension_semantics=("arbitrary",) * n_axes, vmem_limit_bytes=VMEM_LIMIT_BYTES, flags=flags)


def _resident(shape):
    zeros = (0,) * len(shape)
    return pl.BlockSpec(shape, lambda *_: zeros, pipeline_mode=pl.Buffered(1))


def _mod_spec(chunk, tm, seq, d):
    return pl.BlockSpec((None, 1, d), lambda i: ((i * tm) // seq, 0, chunk))


def _adaln_kernel(c_ref, w_ref, b_ref, o_ref):
    c = c_ref[...]
    sc = (c * jax.nn.sigmoid(c)).astype(BF16)
    o_ref[...] = _dot(sc, w_ref[...].astype(BF16)) + b_ref[...]


def _adaln(c, w_ada, b_ada):
    bsz, d = c.shape
    n = w_ada.shape[1]
    tn = d
    return pl.pallas_call(
        _adaln_kernel,
        grid=(n // tn,),
        in_specs=[pl.BlockSpec((bsz, d), lambda j: (0, 0)),
                  pl.BlockSpec((d, tn), lambda j: (0, j)),
                  pl.BlockSpec((1, tn), lambda j: (0, j))],
        out_specs=pl.BlockSpec((bsz, tn), lambda j: (0, j)),
        out_shape=jax.ShapeDtypeStruct((bsz, n), F32),
        compiler_params=_params(1),
        name="adaln",
    )(c, w_ada, b_ada.reshape(1, n))


def _modulated_norm(x, gain, shift, scale):
    return _rms(x, gain) * (1.0 + scale) + shift


def _ffn_kernel(*refs, d_ff, tf, mixer_prologue, epilogue):
    refs = list(refs)
    if mixer_prologue:
        h_ref, oa_ref, ob_ref, bg_ref, gtm_ref, wa_ref, wb_ref, wo_ref = refs[:8]
        refs = refs[8:]
    else:
        x_ref = refs.pop(0)
    g_ref, sh_ref, sc_ref, gt_ref, wg_ref, wu_ref, wd_ref = refs[:7]
    refs = refs[7:]
    if epilogue == "final_norm":
        gf_ref = refs.pop(0)
    elif epilogue == "next_mod":
        gn_ref, shn_ref, scn_ref = refs[:3]
        refs = refs[3:]
    o_ref = refs.pop(0)
    if epilogue == "next_mod":
        un_ref = refs.pop(0)
    u_scr, a_scr = refs[:2]

    if mixer_prologue:
        d = h_ref.shape[1]
        y_a = _dot(oa_ref[...], wa_ref[...])
        y_b = _dot(ob_ref[...], wb_ref[...])
        y = (jax.nn.sigmoid(bg_ref[:, 0:d].astype(F32)) * y_a
             + jax.nn.sigmoid(bg_ref[:, d:2 * d].astype(F32)) * y_b)
        x_ref = refs[2]
        x_ref[...] = h_ref[...] + gtm_ref[...] * _dot(y.astype(BF16), wo_ref[...])

    u_scr[...] = _modulated_norm(x_ref[...], g_ref[...], sh_ref[...], sc_ref[...]).astype(BF16)
    for j in range(d_ff // tf):
        cols = slice(j * tf, (j + 1) * tf)
        g = _dot(u_scr[...], wg_ref[:, cols])
        up = _dot(u_scr[...], wu_ref[:, cols])
        a_scr[:, cols] = (g * jax.nn.sigmoid(g) * up).astype(BF16)
    down = _dot(a_scr[...], wd_ref[...])
    out = x_ref[...] + (0.5 * gt_ref[...]) * down
    if epilogue == "final_norm":
        out = _rms(out, gf_ref[...])
    o_ref[...] = out
    if epilogue == "next_mod":
        un_ref[...] = _modulated_norm(out, gn_ref[...], shn_ref[...], scn_ref[...]).astype(BF16)


def _ffn(h, mod3, chunk0, norm_g, wg, wu, wd, seq, *, mixer=None, final_g=None, next_norm_g=None,
         next_chunk0=None):
    t, d = h.shape
    d_ff = wg.shape[1]
    tm = FFN_TM_MIXER if mixer is not None else FFN_TM
    row = lambda i: (i, 0)
    epilogue = "final_norm" if final_g is not None else "next_mod" if next_norm_g is not None else "plain"
    in_specs = [pl.BlockSpec((tm, d), row)]
    args = [h]
    scratch = [pltpu.VMEM((tm, d), BF16), pltpu.VMEM((tm, d_ff), BF16)]
    if mixer is not None:
        o_a, o_b, branch_gates, gate_chunk, w_a, w_b, w_o = mixer
        in_specs += [pl.BlockSpec((tm, o_a.shape[1]), row), pl.BlockSpec((tm, o_b.shape[1]), row),
                     pl.BlockSpec((tm, branch_gates.shape[1]), row),
                     _mod_spec(gate_chunk, tm, seq, d),
                     _resident(w_a.shape), _resident(w_b.shape), _resident(w_o.shape)]
        args += [o_a, o_b, branch_gates, mod3, w_a, w_b, w_o]
        scratch.append(pltpu.VMEM((tm, d), F32))
    in_specs += [_resident((1, d)), _mod_spec(chunk0, tm, seq, d), _mod_spec(chunk0 + 1, tm, seq, d),
                 _mod_spec(chunk0 + 2, tm, seq, d),
                 _resident((d, d_ff)), _resident((d, d_ff)), _resident((d_ff, d))]
    args += [norm_g.reshape(1, d), mod3, mod3, mod3, wg, wu, wd]
    out_specs = [pl.BlockSpec((tm, d), row)]
    out_shape = [jax.ShapeDtypeStruct((t, d), F32)]
    if epilogue == "final_norm":
        in_specs.append(_resident((1, d)))
        args.append(final_g.reshape(1, d))
    elif epilogue == "next_mod":
        in_specs += [_resident((1, d)), _mod_spec(next_chunk0, tm, seq, d),
                     _mod_spec(next_chunk0 + 1, tm, seq, d)]
        args += [next_norm_g.reshape(1, d), mod3, mod3]
        out_specs.append(pl.BlockSpec((tm, d), row))
        out_shape.append(jax.ShapeDtypeStruct((t, d), BF16))
    return pl.pallas_call(
        functools.partial(_ffn_kernel, d_ff=d_ff, tf=FFN_TF, mixer_prologue=mixer is not None,
                          epilogue=epilogue),
        grid=(t // tm,),
        in_specs=in_specs,
        out_specs=out_specs,
        out_shape=out_shape,
        scratch_shapes=scratch,
        compiler_params=_params(1),
        name=("mixout_ffn_" if mixer is not None else "ffn_") + epilogue,
    )(*args)


_C_QLAT = 0
_C_CKV = _C_QLAT + MLA_Q_RANK
_C_KPE = _C_CKV + MLA_KV_RANK
_C_XM = _C_KPE + HEAD_PAD
_C_VM = _C_XM + MLSTM_WIDTH
_C_OM = _C_VM + MLSTM_WIDTH
_C_IF = _C_OM + MLSTM_WIDTH
_C_GATE = _C_IF + V7X_LANES


ROPE_BLOCKS = HEAD_PAD // MLA_ROPE_DIM


def _pack_positions(positions, tm):
    rows = tm // ROPE_BLOCKS
    p = positions.reshape(-1, ROPE_BLOCKS, rows).transpose(0, 2, 1)
    return jnp.repeat(p, MLA_ROPE_DIM, axis=2).reshape(-1, HEAD_PAD)


def _rope_tables(pos_ref):
    n_blk = ROPE_BLOCKS
    lane = lax.broadcasted_iota(jnp.int32, (1, HEAD_PAD), 1)
    idx = (lane % ROPE_HALF).astype(F32)
    theta = jnp.full((1, HEAD_PAD), ROPE_THETA, F32)
    inv_freq = jnp.exp(-(idx / ROPE_HALF) * jnp.log(theta))
    ang = pos_ref[...].astype(F32) * inv_freq
    cos_p = jnp.cos(ang)
    sin_p = jnp.sin(ang)

    in_lo = (lane >= ROPE_LO) & (lane < ROPE_LO + ROPE_HALF)
    in_hi = (lane >= ROPE_LO + ROPE_HALF) & (lane < ROPE_LO + MLA_ROPE_DIM)
    cos_m, sin_up, sin_dn = [], [], []
    for b in range(n_blk):
        shift = (ROPE_LO - b * MLA_ROPE_DIM) % HEAD_PAD
        cos = cos_p if shift == 0 else pltpu.roll(cos_p, shift, axis=1)
        sin = sin_p if shift == 0 else pltpu.roll(sin_p, shift, axis=1)
        cos_m.append(jnp.where(lane < ROPE_LO, 1.0, jnp.where(in_lo | in_hi, cos, 0.0)))
        sin_up.append(jnp.where(in_lo, -sin, 0.0))
        sin_dn.append(jnp.where(in_hi, sin, 0.0))
    return tuple(jnp.concatenate(t, axis=0) for t in (cos_m, sin_up, sin_dn))


def _rope_group(x, tables):
    cos_m, sin_up, sin_dn = tables
    up = pltpu.roll(x, HEAD_PAD - ROPE_HALF, axis=1)
    dn = pltpu.roll(x, ROPE_HALF, axis=1)
    return x * cos_m + up * sin_up + dn * sin_dn


def _mixin_kernel(u_ref, pos_ref, win_ref, qan_ref, wq_ref, kvn_ref,
                  wk_ref, wv_ref, bif_ref, cw_ref, cb_ref,
                  q_out, k_out, v_out, xc_out, vm_out, og_out, if_out, gate_out, u_scr, xext_scr,
                  *, tm, seq):
    def col_chunks(width):
        return [slice(lo, min(lo + MXU_COLS, width)) for lo in range(0, width, MXU_COLS)]

    u_scr[...] = u_ref[...]

    def proj(lo, cols):
        return _dot(u_scr[...], win_ref[:, lo + cols.start:lo + cols.stop])

    def gate_chunk(cols):
        gate_out[:, cols] = proj(_C_GATE, cols).astype(BF16)

    gate_cols = col_chunks(gate_out.shape[1])
    q_lat = jnp.concatenate([proj(_C_QLAT, cols) for cols in col_chunks(MLA_Q_RANK)], axis=1)
    c_kv = proj(_C_CKV, slice(0, MLA_KV_RANK))
    k_pe_raw = proj(_C_KPE, slice(0, HEAD_PAD))
    for cols in gate_cols[:MIX_LEAD_CHUNKS]:
        gate_chunk(cols)
    tables = _rope_tables(pos_ref)
    q_n = _rms(q_lat, qan_ref[...]).astype(BF16)
    c_n = _rms(c_kv, kvn_ref[...]).astype(BF16)
    k_pe = _rope_group(k_pe_raw, tables)

    q_scale = MLA_QK_DIM ** -0.5 * LOG2_E

    def q_chunk(cols):
        q = _dot(q_n, wq_ref[:, cols])
        for g in range(MXU_COLS // HEAD_PAD):
            grp = slice(g * HEAD_PAD, (g + 1) * HEAD_PAD)
            out = slice(cols.start + grp.start, cols.start + grp.stop)
            q_out[:, out] = (_rope_group(q[:, grp], tables) * q_scale).astype(BF16)

    def k_chunk(cols):
        k_nope = _dot(c_n, wk_ref[:, cols])
        for g in range(MXU_COLS // HEAD_PAD):
            grp = slice(g * HEAD_PAD, (g + 1) * HEAD_PAD)
            out = slice(cols.start + grp.start, cols.start + grp.stop)
            k_out[:, out] = (k_nope[:, grp] + k_pe).astype(BF16)

    def v_chunk(cols):
        v_out[:, cols] = _dot(c_n, wv_ref[:, cols]).astype(BF16)

    @pl.when((pl.program_id(0) * tm) % seq == 0)
    def _():
        xext_scr[0:CONV_HALO, :] = jnp.zeros((CONV_HALO, MLSTM_WIDTH), F32)

    def conv_chunk(cols):
        xext_scr[CONV_HALO:CONV_HALO + tm, cols] = proj(_C_XM, cols)
        conv = cb_ref[:, cols] + (xext_scr[CONV_HALO:CONV_HALO + tm, cols]
                                  * cw_ref[CONV_WIDTH - 1:CONV_WIDTH, cols])
        for j in range(1, CONV_WIDTH):
            conv = conv + (xext_scr[CONV_HALO - j:CONV_HALO - j + tm, cols]
                           * cw_ref[CONV_WIDTH - 1 - j:CONV_WIDTH - j, cols])
        xext_scr[0:CONV_HALO, cols] = xext_scr[tm:tm + CONV_HALO, cols]
        xc_out[:, cols] = (conv * jax.nn.sigmoid(conv)).astype(BF16)

    def vm_chunk(cols):
        vm_out[:, cols] = proj(_C_VM, cols).astype(BF16)

    def og_chunk(cols):
        og_out[:, cols] = jax.nn.sigmoid(proj(_C_OM, cols)).astype(BF16)

    light = ([(gate_chunk, c) for c in gate_cols[MIX_LEAD_CHUNKS:]]
             + [(vm_chunk, c) for c in col_chunks(MLSTM_WIDTH)]
             + [(v_chunk, c) for c in col_chunks(MLA_HEADS * MLA_V_DIM)])
    heavy = ([(q_chunk, c) for c in col_chunks(MLA_HEADS * HEAD_PAD)]
             + [(conv_chunk, c) for c in col_chunks(MLSTM_WIDTH)]
             + [(og_chunk, c) for c in col_chunks(MLSTM_WIDTH)]
             + [(k_chunk, c) for c in col_chunks(MLA_HEADS * HEAD_PAD)])
    for n in range(max(len(light), len(heavy))):
        for items in (light, heavy):
            if n < len(items):
                fn, cols = items[n]
                fn(cols)
    if_out[...] = (proj(_C_IF, slice(0, V7X_LANES)) + bif_ref[...]).T[0:GATE_ROWS, :]


def _mix_in(u, pos, win_r, qan, wq_r, kvn, wk_r, wv_r, bif, conv_w, conv_b, seq):
    t, d = u.shape
    tm = MIX_TM
    row = lambda i: (i, 0)
    n_in = win_r.shape[1]
    qk_w = MLA_HEADS * HEAD_PAD
    v_w = MLA_HEADS * MLA_V_DIM
    outs = [(qk_w, BF16), (qk_w, BF16), (v_w, BF16), (MLSTM_WIDTH, BF16), (MLSTM_WIDTH, BF16),
            (MLSTM_WIDTH, BF16), None, (N_BRANCHES * d, BF16)]
    out_specs = [pl.BlockSpec((tm, o[0]), row) if o else pl.BlockSpec((GATE_ROWS, tm), lambda i: (0, i))
                 for o in outs]
    out_shape = [jax.ShapeDtypeStruct((t, o[0]), o[1]) if o else jax.ShapeDtypeStruct((GATE_ROWS, t), F32)
                 for o in outs]
    return pl.pallas_call(
        functools.partial(_mixin_kernel, tm=tm, seq=seq),
        grid=(t // tm,),
        in_specs=[pl.BlockSpec((tm, d), row), pl.BlockSpec((tm // ROPE_BLOCKS, HEAD_PAD), row),
                  _resident((d, n_in)), _resident((1, MLA_Q_RANK)),
                  _resident((MLA_Q_RANK, qk_w)), _resident((1, MLA_KV_RANK)),
                  _resident((MLA_KV_RANK, qk_w)), _resident((MLA_KV_RANK, v_w)),
                  _resident((1, V7X_LANES)), _resident((CONV_WIDTH, MLSTM_WIDTH)),
                  _resident((1, MLSTM_WIDTH))],
        out_specs=out_specs,
        out_shape=out_shape,
        scratch_shapes=[pltpu.VMEM((tm, d), BF16), pltpu.VMEM((tm + CONV_HALO, MLSTM_WIDTH), F32)],
        compiler_params=_params(1),
        name="mix_in",
    )(u, pos, win_r, qan.reshape(1, -1), wq_r, kvn.reshape(1, -1), wk_r, wv_r, bif, conv_w,
      conv_b.reshape(1, -1))


def _attn_kernel(q_ref, k_ref, v_ref, o_ref, vaug_scr, s_scr, *, seq, tq):
    pair_v = 2 * MLA_V_DIM
    lane = lax.broadcasted_iota(jnp.int32, (1, pair_v), 1)
    vaug_scr[:, 0:pair_v] = v_ref[...]
    vaug_scr[:, pair_v:2 * pair_v] = jnp.where(
        lax.broadcasted_iota(jnp.int32, (seq, pair_v), 1) == 0, 1.0, 0.0).astype(BF16)
    r_i = lax.broadcasted_iota(jnp.int32, (tq, tq), 0)
    c_i = lax.broadcasted_iota(jnp.int32, (tq, tq), 1)
    causal = c_i <= r_i
    neg = jnp.finfo(F32).min
    order = list(reversed(range(seq // tq)))

    def buffer(qi, j):
        return (order.index(qi) % ATTN_DEPTH) * 2 + j

    def scores(qi):
        rows = slice(qi * tq, (qi + 1) * tq)
        maxima = []
        for j in range(2):
            grp = slice(j * HEAD_PAD, (j + 1) * HEAD_PAD)
            qh = q_ref[rows, grp]
            m_vec = None
            for cj in range(qi + 1):
                cols = slice(cj * tq, (cj + 1) * tq)
                s = lax.dot_general(qh, k_ref[cols, grp], (((1,), (1,)), ((), ())),
                                    preferred_element_type=F32)
                if cj == qi:
                    s = jnp.where(causal, s, neg)
                s_scr[buffer(qi, j), :, cols] = s
                for g in range(tq // V7X_LANES):
                    part = s[:, g * V7X_LANES:(g + 1) * V7X_LANES]
                    m_vec = part if m_vec is None else jnp.maximum(m_vec, part)
            maxima.append(jnp.max(m_vec, axis=-1, keepdims=True))
        return maxima

    def outputs(qi, maxima):
        outs = []
        for j in range(2):
            o_aug = None
            for cj in range(qi + 1):
                cols = slice(cj * tq, (cj + 1) * tq)
                p = jnp.exp2(s_scr[buffer(qi, j), :, cols] - maxima[j]).astype(BF16)
                part = _dot(p, vaug_scr[cols, :])
                o_aug = part if o_aug is None else o_aug + part
            outs.append(o_aug[:, 0:pair_v] * (1.0 / o_aug[:, pair_v:pair_v + 1]))
        o_ref[qi * tq:(qi + 1) * tq, :] = jnp.where(lane < MLA_V_DIM, outs[0], outs[1]).astype(BF16)

    lead = ATTN_DEPTH - 1
    pending = [scores(qi) for qi in order[:lead]]
    for n, qi in enumerate(order):
        if n + lead < len(order):
            pending.append(scores(order[n + lead]))
        outputs(qi, pending.pop(0))


def _attention(q, k, v, bsz, seq):
    t = q.shape[0]
    pair_qk = 2 * HEAD_PAD
    pair_v = 2 * MLA_V_DIM
    blk = lambda b, g: (b, g)
    return pl.pallas_call(
        functools.partial(_attn_kernel, seq=seq, tq=ATTN_TQ),
        grid=(bsz, MLA_HEADS // 2),
        in_specs=[pl.BlockSpec((seq, pair_qk), blk), pl.BlockSpec((seq, pair_qk), blk),
                  pl.BlockSpec((seq, pair_v), blk)],
        out_specs=pl.BlockSpec((seq, pair_v), blk),
        out_shape=jax.ShapeDtypeStruct((t, MLA_HEADS * MLA_V_DIM), BF16),
        scratch_shapes=[pltpu.VMEM((seq, 2 * pair_v), BF16),
                        pltpu.VMEM((2 * ATTN_DEPTH, ATTN_TQ, seq), F32)],
        compiler_params=_params(2),
        name="mla_attn",
    )(q, k, v)


def _log_sigmoid(x):
    return jnp.minimum(x, 0.0) - jnp.log1p(jnp.exp(-jnp.abs(x)))


def _chunk_scan(x, lane_in_chunk, chunk, combine, fill):
    step = 1
    while step < chunk:
        x = combine(x, jnp.where(lane_in_chunk >= step, pltpu.roll(x, step, axis=1), fill))
        step *= 2
    return x


def _mlstm_kernel(xc_ref, vm_ref, og_ref, if_ref, wq_ref, wk_ref, wkt_ref, hn_ref, o_ref,
                  q_scr, k_scr, kt_scr, vaug_scr, ct_scr, b_scr, src_scr, mi_scr, *, seq, chunk):
    dh = MLSTM_HEAD_DIM
    nh = MLSTM_HEADS
    nc = seq // chunk
    ones_col = jnp.where(lax.broadcasted_iota(jnp.int32, (seq, dh), 1) == 0, 1.0, 0.0).astype(BF16)
    for h in range(nh):
        hs = slice(h * dh, (h + 1) * dh)
        xc = xc_ref[:, hs]
        q_scr[h] = _dot(xc, wq_ref[h]).astype(BF16)
        k_scr[h] = (_dot(xc, wk_ref[h]) * (dh ** -0.5)).astype(BF16)
        kt = lax.dot_general(wkt_ref[h], xc, (((1,), (1,)), ((), ())),
                             preferred_element_type=F32) * (dh ** -0.5)
        for c in range(nc):
            kt_scr[h * nc + c] = kt[:, c * chunk:(c + 1) * chunk]
        vaug_scr[h, :, 0:dh] = vm_ref[:, hs]
        vaug_scr[h, :, dh:2 * dh] = ones_col
        ct_scr[h] = jnp.zeros((dh, 2 * dh), F32)

    gates_t = if_ref[...]
    lane_in_chunk = lax.broadcasted_iota(jnp.int32, (GATE_ROWS, seq), 1) % chunk
    b_all = _chunk_scan(_log_sigmoid(gates_t) * LOG2_E, lane_in_chunk, chunk, jnp.add, 0.0)
    b_all = pltpu.roll(b_all, nh, axis=0)
    src_all = gates_t * LOG2_E - b_all
    mi_all = b_all + _chunk_scan(src_all, lane_in_chunk, chunk, jnp.maximum, -jnp.inf)
    for c in range(nc):
        cs = slice(c * chunk, (c + 1) * chunk)
        b_scr[c] = b_all[:, cs]
        src_scr[c] = src_all[:, cs]
        mi_scr[c] = mi_all[:, cs]

    r_i = lax.broadcasted_iota(jnp.int32, (chunk, chunk), 0)
    c_i = lax.broadcasted_iota(jnp.int32, (chunk, chunk), 1)
    eye = r_i == c_i
    tril = c_i <= r_i
    last_lane = lax.broadcasted_iota(jnp.int32, (GATE_ROWS, chunk), 1) == chunk - 1

    def to_col(row):
        return jnp.sum(jnp.where(eye, row, 0.0), axis=1, keepdims=True)

    def chunk_step(c, m_prev):
        r0 = pl.multiple_of(c * chunk, chunk)
        b_c = b_scr[c]
        src_c = src_scr[c]
        g = jnp.sum(jnp.where(last_lane, b_c, 0.0), axis=1, keepdims=True)
        m_new = jnp.maximum(g + m_prev, jnp.max(g + src_c, axis=1, keepdims=True))
        decay = jnp.exp2(g + m_prev - m_new)
        e_rows = jnp.exp2(g + src_c - m_new)
        m_t = jnp.maximum(b_c + m_prev, mi_scr[c])
        u_rows = b_c - m_t
        nm_rows = jnp.exp2(-m_t)
        for h in range(nh):
            hs = slice(h * dh, (h + 1) * dh)
            row = slice(h, h + 1)
            u_col = to_col(u_rows[row])
            q_c = q_scr[h, pl.ds(r0, chunk), :]
            v_c = vaug_scr[h, pl.ds(r0, chunk), :]
            s_qk = lax.dot_general(q_c, k_scr[h, pl.ds(r0, chunk), :], (((1,), (1,)), ((), ())),
                                   preferred_element_type=F32)
            d_mat = jnp.exp2(jnp.where(tril, u_col + src_c[row], -jnp.inf))
            q_in = (q_c.astype(F32) * jnp.exp2(u_col + m_prev[row])).astype(BF16)
            ct = ct_scr[h]
            tot = _dot(q_in, ct.astype(BF16)) + _dot((d_mat * s_qk).astype(BF16), v_c)
            den = jnp.maximum(jnp.abs(tot[:, dh:dh + 1]), to_col(nm_rows[row]))
            hh = tot[:, 0:dh] * (1.0 / den) * og_ref[pl.ds(r0, chunk), hs].astype(F32)
            o_ref[pl.ds(r0, chunk), hs] = _rms(hh, hn_ref[:, hs]).astype(BF16)
            c_loc = _dot((kt_scr[h * nc + c] * e_rows[row]).astype(BF16), v_c)
            ct_scr[h] = decay[row] * ct + c_loc
        return m_new

    lax.fori_loop(0, nc, chunk_step, jnp.zeros((GATE_ROWS, 1), F32), unroll=2)


def _mlstm(xc, vm, og, ifg, wq, wk, head_norm, bsz, seq):
    t = xc.shape[0]
    dh = MLSTM_HEAD_DIM
    nh = MLSTM_HEADS
    nc = seq // MLSTM_CHUNK
    blk = lambda b: (b, 0)
    return pl.pallas_call(
        functools.partial(_mlstm_kernel, seq=seq, chunk=MLSTM_CHUNK),
        grid=(bsz,),
        in_specs=[pl.BlockSpec((seq, MLSTM_WIDTH), blk), pl.BlockSpec((seq, MLSTM_WIDTH), blk),
                  pl.BlockSpec((seq, MLSTM_WIDTH), blk),
                  pl.BlockSpec((GATE_ROWS, seq), lambda b: (0, b)),
                  _resident((nh, dh, dh)), _resident((nh, dh, dh)), _resident((nh, dh, dh)),
                  _resident((1, MLSTM_WIDTH))],
        out_specs=pl.BlockSpec((seq, MLSTM_WIDTH), blk),
        out_shape=jax.ShapeDtypeStruct((t, MLSTM_WIDTH), BF16),
        scratch_shapes=[pltpu.VMEM((nh, seq, dh), BF16),
                        pltpu.VMEM((nh, seq, dh), BF16),
                        pltpu.VMEM((nh * nc, dh, MLSTM_CHUNK), F32),
                        pltpu.VMEM((nh, seq, 2 * dh), BF16),
                        pltpu.VMEM((nh, dh, 2 * dh), F32),
                        pltpu.VMEM((nc, GATE_ROWS, MLSTM_CHUNK), F32),
                        pltpu.VMEM((nc, GATE_ROWS, MLSTM_CHUNK), F32),
                        pltpu.VMEM((nc, GATE_ROWS, MLSTM_CHUNK), F32)],
        compiler_params=_params(1),
        name="mlstm",
    )(xc, vm, og, ifg, wq, wk, jnp.swapaxes(wk, 1, 2), head_norm.reshape(1, -1))


def _pad_cols(w, lo, width):
    return jnp.pad(w, ((0, 0), (lo, width - lo - w.shape[1])))


def _w_in_moves(d_in):
    src_kpe = MLA_Q_RANK + MLA_KV_RANK
    src_xm = src_kpe + MLA_ROPE_DIM
    src_if = src_xm + 3 * MLSTM_WIDTH
    src_gate = src_if + 2 * MLSTM_HEADS
    return ((0, _C_QLAT, src_kpe),
            (src_kpe, _C_KPE + ROPE_LO, MLA_ROPE_DIM),
            (src_xm, _C_XM, 3 * MLSTM_WIDTH),
            (src_if, _C_IF, 2 * MLSTM_HEADS),
            (src_gate, _C_GATE, d_in - src_gate))


def _layout_w_in_kernel(w_ref, o_ref):
    o_ref[...] = jnp.zeros(o_ref.shape, BF16)
    for src, dst, width in _w_in_moves(w_ref.shape[1]):
        o_ref[:, dst:dst + width] = w_ref[:, src:src + width].astype(BF16)


def _layout_w_in(w_in, layer):
    _, rows, d_in = w_in.shape
    d_out = _C_GATE + (d_in - _w_in_moves(d_in)[-1][0])
    tr = rows // 4
    return pl.pallas_call(
        _layout_w_in_kernel,
        grid=(rows // tr,),
        in_specs=[pl.BlockSpec((None, tr, d_in), lambda i: (layer, i, 0))],
        out_specs=pl.BlockSpec((tr, d_out), lambda i: (i, 0)),
        out_shape=jax.ShapeDtypeStruct((rows, d_out), BF16),
        compiler_params=_params(1),
        name="layout_w_in",
    )(w_in)


def _layout_w_q_b(w_q_b):
    w = w_q_b.reshape(MLA_Q_RANK, MLA_HEADS, MLA_QK_DIM)
    w = jnp.pad(w, ((0, 0), (0, 0), (0, HEAD_PAD - MLA_QK_DIM)))
    return w.reshape(MLA_Q_RANK, MLA_HEADS * HEAD_PAD).astype(BF16)


def _layout_w_kv_b(w_kv_b):
    w = w_kv_b.reshape(MLA_KV_RANK, MLA_HEADS, MLA_NOPE_DIM + MLA_V_DIM)
    wk = jnp.pad(w[:, :, :MLA_NOPE_DIM], ((0, 0), (0, 0), (0, HEAD_PAD - MLA_NOPE_DIM)))
    wv = w[:, :, MLA_NOPE_DIM:]
    return (wk.reshape(MLA_KV_RANK, MLA_HEADS * HEAD_PAD).astype(BF16),
            wv.reshape(MLA_KV_RANK, MLA_HEADS * MLA_V_DIM).astype(BF16))


def kernel(x, c, positions, w_ada, b_ada, norm_ff1, ff1_w_gate, ff1_w_up, ff1_w_down, norm_mix, w_in,
           q_a_norm, w_q_b, kv_a_norm, w_kv_b, conv_w, conv_b, w_q_m, w_k_m, b_i, b_f, mlstm_norm,
           w_mla_out, w_mlstm_out, w_o, norm_ff2, ff2_w_gate, ff2_w_up, ff2_w_down, norm_final):
    bsz, seq, d = x.shape
    t = bsz * seq
    depth = w_ada.shape[0]
    h = x.reshape(t, d)
    pos = _pack_positions(positions.reshape(t), MIX_TM)
    for l in range(depth):
        mod3 = _adaln(c, w_ada[l], b_ada[l]).reshape(bsz, 1, N_MOD * d)
        h, u_mix = _ffn(h, mod3, 0, norm_ff1[l], ff1_w_gate[l].astype(BF16), ff1_w_up[l].astype(BF16),
                        ff1_w_down[l].astype(BF16), seq, next_norm_g=norm_mix[l], next_chunk0=3)
        wk_r, wv_r = _layout_w_kv_b(w_kv_b[l])
        bif = _pad_cols(jnp.concatenate([b_i[l], b_f[l]]).reshape(1, -1), 0, V7X_LANES)
        q, k, v, xc, vm, og, ifg, gates = _mix_in(
            u_mix, pos, _layout_w_in(w_in, l), q_a_norm[l], _layout_w_q_b(w_q_b[l]),
            kv_a_norm[l], wk_r, wv_r, bif, conv_w[l], conv_b[l], seq)
        o_a = _attention(q, k, v, bsz, seq)
        o_b = _mlstm(xc, vm, og, ifg, w_q_m[l].astype(BF16), w_k_m[l].astype(BF16), mlstm_norm[l],
                     bsz, seq)
        mixer = (o_a, o_b, gates, 5, w_mla_out[l].astype(BF16), w_mlstm_out[l].astype(BF16),
                 w_o[l].astype(BF16))
        final_g = norm_final if l == depth - 1 else None
        h = _ffn(h, mod3, 6, norm_ff2[l], ff2_w_gate[l].astype(BF16), ff2_w_up[l].astype(BF16),
                 ff2_w_down[l].astype(BF16), seq, mixer=mixer, final_g=final_g)[0]
    return h.reshape(bsz, seq, d)
```

```python
import functools

import jax
import jax.numpy as jnp
from jax import lax
from jax.experimental import pallas as pl
from jax.experimental.pallas import tpu as pltpu

F32 = jnp.float32
BF16 = jnp.bfloat16

V7X_LANES = 128
MXU_COLS = 256
V7X_VMEM_BYTES = 64 * 1024 * 1024
VMEM_LIMIT_BYTES = V7X_VMEM_BYTES - 8 * 1024 * 1024

MLA_HEADS = 8
MLA_NOPE_DIM = 64
MLA_ROPE_DIM = 32
MLA_V_DIM = 64
MLA_QK_DIM = MLA_NOPE_DIM + MLA_ROPE_DIM
MLA_Q_RANK = 384
MLA_KV_RANK = 256
ROPE_THETA = 10000.0
MLSTM_HEADS = 4
MLSTM_HEAD_DIM = 128
MLSTM_WIDTH = MLSTM_HEADS * MLSTM_HEAD_DIM
CONV_WIDTH = 4
GATE_ROWS = 2 * MLSTM_HEADS
CONV_HALO = 8
N_BRANCHES = 2
N_MOD = 9
EPS = 1e-6
LOG2_E = 1.4426950408889634

HEAD_PAD = V7X_LANES
ROPE_LO = MLA_NOPE_DIM
ROPE_HALF = MLA_ROPE_DIM // 2

FFN_TM = 1024
FFN_TM_MIXER = 512
FFN_TF = 256
MIX_TM = 1024
MIX_LEAD_CHUNKS = 3
ATTN_TQ = 256
ATTN_HEADS = 4
ATTN_DEPTH = 4
MLSTM_CHUNK = 128


def _dot(a, b):
    return jnp.dot(a, b, preferred_element_type=F32)


def _rms(x, gain):
    return x * lax.rsqrt(jnp.mean(x * x, axis=-1, keepdims=True) + EPS) * gain


def _params(n_axes, flags=None):
    return pltpu.CompilerParams(
        dimension_semantics=("arbitrary",) * n_axes, vmem_limit_bytes=VMEM_LIMIT_BYTES, flags=flags)


def _resident(shape):
    zeros = (0,) * len(shape)
    return pl.BlockSpec(shape, lambda *_: zeros, pipeline_mode=pl.Buffered(1))


def _mod_spec(chunk, tm, seq, d):
    return pl.BlockSpec((None, 1, d), lambda i: ((i * tm) // seq, 0, chunk))


def _adaln_kernel(c_ref, w_ref, b_ref, o_ref):
    c = c_ref[...]
    sc = (c * jax.nn.sigmoid(c)).astype(BF16)
    o_ref[...] = _dot(sc, w_ref[...].astype(BF16)) + b_ref[...]


def _adaln(c, w_ada, b_ada):
    bsz, d = c.shape
    n = w_ada.shape[1]
    tn = d
    return pl.pallas_call(
        _adaln_kernel,
        grid=(n // tn,),
        in_specs=[pl.BlockSpec((bsz, d), lambda j: (0, 0)),
                  pl.BlockSpec((d, tn), lambda j: (0, j)),
                  pl.BlockSpec((1, tn), lambda j: (0, j))],
        out_specs=pl.BlockSpec((bsz, tn), lambda j: (0, j)),
        out_shape=jax.ShapeDtypeStruct((bsz, n), F32),
        compiler_params=_params(1),
        name="adaln",
    )(c, w_ada, b_ada.reshape(1, n))


def _modulated_norm(x, gain, shift, scale):
    return _rms(x, gain) * (1.0 + scale) + shift


def _ffn_kernel(*refs, d_ff, tf, mixer_prologue, epilogue):
    refs = list(refs)
    if mixer_prologue:
        h_ref, oa_ref, ob_ref, bg_ref, gtm_ref, wa_ref, wb_ref, wo_ref = refs[:8]
        refs = refs[8:]
    else:
        x_ref = refs.pop(0)
    g_ref, sh_ref, sc_ref, gt_ref, wg_ref, wu_ref, wd_ref = refs[:7]
    refs = refs[7:]
    if epilogue == "final_norm":
        gf_ref = refs.pop(0)
    elif epilogue == "next_mod":
        gn_ref, shn_ref, scn_ref = refs[:3]
        refs = refs[3:]
    o_ref = refs.pop(0)
    if epilogue == "next_mod":
        un_ref = refs.pop(0)
    u_scr, a_scr = refs[:2]

    if mixer_prologue:
        d = h_ref.shape[1]
        y_a = _dot(oa_ref[...], wa_ref[...])
        y_b = _dot(ob_ref[...], wb_ref[...])
        y = (jax.nn.sigmoid(bg_ref[:, 0:d].astype(F32)) * y_a
             + jax.nn.sigmoid(bg_ref[:, d:2 * d].astype(F32)) * y_b)
        x_ref = refs[2]
        x_ref[...] = h_ref[...] + gtm_ref[...] * _dot(y.astype(BF16), wo_ref[...])

    u_scr[...] = _modulated_norm(x_ref[...], g_ref[...], sh_ref[...], sc_ref[...]).astype(BF16)
    for j in range(d_ff // tf):
        cols = slice(j * tf, (j + 1) * tf)
        g = _dot(u_scr[...], wg_ref[:, cols])
        up = _dot(u_scr[...], wu_ref[:, cols])
        a_scr[:, cols] = (g * jax.nn.sigmoid(g) * up).astype(BF16)
    down = _dot(a_scr[...], wd_ref[...])
    out = x_ref[...] + (0.5 * gt_ref[...]) * down
    if epilogue == "final_norm":
        out = _rms(out, gf_ref[...])
    o_ref[...] = out
    if epilogue == "next_mod":
        un_ref[...] = _modulated_norm(out, gn_ref[...], shn_ref[...], scn_ref[...]).astype(BF16)


def _ffn(h, mod3, chunk0, norm_g, wg, wu, wd, seq, *, mixer=None, final_g=None, next_norm_g=None,
         next_chunk0=None):
    t, d = h.shape
    d_ff = wg.shape[1]
    tm = FFN_TM_MIXER if mixer is not None else FFN_TM
    row = lambda i: (i, 0)
    epilogue = "final_norm" if final_g is not None else "next_mod" if next_norm_g is not None else "plain"
    in_specs = [pl.BlockSpec((tm, d), row)]
    args = [h]
    scratch = [pltpu.VMEM((tm, d), BF16), pltpu.VMEM((tm, d_ff), BF16)]
    if mixer is not None:
        o_a, o_b, branch_gates, gate_chunk, w_a, w_b, w_o = mixer
        in_specs += [pl.BlockSpec((tm, o_a.shape[1]), row), pl.BlockSpec((tm, o_b.shape[1]), row),
                     pl.BlockSpec((tm, branch_gates.shape[1]), row),
                     _mod_spec(gate_chunk, tm, seq, d),
                     _resident(w_a.shape), _resident(w_b.shape), _resident(w_o.shape)]
        args += [o_a, o_b, branch_gates, mod3, w_a, w_b, w_o]
        scratch.append(pltpu.VMEM((tm, d), F32))
    in_specs += [_resident((1, d)), _mod_spec(chunk0, tm, seq, d), _mod_spec(chunk0 + 1, tm, seq, d),
                 _mod_spec(chunk0 + 2, tm, seq, d),
                 _resident((d, d_ff)), _resident((d, d_ff)), _resident((d_ff, d))]
    args += [norm_g.reshape(1, d), mod3, mod3, mod3, wg, wu, wd]
    out_specs = [pl.BlockSpec((tm, d), row)]
    out_shape = [jax.ShapeDtypeStruct((t, d), F32)]
    if epilogue == "final_norm":
        in_specs.append(_resident((1, d)))
        args.append(final_g.reshape(1, d))
    elif epilogue == "next_mod":
        in_specs += [_resident((1, d)), _mod_spec(next_chunk0, tm, seq, d),
                     _mod_spec(next_chunk0 + 1, tm, seq, d)]
        args += [next_norm_g.reshape(1, d), mod3, mod3]
        out_specs.append(pl.BlockSpec((tm, d), row))
        out_shape.append(jax.ShapeDtypeStruct((t, d), BF16))
    return pl.pallas_call(
        functools.partial(_ffn_kernel, d_ff=d_ff, tf=FFN_TF, mixer_prologue=mixer is not None,
                          epilogue=epilogue),
        grid=(t // tm,),
        in_specs=in_specs,
        out_specs=out_specs,
        out_shape=out_shape,
        scratch_shapes=scratch,
        compiler_params=_params(1),
        name=("mixout_ffn_" if mixer is not None else "ffn_") + epilogue,
    )(*args)


_C_QLAT = 0
_C_CKV = _C_QLAT + MLA_Q_RANK
_C_KPE = _C_CKV + MLA_KV_RANK
_C_XM = _C_KPE + HEAD_PAD
_C_VM = _C_XM + MLSTM_WIDTH
_C_OM = _C_VM + MLSTM_WIDTH
_C_IF = _C_OM + MLSTM_WIDTH
_C_GATE = _C_IF + V7X_LANES


ROPE_BLOCKS = HEAD_PAD // MLA_ROPE_DIM


def _pack_positions(positions, tm):
    rows = tm // ROPE_BLOCKS
    p = positions.reshape(-1, ROPE_BLOCKS, rows).transpose(0, 2, 1)
    return jnp.repeat(p, MLA_ROPE_DIM, axis=2).reshape(-1, HEAD_PAD)


def _rope_tables(pos_ref):
    n_blk = ROPE_BLOCKS
    lane = lax.broadcasted_iota(jnp.int32, (1, HEAD_PAD), 1)
    idx = (lane % ROPE_HALF).astype(F32)
    theta = jnp.full((1, HEAD_PAD), ROPE_THETA, F32)
    inv_freq = jnp.exp(-(idx / ROPE_HALF) * jnp.log(theta))
    ang = pos_ref[...].astype(F32) * inv_freq
    cos_p = jnp.cos(ang)
    sin_p = jnp.sin(ang)

    in_lo = (lane >= ROPE_LO) & (lane < ROPE_LO + ROPE_HALF)
    in_hi = (lane >= ROPE_LO + ROPE_HALF) & (lane < ROPE_LO + MLA_ROPE_DIM)
    cos_m, sin_up, sin_dn = [], [], []
    for b in range(n_blk):
        shift = (ROPE_LO - b * MLA_ROPE_DIM) % HEAD_PAD
        cos = cos_p if shift == 0 else pltpu.roll(cos_p, shift, axis=1)
        sin = sin_p if shift == 0 else pltpu.roll(sin_p, shift, axis=1)
        cos_m.append(jnp.where(lane < ROPE_LO, 1.0, jnp.where(in_lo | in_hi, cos, 0.0)))
        sin_up.append(jnp.where(in_lo, -sin, 0.0))
        sin_dn.append(jnp.where(in_hi, sin, 0.0))
    return tuple(jnp.concatenate(t, axis=0) for t in (cos_m, sin_up, sin_dn))


def _rope_group(x, tables):
    cos_m, sin_up, sin_dn = tables
    up = pltpu.roll(x, HEAD_PAD - ROPE_HALF, axis=1)
    dn = pltpu.roll(x, ROPE_HALF, axis=1)
    return x * cos_m + up * sin_up + dn * sin_dn


def _mixin_kernel(u_ref, pos_ref, win_ref, qan_ref, wq_ref, kvn_ref,
                  wk_ref, wv_ref, bif_ref, cw_ref, cb_ref,
                  q_out, k_out, v_out, xc_out, vm_out, og_out, if_out, gate_out, u_scr, xext_scr,
                  *, tm, seq):
    def col_chunks(width):
        return [slice(lo, min(lo + MXU_COLS, width)) for lo in range(0, width, MXU_COLS)]

    u_scr[...] = u_ref[...]

    def proj(lo, cols):
        return _dot(u_scr[...], win_ref[:, lo + cols.start:lo + cols.stop])

    def gate_chunk(cols):
        gate_out[:, cols] = proj(_C_GATE, cols).astype(BF16)

    gate_cols = col_chunks(gate_out.shape[1])
    q_lat = jnp.concatenate([proj(_C_QLAT, cols) for cols in col_chunks(MLA_Q_RANK)], axis=1)
    c_kv = proj(_C_CKV, slice(0, MLA_KV_RANK))
    k_pe_raw = proj(_C_KPE, slice(0, HEAD_PAD))
    for cols in gate_cols[:MIX_LEAD_CHUNKS]:
        gate_chunk(cols)
    tables = _rope_tables(pos_ref)
    q_n = _rms(q_lat, qan_ref[...]).astype(BF16)
    c_n = _rms(c_kv, kvn_ref[...]).astype(BF16)
    k_pe = _rope_group(k_pe_raw, tables)

    q_scale = MLA_QK_DIM ** -0.5 * LOG2_E

    def q_chunk(cols):
        q = _dot(q_n, wq_ref[:, cols])
        for g in range(MXU_COLS // HEAD_PAD):
            grp = slice(g * HEAD_PAD, (g + 1) * HEAD_PAD)
            out = slice(cols.start + grp.start, cols.start + grp.stop)
            q_out[:, out] = (_rope_group(q[:, grp], tables) * q_scale).astype(BF16)

    def k_chunk(cols):
        k_nope = _dot(c_n, wk_ref[:, cols])
        for g in range(MXU_COLS // HEAD_PAD):
            grp = slice(g * HEAD_PAD, (g + 1) * HEAD_PAD)
            out = slice(cols.start + grp.start, cols.start + grp.stop)
            k_out[:, out] = (k_nope[:, grp] + k_pe).astype(BF16)

    def v_chunk(cols):
        v_out[:, cols] = _dot(c_n, wv_ref[:, cols]).astype(BF16)

    @pl.when((pl.program_id(0) * tm) % seq == 0)
    def _():
        xext_scr[0:CONV_HALO, :] = jnp.zeros((CONV_HALO, MLSTM_WIDTH), F32)

    def conv_chunk(cols):
        xext_scr[CONV_HALO:CONV_HALO + tm, cols] = proj(_C_XM, cols)
        conv = cb_ref[:, cols] + (xext_scr[CONV_HALO:CONV_HALO + tm, cols]
                                  * cw_ref[CONV_WIDTH - 1:CONV_WIDTH, cols])
        for j in range(1, CONV_WIDTH):
            conv = conv + (xext_scr[CONV_HALO - j:CONV_HALO - j + tm, cols]
                           * cw_ref[CONV_WIDTH - 1 - j:CONV_WIDTH - j, cols])
        xext_scr[0:CONV_HALO, cols] = xext_scr[tm:tm + CONV_HALO, cols]
        xc_out[:, cols] = (conv * jax.nn.sigmoid(conv)).astype(BF16)

    def vm_chunk(cols):
        vm_out[:, cols] = proj(_C_VM, cols).astype(BF16)

    def og_chunk(cols):
        og_out[:, cols] = jax.nn.sigmoid(proj(_C_OM, cols)).astype(BF16)

    light = ([(gate_chunk, c) for c in gate_cols[MIX_LEAD_CHUNKS:]]
             + [(vm_chunk, c) for c in col_chunks(MLSTM_WIDTH)]
             + [(v_chunk, c) for c in col_chunks(MLA_HEADS * MLA_V_DIM)])
    heavy = ([(q_chunk, c) for c in col_chunks(MLA_HEADS * HEAD_PAD)]
             + [(conv_chunk, c) for c in col_chunks(MLSTM_WIDTH)]
             + [(og_chunk, c) for c in col_chunks(MLSTM_WIDTH)]
             + [(k_chunk, c) for c in col_chunks(MLA_HEADS * HEAD_PAD)])
    for n in range(max(len(light), len(heavy))):
        for items in (light, heavy):
            if n < len(items):
                fn, cols = items[n]
                fn(cols)
    if_out[...] = (proj(_C_IF, slice(0, V7X_LANES)) + bif_ref[...]).T[0:GATE_ROWS, :]


def _mix_in(u, pos, win_r, qan, wq_r, kvn, wk_r, wv_r, bif, conv_w, conv_b, seq):
    t, d = u.shape
    tm = MIX_TM
    row = lambda i: (i, 0)
    n_in = win_r.shape[1]
    qk_w = MLA_HEADS * HEAD_PAD
    v_w = MLA_HEADS * MLA_V_DIM
    outs = [(qk_w, BF16), (qk_w, BF16), (v_w, BF16), (MLSTM_WIDTH, BF16), (MLSTM_WIDTH, BF16),
            (MLSTM_WIDTH, BF16), None, (N_BRANCHES * d, BF16)]
    out_specs = [pl.BlockSpec((tm, o[0]), row) if o else pl.BlockSpec((GATE_ROWS, tm), lambda i: (0, i))
                 for o in outs]
    out_shape = [jax.ShapeDtypeStruct((t, o[0]), o[1]) if o else jax.ShapeDtypeStruct((GATE_ROWS, t), F32)
                 for o in outs]
    return pl.pallas_call(
        functools.partial(_mixin_kernel, tm=tm, seq=seq),
        grid=(t // tm,),
        in_specs=[pl.BlockSpec((tm, d), row), pl.BlockSpec((tm // ROPE_BLOCKS, HEAD_PAD), row),
                  _resident((d, n_in)), _resident((1, MLA_Q_RANK)),
                  _resident((MLA_Q_RANK, qk_w)), _resident((1, MLA_KV_RANK)),
                  _resident((MLA_KV_RANK, qk_w)), _resident((MLA_KV_RANK, v_w)),
                  _resident((1, V7X_LANES)), _resident((CONV_WIDTH, MLSTM_WIDTH)),
                  _resident((1, MLSTM_WIDTH))],
        out_specs=out_specs,
        out_shape=out_shape,
        scratch_shapes=[pltpu.VMEM((tm, d), BF16), pltpu.VMEM((tm + CONV_HALO, MLSTM_WIDTH), F32)],
        compiler_params=_params(1),
        name="mix_in",
    )(u, pos, win_r, qan.reshape(1, -1), wq_r, kvn.reshape(1, -1), wk_r, wv_r, bif, conv_w,
      conv_b.reshape(1, -1))


def _attn_kernel(q_ref, k_ref, v_ref, o_ref, vaug_scr, s_scr, *, seq, tq):
    pair_v = 2 * MLA_V_DIM
    lane = lax.broadcasted_iota(jnp.int32, (1, pair_v), 1)
    ones_col = jnp.where(
        lax.broadcasted_iota(jnp.int32, (seq, pair_v), 1) == 0, 1.0, 0.0).astype(BF16)
    for pair in range(ATTN_HEADS // 2):
        vaug_scr[pair, :, 0:pair_v] = v_ref[:, pair * pair_v:(pair + 1) * pair_v]
        vaug_scr[pair, :, pair_v:2 * pair_v] = ones_col
    r_i = lax.broadcasted_iota(jnp.int32, (tq, tq), 0)
    c_i = lax.broadcasted_iota(jnp.int32, (tq, tq), 1)
    causal = c_i <= r_i
    neg = jnp.finfo(F32).min
    order = list(reversed(range(seq // tq)))

    def buffer(qi, j):
        return (order.index(qi) % ATTN_DEPTH) * ATTN_HEADS + j

    def scores(qi):
        rows = slice(qi * tq, (qi + 1) * tq)
        maxima = []
        for j in range(ATTN_HEADS):
            grp = slice(j * HEAD_PAD, (j + 1) * HEAD_PAD)
            qh = q_ref[rows, grp]
            m_vec = None
            for cj in range(qi + 1):
                cols = slice(cj * tq, (cj + 1) * tq)
                s = lax.dot_general(qh, k_ref[cols, grp], (((1,), (1,)), ((), ())),
                                    preferred_element_type=F32)
                if cj == qi:
                    s = jnp.where(causal, s, neg)
                s_scr[buffer(qi, j), :, cols] = s
                for g in range(tq // V7X_LANES):
                    part = s[:, g * V7X_LANES:(g + 1) * V7X_LANES]
                    m_vec = part if m_vec is None else jnp.maximum(m_vec, part)
            maxima.append(jnp.max(m_vec, axis=-1, keepdims=True))
        return maxima

    def outputs(qi, maxima):
        outs = []
        for j in range(ATTN_HEADS):
            o_aug = None
            for cj in range(qi + 1):
                cols = slice(cj * tq, (cj + 1) * tq)
                p = jnp.exp2(s_scr[buffer(qi, j), :, cols] - maxima[j]).astype(BF16)
                part = _dot(p, vaug_scr[j // 2, cols, :])
                o_aug = part if o_aug is None else o_aug + part
            outs.append(o_aug[:, 0:pair_v] * (1.0 / o_aug[:, pair_v:pair_v + 1]))
        for pair in range(ATTN_HEADS // 2):
            o_ref[qi * tq:(qi + 1) * tq, pair * pair_v:(pair + 1) * pair_v] = jnp.where(
                lane < MLA_V_DIM, outs[2 * pair], outs[2 * pair + 1]).astype(BF16)

    lead = ATTN_DEPTH - 1
    pending = [scores(qi) for qi in order[:lead]]
    for n, qi in enumerate(order):
        if n + lead < len(order):
            pending.append(scores(order[n + lead]))
        outputs(qi, pending.pop(0))


def _attention(q, k, v, bsz, seq):
    t = q.shape[0]
    group_qk = ATTN_HEADS * HEAD_PAD
    group_v = ATTN_HEADS * MLA_V_DIM
    blk = lambda b, g: (b, g)
    return pl.pallas_call(
        functools.partial(_attn_kernel, seq=seq, tq=ATTN_TQ),
        grid=(bsz, MLA_HEADS // ATTN_HEADS),
        in_specs=[pl.BlockSpec((seq, group_qk), blk), pl.BlockSpec((seq, group_qk), blk),
                  pl.BlockSpec((seq, group_v), blk)],
        out_specs=pl.BlockSpec((seq, group_v), blk),
        out_shape=jax.ShapeDtypeStruct((t, MLA_HEADS * MLA_V_DIM), BF16),
        scratch_shapes=[pltpu.VMEM((ATTN_HEADS // 2, seq, 4 * MLA_V_DIM), BF16),
                        pltpu.VMEM((ATTN_HEADS * ATTN_DEPTH, ATTN_TQ, seq), F32)],
        compiler_params=_params(2),
        name="mla_attn",
    )(q, k, v)


def _log_sigmoid(x):
    return jnp.minimum(x, 0.0) - jnp.log1p(jnp.exp(-jnp.abs(x)))


def _chunk_scan(x, lane_in_chunk, chunk, combine, fill):
    step = 1
    while step < chunk:
        x = combine(x, jnp.where(lane_in_chunk >= step, pltpu.roll(x, step, axis=1), fill))
        step *= 2
    return x


def _mlstm_kernel(xc_ref, vm_ref, og_ref, if_ref, wq_ref, wk_ref, wkt_ref, hn_ref, o_ref,
                  q_scr, k_scr, kt_scr, vaug_scr, ct_scr, b_scr, src_scr, mi_scr, *, seq, chunk):
    dh = MLSTM_HEAD_DIM
    nh = MLSTM_HEADS
    nc = seq // chunk
    ones_col = jnp.where(lax.broadcasted_iota(jnp.int32, (seq, dh), 1) == 0, 1.0, 0.0).astype(BF16)
    for h in range(nh):
        hs = slice(h * dh, (h + 1) * dh)
        xc = xc_ref[:, hs]
        q_scr[h] = _dot(xc, wq_ref[h]).astype(BF16)
        k_scr[h] = (_dot(xc, wk_ref[h]) * (dh ** -0.5)).astype(BF16)
        kt = lax.dot_general(wkt_ref[h], xc, (((1,), (1,)), ((), ())),
                             preferred_element_type=F32) * (dh ** -0.5)
        for c in range(nc):
            kt_scr[h * nc + c] = kt[:, c * chunk:(c + 1) * chunk]
        vaug_scr[h, :, 0:dh] = vm_ref[:, hs]
        vaug_scr[h, :, dh:2 * dh] = ones_col
        ct_scr[h] = jnp.zeros((dh, 2 * dh), F32)

    gates_t = if_ref[...]
    lane_in_chunk = lax.broadcasted_iota(jnp.int32, (GATE_ROWS, seq), 1) % chunk
    b_all = _chunk_scan(_log_sigmoid(gates_t) * LOG2_E, lane_in_chunk, chunk, jnp.add, 0.0)
    b_all = pltpu.roll(b_all, nh, axis=0)
    src_all = gates_t * LOG2_E - b_all
    mi_all = b_all + _chunk_scan(src_all, lane_in_chunk, chunk, jnp.maximum, -jnp.inf)
    for c in range(nc):
        cs = slice(c * chunk, (c + 1) * chunk)
        b_scr[c] = b_all[:, cs]
        src_scr[c] = src_all[:, cs]
        mi_scr[c] = mi_all[:, cs]

    r_i = lax.broadcasted_iota(jnp.int32, (chunk, chunk), 0)
    c_i = lax.broadcasted_iota(jnp.int32, (chunk, chunk), 1)
    eye = r_i == c_i
    tril = c_i <= r_i
    last_lane = lax.broadcasted_iota(jnp.int32, (GATE_ROWS, chunk), 1) == chunk - 1

    def to_col(row):
        return jnp.sum(jnp.where(eye, row, 0.0), axis=1, keepdims=True)

    def chunk_step(c, m_prev):
        r0 = pl.multiple_of(c * chunk, chunk)
        b_c = b_scr[c]
        src_c = src_scr[c]
        g = jnp.sum(jnp.where(last_lane, b_c, 0.0), axis=1, keepdims=True)
        m_new = jnp.maximum(g + m_prev, jnp.max(g + src_c, axis=1, keepdims=True))
        decay = jnp.exp2(g + m_prev - m_new)
        e_rows = jnp.exp2(g + src_c - m_new)
        m_t = jnp.maximum(b_c + m_prev, mi_scr[c])
        u_rows = b_c - m_t
        nm_rows = jnp.exp2(-m_t)
        for h in range(nh):
            hs = slice(h * dh, (h + 1) * dh)
            row = slice(h, h + 1)
            u_col = to_col(u_rows[row])
            q_c = q_scr[h, pl.ds(r0, chunk), :]
            v_c = vaug_scr[h, pl.ds(r0, chunk), :]
            s_qk = lax.dot_general(q_c, k_scr[h, pl.ds(r0, chunk), :], (((1,), (1,)), ((), ())),
                                   preferred_element_type=F32)
            d_mat = jnp.exp2(jnp.where(tril, u_col + src_c[row], -jnp.inf))
            q_in = (q_c.astype(F32) * jnp.exp2(u_col + m_prev[row])).astype(BF16)
            ct = ct_scr[h]
            tot = _dot(q_in, ct.astype(BF16)) + _dot((d_mat * s_qk).astype(BF16), v_c)
            den = jnp.maximum(jnp.abs(tot[:, dh:dh + 1]), to_col(nm_rows[row]))
            hh = tot[:, 0:dh] * (1.0 / den) * og_ref[pl.ds(r0, chunk), hs].astype(F32)
            o_ref[pl.ds(r0, chunk), hs] = _rms(hh, hn_ref[:, hs]).astype(BF16)
            c_loc = _dot((kt_scr[h * nc + c] * e_rows[row]).astype(BF16), v_c)
            ct_scr[h] = decay[row] * ct + c_loc
        return m_new

    lax.fori_loop(0, nc, chunk_step, jnp.zeros((GATE_ROWS, 1), F32), unroll=2)


def _mlstm(xc, vm, og, ifg, wq, wk, head_norm, bsz, seq):
    t = xc.shape[0]
    dh = MLSTM_HEAD_DIM
    nh = MLSTM_HEADS
    nc = seq // MLSTM_CHUNK
    blk = lambda b: (b, 0)
    return pl.pallas_call(
        functools.partial(_mlstm_kernel, seq=seq, chunk=MLSTM_CHUNK),
        grid=(bsz,),
        in_specs=[pl.BlockSpec((seq, MLSTM_WIDTH), blk), pl.BlockSpec((seq, MLSTM_WIDTH), blk),
                  pl.BlockSpec((seq, MLSTM_WIDTH), blk),
                  pl.BlockSpec((GATE_ROWS, seq), lambda b: (0, b)),
                  _resident((nh, dh, dh)), _resident((nh, dh, dh)), _resident((nh, dh, dh)),
                  _resident((1, MLSTM_WIDTH))],
        out_specs=pl.BlockSpec((seq, MLSTM_WIDTH), blk),
        out_shape=jax.ShapeDtypeStruct((t, MLSTM_WIDTH), BF16),
        scratch_shapes=[pltpu.VMEM((nh, seq, dh), BF16),
                        pltpu.VMEM((nh, seq, dh), BF16),
                        pltpu.VMEM((nh * nc, dh, MLSTM_CHUNK), F32),
                        pltpu.VMEM((nh, seq, 2 * dh), BF16),
                        pltpu.VMEM((nh, dh, 2 * dh), F32),
                        pltpu.VMEM((nc, GATE_ROWS, MLSTM_CHUNK), F32),
                        pltpu.VMEM((nc, GATE_ROWS, MLSTM_CHUNK), F32),
                        pltpu.VMEM((nc, GATE_ROWS, MLSTM_CHUNK), F32)],
        compiler_params=_params(1),
        name="mlstm",
    )(xc, vm, og, ifg, wq, wk, jnp.swapaxes(wk, 1, 2), head_norm.reshape(1, -1))


def _pad_cols(w, lo, width):
    return jnp.pad(w, ((0, 0), (lo, width - lo - w.shape[1])))


def _w_in_moves(d_in):
    src_kpe = MLA_Q_RANK + MLA_KV_RANK
    src_xm = src_kpe + MLA_ROPE_DIM
    src_if = src_xm + 3 * MLSTM_WIDTH
    src_gate = src_if + 2 * MLSTM_HEADS
    return ((0, _C_QLAT, src_kpe),
            (src_kpe, _C_KPE + ROPE_LO, MLA_ROPE_DIM),
            (src_xm, _C_XM, 3 * MLSTM_WIDTH),
            (src_if, _C_IF, 2 * MLSTM_HEADS),
            (src_gate, _C_GATE, d_in - src_gate))


def _layout_w_in_kernel(w_ref, o_ref):
    o_ref[...] = jnp.zeros(o_ref.shape, BF16)
    for src, dst, width in _w_in_moves(w_ref.shape[1]):
        o_ref[:, dst:dst + width] = w_ref[:, src:src + width].astype(BF16)


def _layout_w_in(w_in, layer):
    _, rows, d_in = w_in.shape
    d_out = _C_GATE + (d_in - _w_in_moves(d_in)[-1][0])
    tr = rows // 4
    return pl.pallas_call(
        _layout_w_in_kernel,
        grid=(rows // tr,),
        in_specs=[pl.BlockSpec((None, tr, d_in), lambda i: (layer, i, 0))],
        out_specs=pl.BlockSpec((tr, d_out), lambda i: (i, 0)),
        out_shape=jax.ShapeDtypeStruct((rows, d_out), BF16),
        compiler_params=_params(1),
        name="layout_w_in",
    )(w_in)


def _layout_w_q_b(w_q_b):
    w = w_q_b.reshape(MLA_Q_RANK, MLA_HEADS, MLA_QK_DIM)
    w = jnp.pad(w, ((0, 0), (0, 0), (0, HEAD_PAD - MLA_QK_DIM)))
    return w.reshape(MLA_Q_RANK, MLA_HEADS * HEAD_PAD).astype(BF16)


def _layout_w_kv_b(w_kv_b):
    w = w_kv_b.reshape(MLA_KV_RANK, MLA_HEADS, MLA_NOPE_DIM + MLA_V_DIM)
    wk = jnp.pad(w[:, :, :MLA_NOPE_DIM], ((0, 0), (0, 0), (0, HEAD_PAD - MLA_NOPE_DIM)))
    wv = w[:, :, MLA_NOPE_DIM:]
    return (wk.reshape(MLA_KV_RANK, MLA_HEADS * HEAD_PAD).astype(BF16),
            wv.reshape(MLA_KV_RANK, MLA_HEADS * MLA_V_DIM).astype(BF16))


def kernel(x, c, positions, w_ada, b_ada, norm_ff1, ff1_w_gate, ff1_w_up, ff1_w_down, norm_mix, w_in,
           q_a_norm, w_q_b, kv_a_norm, w_kv_b, conv_w, conv_b, w_q_m, w_k_m, b_i, b_f, mlstm_norm,
           w_mla_out, w_mlstm_out, w_o, norm_ff2, ff2_w_gate, ff2_w_up, ff2_w_down, norm_final):
    bsz, seq, d = x.shape
    t = bsz * seq
    depth = w_ada.shape[0]
    h = x.reshape(t, d)
    pos = _pack_positions(positions.reshape(t), MIX_TM)
    for l in range(depth):
        mod3 = _adaln(c, w_ada[l], b_ada[l]).reshape(bsz, 1, N_MOD * d)
        h, u_mix = _ffn(h, mod3, 0, norm_ff1[l], ff1_w_gate[l].astype(BF16), ff1_w_up[l].astype(BF16),
                        ff1_w_down[l].astype(BF16), seq, next_norm_g=norm_mix[l], next_chunk0=3)
        wk_r, wv_r = _layout_w_kv_b(w_kv_b[l])
        bif = _pad_cols(jnp.concatenate([b_i[l], b_f[l]]).reshape(1, -1), 0, V7X_LANES)
        q, k, v, xc, vm, og, ifg, gates = _mix_in(
            u_mix, pos, _layout_w_in(w_in, l), q_a_norm[l], _layout_w_q_b(w_q_b[l]),
            kv_a_norm[l], wk_r, wv_r, bif, conv_w[l], conv_b[l], seq)
        o_a = _attention(q, k, v, bsz, seq)
        o_b = _mlstm(xc, vm, og, ifg, w_q_m[l].astype(BF16), w_k_m[l].astype(BF16), mlstm_norm[l],
                     bsz, seq)
        mixer = (o_a, o_b, gates, 5, w_mla_out[l].astype(BF16), w_mlstm_out[l].astype(BF16),
                 w_o[l].astype(BF16))
        final_g = norm_final if l == depth - 1 else None
        h = _ffn(h, mod3, 6, norm_ff2[l], ff2_w_gate[l].astype(BF16), ff2_w_up[l].astype(BF16),
                 ff2_w_down[l].astype(BF16), seq, mixer=mixer, final_g=final_g)[0]
    return h.reshape(bsz, seq, d)
```

```python
import functools

import jax
import jax.numpy as jnp
from jax import lax
from jax.experimental import pallas as pl
from jax.experimental.pallas import tpu as pltpu

F32 = jnp.float32
BF16 = jnp.bfloat16

V7X_LANES = 128
MXU_COLS = 256
V7X_VMEM_BYTES = 64 * 1024 * 1024
VMEM_LIMIT_BYTES = V7X_VMEM_BYTES - 8 * 1024 * 1024

MLA_HEADS = 8
MLA_NOPE_DIM = 64
MLA_ROPE_DIM = 32
MLA_V_DIM = 64
MLA_QK_DIM = MLA_NOPE_DIM + MLA_ROPE_DIM
MLA_Q_RANK = 384
MLA_KV_RANK = 256
ROPE_THETA = 10000.0
MLSTM_HEADS = 4
MLSTM_HEAD_DIM = 128
MLSTM_WIDTH = MLSTM_HEADS * MLSTM_HEAD_DIM
CONV_WIDTH = 4
GATE_ROWS = 2 * MLSTM_HEADS
CONV_HALO = 8
N_BRANCHES = 2
N_MOD = 9
EPS = 1e-6
LOG2_E = 1.4426950408889634

HEAD_PAD = V7X_LANES
ROPE_LO = MLA_NOPE_DIM
ROPE_HALF = MLA_ROPE_DIM // 2

FFN_TM = 1024
FFN_TM_MIXER = 512
FFN_TF = 256
MIX_TM = 1024
MIX_LEAD_CHUNKS = 3
ATTN_TQ = 256
ATTN_HEADS = 4
ATTN_DEPTH = 4
MLSTM_CHUNK = 256


def _dot(a, b):
    return jnp.dot(a, b, preferred_element_type=F32)


def _rms(x, gain):
    return x * lax.rsqrt(jnp.mean(x * x, axis=-1, keepdims=True) + EPS) * gain


def _params(n_axes, flags=None):
    return pltpu.CompilerParams(
        dimension_semantics=("arbitrary",) * n_axes, vmem_limit_bytes=VMEM_LIMIT_BYTES, flags=flags)


def _resident(shape):
    zeros = (0,) * len(shape)
    return pl.BlockSpec(shape, lambda *_: zeros, pipeline_mode=pl.Buffered(1))


def _mod_spec(chunk, tm, seq, d):
    return pl.BlockSpec((None, 1, d), lambda i: ((i * tm) // seq, 0, chunk))


def _adaln_kernel(c_ref, w_ref, b_ref, o_ref):
    c = c_ref[...]
    sc = (c * jax.nn.sigmoid(c)).astype(BF16)
    o_ref[...] = _dot(sc, w_ref[...].astype(BF16)) + b_ref[...]


def _adaln(c, w_ada, b_ada):
    bsz, d = c.shape
    n = w_ada.shape[1]
    tn = d
    return pl.pallas_call(
        _adaln_kernel,
        grid=(n // tn,),
        in_specs=[pl.BlockSpec((bsz, d), lambda j: (0, 0)),
                  pl.BlockSpec((d, tn), lambda j: (0, j)),
                  pl.BlockSpec((1, tn), lambda j: (0, j))],
        out_specs=pl.BlockSpec((bsz, tn), lambda j: (0, j)),
        out_shape=jax.ShapeDtypeStruct((bsz, n), F32),
        compiler_params=_params(1),
        name="adaln",
    )(c, w_ada, b_ada.reshape(1, n))


def _modulated_norm(x, gain, shift, scale):
    return _rms(x, gain) * (1.0 + scale) + shift


def _ffn_kernel(*refs, d_ff, tf, mixer_prologue, epilogue):
    refs = list(refs)
    if mixer_prologue:
        h_ref, oa_ref, ob_ref, bg_ref, gtm_ref, wa_ref, wb_ref, wo_ref = refs[:8]
        refs = refs[8:]
    else:
        x_ref = refs.pop(0)
    g_ref, sh_ref, sc_ref, gt_ref, wg_ref, wu_ref, wd_ref = refs[:7]
    refs = refs[7:]
    if epilogue == "final_norm":
        gf_ref = refs.pop(0)
    elif epilogue == "next_mod":
        gn_ref, shn_ref, scn_ref = refs[:3]
        refs = refs[3:]
    o_ref = refs.pop(0)
    if epilogue == "next_mod":
        un_ref = refs.pop(0)
    u_scr, a_scr = refs[:2]

    if mixer_prologue:
        d = h_ref.shape[1]
        y_a = _dot(oa_ref[...], wa_ref[...])
        y_b = _dot(ob_ref[...], wb_ref[...])
        y = (jax.nn.sigmoid(bg_ref[:, 0:d].astype(F32)) * y_a
             + jax.nn.sigmoid(bg_ref[:, d:2 * d].astype(F32)) * y_b)
        x_ref = refs[2]
        x_ref[...] = h_ref[...] + gtm_ref[...] * _dot(y.astype(BF16), wo_ref[...])

    u_scr[...] = _modulated_norm(x_ref[...], g_ref[...], sh_ref[...], sc_ref[...]).astype(BF16)
    for j in range(d_ff // tf):
        cols = slice(j * tf, (j + 1) * tf)
        g = _dot(u_scr[...], wg_ref[:, cols])
        up = _dot(u_scr[...], wu_ref[:, cols])
        a_scr[:, cols] = (g * jax.nn.sigmoid(g) * up).astype(BF16)
    down = _dot(a_scr[...], wd_ref[...])
    out = x_ref[...] + (0.5 * gt_ref[...]) * down
    if epilogue == "final_norm":
        out = _rms(out, gf_ref[...])
    o_ref[...] = out
    if epilogue == "next_mod":
        un_ref[...] = _modulated_norm(out, gn_ref[...], shn_ref[...], scn_ref[...]).astype(BF16)


def _ffn(h, mod3, chunk0, norm_g, wg, wu, wd, seq, *, mixer=None, final_g=None, next_norm_g=None,
         next_chunk0=None):
    t, d = h.shape
    d_ff = wg.shape[1]
    tm = FFN_TM_MIXER if mixer is not None else FFN_TM
    row = lambda i: (i, 0)
    epilogue = "final_norm" if final_g is not None else "next_mod" if next_norm_g is not None else "plain"
    in_specs = [pl.BlockSpec((tm, d), row)]
    args = [h]
    scratch = [pltpu.VMEM((tm, d), BF16), pltpu.VMEM((tm, d_ff), BF16)]
    if mixer is not None:
        o_a, o_b, branch_gates, gate_chunk, w_a, w_b, w_o = mixer
        in_specs += [pl.BlockSpec((tm, o_a.shape[1]), row), pl.BlockSpec((tm, o_b.shape[1]), row),
                     pl.BlockSpec((tm, branch_gates.shape[1]), row),
                     _mod_spec(gate_chunk, tm, seq, d),
                     _resident(w_a.shape), _resident(w_b.shape), _resident(w_o.shape)]
        args += [o_a, o_b, branch_gates, mod3, w_a, w_b, w_o]
        scratch.append(pltpu.VMEM((tm, d), F32))
    in_specs += [_resident((1, d)), _mod_spec(chunk0, tm, seq, d), _mod_spec(chunk0 + 1, tm, seq, d),
                 _mod_spec(chunk0 + 2, tm, seq, d),
                 _resident((d, d_ff)), _resident((d, d_ff)), _resident((d_ff, d))]
    args += [norm_g.reshape(1, d), mod3, mod3, mod3, wg, wu, wd]
    out_specs = [pl.BlockSpec((tm, d), row)]
    out_shape = [jax.ShapeDtypeStruct((t, d), F32)]
    if epilogue == "final_norm":
        in_specs.append(_resident((1, d)))
        args.append(final_g.reshape(1, d))
    elif epilogue == "next_mod":
        in_specs += [_resident((1, d)), _mod_spec(next_chunk0, tm, seq, d),
                     _mod_spec(next_chunk0 + 1, tm, seq, d)]
        args += [next_norm_g.reshape(1, d), mod3, mod3]
        out_specs.append(pl.BlockSpec((tm, d), row))
        out_shape.append(jax.ShapeDtypeStruct((t, d), BF16))
    return pl.pallas_call(
        functools.partial(_ffn_kernel, d_ff=d_ff, tf=FFN_TF, mixer_prologue=mixer is not None,
                          epilogue=epilogue),
        grid=(t // tm,),
        in_specs=in_specs,
        out_specs=out_specs,
        out_shape=out_shape,
        scratch_shapes=scratch,
        compiler_params=_params(1),
        name=("mixout_ffn_" if mixer is not None else "ffn_") + epilogue,
    )(*args)


_C_QLAT = 0
_C_CKV = _C_QLAT + MLA_Q_RANK
_C_KPE = _C_CKV + MLA_KV_RANK
_C_XM = _C_KPE + HEAD_PAD
_C_VM = _C_XM + MLSTM_WIDTH
_C_OM = _C_VM + MLSTM_WIDTH
_C_IF = _C_OM + MLSTM_WIDTH
_C_GATE = _C_IF + V7X_LANES


ROPE_BLOCKS = HEAD_PAD // MLA_ROPE_DIM


def _pack_positions(positions, tm):
    rows = tm // ROPE_BLOCKS
    p = positions.reshape(-1, ROPE_BLOCKS, rows).transpose(0, 2, 1)
    return jnp.repeat(p, MLA_ROPE_DIM, axis=2).reshape(-1, HEAD_PAD)


def _rope_tables(pos_ref):
    n_blk = ROPE_BLOCKS
    lane = lax.broadcasted_iota(jnp.int32, (1, HEAD_PAD), 1)
    idx = (lane % ROPE_HALF).astype(F32)
    theta = jnp.full((1, HEAD_PAD), ROPE_THETA, F32)
    inv_freq = jnp.exp(-(idx / ROPE_HALF) * jnp.log(theta))
    ang = pos_ref[...].astype(F32) * inv_freq
    cos_p = jnp.cos(ang)
    sin_p = jnp.sin(ang)

    in_lo = (lane >= ROPE_LO) & (lane < ROPE_LO + ROPE_HALF)
    in_hi = (lane >= ROPE_LO + ROPE_HALF) & (lane < ROPE_LO + MLA_ROPE_DIM)
    cos_m, sin_up, sin_dn = [], [], []
    for b in range(n_blk):
        shift = (ROPE_LO - b * MLA_ROPE_DIM) % HEAD_PAD
        cos = cos_p if shift == 0 else pltpu.roll(cos_p, shift, axis=1)
        sin = sin_p if shift == 0 else pltpu.roll(sin_p, shift, axis=1)
        cos_m.append(jnp.where(lane < ROPE_LO, 1.0, jnp.where(in_lo | in_hi, cos, 0.0)))
        sin_up.append(jnp.where(in_lo, -sin, 0.0))
        sin_dn.append(jnp.where(in_hi, sin, 0.0))
    return tuple(jnp.concatenate(t, axis=0) for t in (cos_m, sin_up, sin_dn))


def _rope_group(x, tables):
    cos_m, sin_up, sin_dn = tables
    up = pltpu.roll(x, HEAD_PAD - ROPE_HALF, axis=1)
    dn = pltpu.roll(x, ROPE_HALF, axis=1)
    return x * cos_m + up * sin_up + dn * sin_dn


def _mixin_kernel(u_ref, pos_ref, win_ref, qan_ref, wq_ref, kvn_ref,
                  wk_ref, wv_ref, bif_ref, cw_ref, cb_ref,
                  q_out, k_out, v_out, xc_out, vm_out, og_out, if_out, gate_out, u_scr, xext_scr,
                  *, tm, seq):
    def col_chunks(width):
        return [slice(lo, min(lo + MXU_COLS, width)) for lo in range(0, width, MXU_COLS)]

    u_scr[...] = u_ref[...]

    def proj(lo, cols):
        return _dot(u_scr[...], win_ref[:, lo + cols.start:lo + cols.stop])

    def gate_chunk(cols):
        gate_out[:, cols] = proj(_C_GATE, cols).astype(BF16)

    gate_cols = col_chunks(gate_out.shape[1])
    q_lat = jnp.concatenate([proj(_C_QLAT, cols) for cols in col_chunks(MLA_Q_RANK)], axis=1)
    c_kv = proj(_C_CKV, slice(0, MLA_KV_RANK))
    k_pe_raw = proj(_C_KPE, slice(0, HEAD_PAD))
    for cols in gate_cols[:MIX_LEAD_CHUNKS]:
        gate_chunk(cols)
    tables = _rope_tables(pos_ref)
    q_n = _rms(q_lat, qan_ref[...]).astype(BF16)
    c_n = _rms(c_kv, kvn_ref[...]).astype(BF16)
    k_pe = _rope_group(k_pe_raw, tables)

    q_scale = MLA_QK_DIM ** -0.5 * LOG2_E

    def q_chunk(cols):
        q = _dot(q_n, wq_ref[:, cols])
        for g in range(MXU_COLS // HEAD_PAD):
            grp = slice(g * HEAD_PAD, (g + 1) * HEAD_PAD)
            out = slice(cols.start + grp.start, cols.start + grp.stop)
            q_out[:, out] = (_rope_group(q[:, grp], tables) * q_scale).astype(BF16)

    def k_chunk(cols):
        k_nope = _dot(c_n, wk_ref[:, cols])
        for g in range(MXU_COLS // HEAD_PAD):
            grp = slice(g * HEAD_PAD, (g + 1) * HEAD_PAD)
            out = slice(cols.start + grp.start, cols.start + grp.stop)
            k_out[:, out] = (k_nope[:, grp] + k_pe).astype(BF16)

    def v_chunk(cols):
        v_out[:, cols] = _dot(c_n, wv_ref[:, cols]).astype(BF16)

    @pl.when((pl.program_id(0) * tm) % seq == 0)
    def _():
        xext_scr[0:CONV_HALO, :] = jnp.zeros((CONV_HALO, MLSTM_WIDTH), F32)

    def conv_chunk(cols):
        xext_scr[CONV_HALO:CONV_HALO + tm, cols] = proj(_C_XM, cols)
        conv = cb_ref[:, cols] + (xext_scr[CONV_HALO:CONV_HALO + tm, cols]
                                  * cw_ref[CONV_WIDTH - 1:CONV_WIDTH, cols])
        for j in range(1, CONV_WIDTH):
            conv = conv + (xext_scr[CONV_HALO - j:CONV_HALO - j + tm, cols]
                           * cw_ref[CONV_WIDTH - 1 - j:CONV_WIDTH - j, cols])
        xext_scr[0:CONV_HALO, cols] = xext_scr[tm:tm + CONV_HALO, cols]
        xc_out[:, cols] = (conv * jax.nn.sigmoid(conv)).astype(BF16)

    def vm_chunk(cols):
        vm_out[:, cols] = proj(_C_VM, cols).astype(BF16)

    def og_chunk(cols):
        og_out[:, cols] = jax.nn.sigmoid(proj(_C_OM, cols)).astype(BF16)

    light = ([(gate_chunk, c) for c in gate_cols[MIX_LEAD_CHUNKS:]]
             + [(vm_chunk, c) for c in col_chunks(MLSTM_WIDTH)]
             + [(v_chunk, c) for c in col_chunks(MLA_HEADS * MLA_V_DIM)])
    heavy = ([(q_chunk, c) for c in col_chunks(MLA_HEADS * HEAD_PAD)]
             + [(conv_chunk, c) for c in col_chunks(MLSTM_WIDTH)]
             + [(og_chunk, c) for c in col_chunks(MLSTM_WIDTH)]
             + [(k_chunk, c) for c in col_chunks(MLA_HEADS * HEAD_PAD)])
    for n in range(max(len(light), len(heavy))):
        for items in (light, heavy):
            if n < len(items):
                fn, cols = items[n]
                fn(cols)
    if_out[...] = (proj(_C_IF, slice(0, V7X_LANES)) + bif_ref[...]).T[0:GATE_ROWS, :]


def _mix_in(u, pos, win_r, qan, wq_r, kvn, wk_r, wv_r, bif, conv_w, conv_b, seq):
    t, d = u.shape
    tm = MIX_TM
    row = lambda i: (i, 0)
    n_in = win_r.shape[1]
    qk_w = MLA_HEADS * HEAD_PAD
    v_w = MLA_HEADS * MLA_V_DIM
    outs = [(qk_w, BF16), (qk_w, BF16), (v_w, BF16), (MLSTM_WIDTH, BF16), (MLSTM_WIDTH, BF16),
            (MLSTM_WIDTH, BF16), None, (N_BRANCHES * d, BF16)]
    out_specs = [pl.BlockSpec((tm, o[0]), row) if o else pl.BlockSpec((GATE_ROWS, tm), lambda i: (0, i))
                 for o in outs]
    out_shape = [jax.ShapeDtypeStruct((t, o[0]), o[1]) if o else jax.ShapeDtypeStruct((GATE_ROWS, t), F32)
                 for o in outs]
    return pl.pallas_call(
        functools.partial(_mixin_kernel, tm=tm, seq=seq),
        grid=(t // tm,),
        in_specs=[pl.BlockSpec((tm, d), row), pl.BlockSpec((tm // ROPE_BLOCKS, HEAD_PAD), row),
                  _resident((d, n_in)), _resident((1, MLA_Q_RANK)),
                  _resident((MLA_Q_RANK, qk_w)), _resident((1, MLA_KV_RANK)),
                  _resident((MLA_KV_RANK, qk_w)), _resident((MLA_KV_RANK, v_w)),
                  _resident((1, V7X_LANES)), _resident((CONV_WIDTH, MLSTM_WIDTH)),
                  _resident((1, MLSTM_WIDTH))],
        out_specs=out_specs,
        out_shape=out_shape,
        scratch_shapes=[pltpu.VMEM((tm, d), BF16), pltpu.VMEM((tm + CONV_HALO, MLSTM_WIDTH), F32)],
        compiler_params=_params(1),
        name="mix_in",
    )(u, pos, win_r, qan.reshape(1, -1), wq_r, kvn.reshape(1, -1), wk_r, wv_r, bif, conv_w,
      conv_b.reshape(1, -1))


def _attn_kernel(q_ref, k_ref, v_ref, o_ref, vaug_scr, s_scr, *, seq, tq):
    pair_v = 2 * MLA_V_DIM
    lane = lax.broadcasted_iota(jnp.int32, (1, pair_v), 1)
    ones_col = jnp.where(
        lax.broadcasted_iota(jnp.int32, (seq, pair_v), 1) == 0, 1.0, 0.0).astype(BF16)
    for pair in range(ATTN_HEADS // 2):
        vaug_scr[pair, :, 0:pair_v] = v_ref[:, pair * pair_v:(pair + 1) * pair_v]
        vaug_scr[pair, :, pair_v:2 * pair_v] = ones_col
    r_i = lax.broadcasted_iota(jnp.int32, (tq, tq), 0)
    c_i = lax.broadcasted_iota(jnp.int32, (tq, tq), 1)
    causal = c_i <= r_i
    neg = jnp.finfo(F32).min
    order = list(reversed(range(seq // tq)))

    def buffer(qi, j):
        return (order.index(qi) % ATTN_DEPTH) * ATTN_HEADS + j

    def scores(qi):
        rows = slice(qi * tq, (qi + 1) * tq)
        maxima = []
        for j in range(ATTN_HEADS):
            grp = slice(j * HEAD_PAD, (j + 1) * HEAD_PAD)
            qh = q_ref[rows, grp]
            m_vec = None
            for cj in range(qi + 1):
                cols = slice(cj * tq, (cj + 1) * tq)
                s = lax.dot_general(qh, k_ref[cols, grp], (((1,), (1,)), ((), ())),
                                    preferred_element_type=F32)
                if cj == qi:
                    s = jnp.where(causal, s, neg)
                s_scr[buffer(qi, j), :, cols] = s
                for g in range(tq // V7X_LANES):
                    part = s[:, g * V7X_LANES:(g + 1) * V7X_LANES]
                    m_vec = part if m_vec is None else jnp.maximum(m_vec, part)
            maxima.append(jnp.max(m_vec, axis=-1, keepdims=True))
        return maxima

    def outputs(qi, maxima):
        outs = []
        for j in range(ATTN_HEADS):
            o_aug = None
            for cj in range(qi + 1):
                cols = slice(cj * tq, (cj + 1) * tq)
                p = jnp.exp2(s_scr[buffer(qi, j), :, cols] - maxima[j]).astype(BF16)
                part = _dot(p, vaug_scr[j // 2, cols, :])
                o_aug = part if o_aug is None else o_aug + part
            outs.append(o_aug[:, 0:pair_v] * (1.0 / o_aug[:, pair_v:pair_v + 1]))
        for pair in range(ATTN_HEADS // 2):
            o_ref[qi * tq:(qi + 1) * tq, pair * pair_v:(pair + 1) * pair_v] = jnp.where(
                lane < MLA_V_DIM, outs[2 * pair], outs[2 * pair + 1]).astype(BF16)

    lead = ATTN_DEPTH - 1
    pending = [scores(qi) for qi in order[:lead]]
    for n, qi in enumerate(order):
        if n + lead < len(order):
            pending.append(scores(order[n + lead]))
        outputs(qi, pending.pop(0))


def _attention(q, k, v, bsz, seq):
    t = q.shape[0]
    group_qk = ATTN_HEADS * HEAD_PAD
    group_v = ATTN_HEADS * MLA_V_DIM
    blk = lambda b, g: (b, g)
    return pl.pallas_call(
        functools.partial(_attn_kernel, seq=seq, tq=ATTN_TQ),
        grid=(bsz, MLA_HEADS // ATTN_HEADS),
        in_specs=[pl.BlockSpec((seq, group_qk), blk), pl.BlockSpec((seq, group_qk), blk),
                  pl.BlockSpec((seq, group_v), blk)],
        out_specs=pl.BlockSpec((seq, group_v), blk),
        out_shape=jax.ShapeDtypeStruct((t, MLA_HEADS * MLA_V_DIM), BF16),
        scratch_shapes=[pltpu.VMEM((ATTN_HEADS // 2, seq, 4 * MLA_V_DIM), BF16),
                        pltpu.VMEM((ATTN_HEADS * ATTN_DEPTH, ATTN_TQ, seq), F32)],
        compiler_params=_params(2),
        name="mla_attn",
    )(q, k, v)


def _log_sigmoid(x):
    return jnp.minimum(x, 0.0) - jnp.log1p(jnp.exp(-jnp.abs(x)))


def _chunk_scan(x, lane_in_chunk, chunk, combine, fill):
    step = 1
    while step < chunk:
        x = combine(x, jnp.where(lane_in_chunk >= step, pltpu.roll(x, step, axis=1), fill))
        step *= 2
    return x


def _mlstm_kernel(xc_ref, vm_ref, og_ref, if_ref, wq_ref, wk_ref, wkt_ref, hn_ref, o_ref,
                  q_scr, k_scr, kt_scr, vaug_scr, ct_scr, b_scr, src_scr, mi_scr, *, seq, chunk):
    dh = MLSTM_HEAD_DIM
    nh = MLSTM_HEADS
    nc = seq // chunk
    ones_col = jnp.where(lax.broadcasted_iota(jnp.int32, (seq, dh), 1) == 0, 1.0, 0.0).astype(BF16)
    for h in range(nh):
        hs = slice(h * dh, (h + 1) * dh)
        xc = xc_ref[:, hs]
        q_scr[h] = _dot(xc, wq_ref[h]).astype(BF16)
        k_scr[h] = (_dot(xc, wk_ref[h]) * (dh ** -0.5)).astype(BF16)
        kt = lax.dot_general(wkt_ref[h], xc, (((1,), (1,)), ((), ())),
                             preferred_element_type=F32) * (dh ** -0.5)
        for c in range(nc):
            kt_scr[h * nc + c] = kt[:, c * chunk:(c + 1) * chunk]
        vaug_scr[h, :, 0:dh] = vm_ref[:, hs]
        vaug_scr[h, :, dh:2 * dh] = ones_col
        ct_scr[h] = jnp.zeros((dh, 2 * dh), F32)

    gates_t = if_ref[...]
    lane_in_chunk = lax.broadcasted_iota(jnp.int32, (GATE_ROWS, seq), 1) % chunk
    b_all = _chunk_scan(_log_sigmoid(gates_t) * LOG2_E, lane_in_chunk, chunk, jnp.add, 0.0)
    b_all = pltpu.roll(b_all, nh, axis=0)
    src_all = gates_t * LOG2_E - b_all
    mi_all = b_all + _chunk_scan(src_all, lane_in_chunk, chunk, jnp.maximum, -jnp.inf)
    for c in range(nc):
        cs = slice(c * chunk, (c + 1) * chunk)
        b_scr[c] = b_all[:, cs]
        src_scr[c] = src_all[:, cs]
        mi_scr[c] = mi_all[:, cs]

    r_i = lax.broadcasted_iota(jnp.int32, (chunk, chunk), 0)
    c_i = lax.broadcasted_iota(jnp.int32, (chunk, chunk), 1)
    eye = r_i == c_i
    tril = c_i <= r_i
    last_lane = lax.broadcasted_iota(jnp.int32, (GATE_ROWS, chunk), 1) == chunk - 1

    def to_col(row):
        return jnp.sum(jnp.where(eye, row, 0.0), axis=1, keepdims=True)

    def chunk_step(c, m_prev):
        r0 = pl.multiple_of(c * chunk, chunk)
        b_c = b_scr[c]
        src_c = src_scr[c]
        g = jnp.sum(jnp.where(last_lane, b_c, 0.0), axis=1, keepdims=True)
        m_new = jnp.maximum(g + m_prev, jnp.max(g + src_c, axis=1, keepdims=True))
        decay = jnp.exp2(g + m_prev - m_new)
        e_rows = jnp.exp2(g + src_c - m_new)
        m_t = jnp.maximum(b_c + m_prev, mi_scr[c])
        u_rows = b_c - m_t
        nm_rows = jnp.exp2(-m_t)
        for h in range(nh):
            hs = slice(h * dh, (h + 1) * dh)
            row = slice(h, h + 1)
            u_col = to_col(u_rows[row])
            q_c = q_scr[h, pl.ds(r0, chunk), :]
            v_c = vaug_scr[h, pl.ds(r0, chunk), :]
            s_qk = lax.dot_general(q_c, k_scr[h, pl.ds(r0, chunk), :], (((1,), (1,)), ((), ())),
                                   preferred_element_type=F32)
            d_mat = jnp.exp2(jnp.where(tril, u_col + src_c[row], -jnp.inf))
            q_in = (q_c.astype(F32) * jnp.exp2(u_col + m_prev[row])).astype(BF16)
            ct = ct_scr[h]
            tot = _dot(q_in, ct.astype(BF16)) + _dot((d_mat * s_qk).astype(BF16), v_c)
            den = jnp.maximum(jnp.abs(tot[:, dh:dh + 1]), to_col(nm_rows[row]))
            hh = tot[:, 0:dh] * (1.0 / den) * og_ref[pl.ds(r0, chunk), hs].astype(F32)
            o_ref[pl.ds(r0, chunk), hs] = _rms(hh, hn_ref[:, hs]).astype(BF16)
            c_loc = _dot((kt_scr[h * nc + c] * e_rows[row]).astype(BF16), v_c)
            ct_scr[h] = decay[row] * ct + c_loc
        return m_new

    lax.fori_loop(0, nc, chunk_step, jnp.zeros((GATE_ROWS, 1), F32), unroll=2)


def _mlstm(xc, vm, og, ifg, wq, wk, head_norm, bsz, seq):
    t = xc.shape[0]
    dh = MLSTM_HEAD_DIM
    nh = MLSTM_HEADS
    nc = seq // MLSTM_CHUNK
    blk = lambda b: (b, 0)
    return pl.pallas_call(
        functools.partial(_mlstm_kernel, seq=seq, chunk=MLSTM_CHUNK),
        grid=(bsz,),
        in_specs=[pl.BlockSpec((seq, MLSTM_WIDTH), blk), pl.BlockSpec((seq, MLSTM_WIDTH), blk),
                  pl.BlockSpec((seq, MLSTM_WIDTH), blk),
                  pl.BlockSpec((GATE_ROWS, seq), lambda b: (0, b)),
                  _resident((nh, dh, dh)), _resident((nh, dh, dh)), _resident((nh, dh, dh)),
                  _resident((1, MLSTM_WIDTH))],
        out_specs=pl.BlockSpec((seq, MLSTM_WIDTH), blk),
        out_shape=jax.ShapeDtypeStruct((t, MLSTM_WIDTH), BF16),
        scratch_shapes=[pltpu.VMEM((nh, seq, dh), BF16),
                        pltpu.VMEM((nh, seq, dh), BF16),
                        pltpu.VMEM((nh * nc, dh, MLSTM_CHUNK), F32),
                        pltpu.VMEM((nh, seq, 2 * dh), BF16),
                        pltpu.VMEM((nh, dh, 2 * dh), F32),
                        pltpu.VMEM((nc, GATE_ROWS, MLSTM_CHUNK), F32),
                        pltpu.VMEM((nc, GATE_ROWS, MLSTM_CHUNK), F32),
                        pltpu.VMEM((nc, GATE_ROWS, MLSTM_CHUNK), F32)],
        compiler_params=_params(1),
        name="mlstm",
    )(xc, vm, og, ifg, wq, wk, jnp.swapaxes(wk, 1, 2), head_norm.reshape(1, -1))


def _pad_cols(w, lo, width):
    return jnp.pad(w, ((0, 0), (lo, width - lo - w.shape[1])))


def _w_in_moves(d_in):
    src_kpe = MLA_Q_RANK + MLA_KV_RANK
    src_xm = src_kpe + MLA_ROPE_DIM
    src_if = src_xm + 3 * MLSTM_WIDTH
    src_gate = src_if + 2 * MLSTM_HEADS
    return ((0, _C_QLAT, src_kpe),
            (src_kpe, _C_KPE + ROPE_LO, MLA_ROPE_DIM),
            (src_xm, _C_XM, 3 * MLSTM_WIDTH),
            (src_if, _C_IF, 2 * MLSTM_HEADS),
            (src_gate, _C_GATE, d_in - src_gate))


def _layout_w_in_kernel(w_ref, o_ref):
    o_ref[...] = jnp.zeros(o_ref.shape, BF16)
    for src, dst, width in _w_in_moves(w_ref.shape[1]):
        o_ref[:, dst:dst + width] = w_ref[:, src:src + width].astype(BF16)


def _layout_w_in(w_in, layer):
    _, rows, d_in = w_in.shape
    d_out = _C_GATE + (d_in - _w_in_moves(d_in)[-1][0])
    tr = rows // 4
    return pl.pallas_call(
        _layout_w_in_kernel,
        grid=(rows // tr,),
        in_specs=[pl.BlockSpec((None, tr, d_in), lambda i: (layer, i, 0))],
        out_specs=pl.BlockSpec((tr, d_out), lambda i: (i, 0)),
        out_shape=jax.ShapeDtypeStruct((rows, d_out), BF16),
        compiler_params=_params(1),
        name="layout_w_in",
    )(w_in)


def _layout_w_q_b(w_q_b):
    w = w_q_b.reshape(MLA_Q_RANK, MLA_HEADS, MLA_QK_DIM)
    w = jnp.pad(w, ((0, 0), (0, 0), (0, HEAD_PAD - MLA_QK_DIM)))
    return w.reshape(MLA_Q_RANK, MLA_HEADS * HEAD_PAD).astype(BF16)


def _layout_w_kv_b(w_kv_b):
    w = w_kv_b.reshape(MLA_KV_RANK, MLA_HEADS, MLA_NOPE_DIM + MLA_V_DIM)
    wk = jnp.pad(w[:, :, :MLA_NOPE_DIM], ((0, 0), (0, 0), (0, HEAD_PAD - MLA_NOPE_DIM)))
    wv = w[:, :, MLA_NOPE_DIM:]
    return (wk.reshape(MLA_KV_RANK, MLA_HEADS * HEAD_PAD).astype(BF16),
            wv.reshape(MLA_KV_RANK, MLA_HEADS * MLA_V_DIM).astype(BF16))


def kernel(x, c, positions, w_ada, b_ada, norm_ff1, ff1_w_gate, ff1_w_up, ff1_w_down, norm_mix, w_in,
           q_a_norm, w_q_b, kv_a_norm, w_kv_b, conv_w, conv_b, w_q_m, w_k_m, b_i, b_f, mlstm_norm,
           w_mla_out, w_mlstm_out, w_o, norm_ff2, ff2_w_gate, ff2_w_up, ff2_w_down, norm_final):
    bsz, seq, d = x.shape
    t = bsz * seq
    depth = w_ada.shape[0]
    h = x.reshape(t, d)
    pos = _pack_positions(positions.reshape(t), MIX_TM)
    for l in range(depth):
        mod3 = _adaln(c, w_ada[l], b_ada[l]).reshape(bsz, 1, N_MOD * d)
        h, u_mix = _ffn(h, mod3, 0, norm_ff1[l], ff1_w_gate[l].astype(BF16), ff1_w_up[l].astype(BF16),
                        ff1_w_down[l].astype(BF16), seq, next_norm_g=norm_mix[l], next_chunk0=3)
        wk_r, wv_r = _layout_w_kv_b(w_kv_b[l])
        bif = _pad_cols(jnp.concatenate([b_i[l], b_f[l]]).reshape(1, -1), 0, V7X_LANES)
        q, k, v, xc, vm, og, ifg, gates = _mix_in(
            u_mix, pos, _layout_w_in(w_in, l), q_a_norm[l], _layout_w_q_b(w_q_b[l]),
            kv_a_norm[l], wk_r, wv_r, bif, conv_w[l], conv_b[l], seq)
        o_a = _attention(q, k, v, bsz, seq)
        o_b = _mlstm(xc, vm, og, ifg, w_q_m[l].astype(BF16), w_k_m[l].astype(BF16), mlstm_norm[l],
                     bsz, seq)
        mixer = (o_a, o_b, gates, 5, w_mla_out[l].astype(BF16), w_mlstm_out[l].astype(BF16),
                 w_o[l].astype(BF16))
        final_g = norm_final if l == depth - 1 else None
        h = _ffn(h, mod3, 6, norm_ff2[l], ff2_w_gate[l].astype(BF16), ff2_w_up[l].astype(BF16),
                 ff2_w_down[l].astype(BF16), seq, mixer=mixer, final_g=final_g)[0]
    return h.reshape(bsz, seq, d)
```

```python
import functools

import jax
import jax.numpy as jnp
from jax import lax
from jax.experimental import pallas as pl
from jax.experimental.pallas import tpu as pltpu

F32 = jnp.float32
BF16 = jnp.bfloat16

V7X_LANES = 128
MXU_COLS = 256
V7X_VMEM_BYTES = 64 * 1024 * 1024
VMEM_LIMIT_BYTES = V7X_VMEM_BYTES - 8 * 1024 * 1024

MLA_HEADS = 8
MLA_NOPE_DIM = 64
MLA_ROPE_DIM = 32
MLA_V_DIM = 64
MLA_QK_DIM = MLA_NOPE_DIM + MLA_ROPE_DIM
MLA_Q_RANK = 384
MLA_KV_RANK = 256
ROPE_THETA = 10000.0
MLSTM_HEADS = 4
MLSTM_HEAD_DIM = 128
MLSTM_WIDTH = MLSTM_HEADS * MLSTM_HEAD_DIM
CONV_WIDTH = 4
GATE_ROWS = 2 * MLSTM_HEADS
CONV_HALO = 8
N_BRANCHES = 2
N_MOD = 9
EPS = 1e-6
LOG2_E = 1.4426950408889634

HEAD_PAD = V7X_LANES
ROPE_LO = MLA_NOPE_DIM
ROPE_HALF = MLA_ROPE_DIM // 2

ADALN_COLS = 3072
FFN_TM = 1024
FFN_TM_MIXER = 512
FFN_TF = 256
MIX_TM = 1024
MIX_LEAD_CHUNKS = 3
ATTN_TQ = 256
ATTN_HEADS = 4
ATTN_DEPTH = 4
MLSTM_CHUNK = 256


def _dot(a, b):
    return jnp.dot(a, b, preferred_element_type=F32)


def _rms(x, gain):
    return x * lax.rsqrt(jnp.mean(x * x, axis=-1, keepdims=True) + EPS) * gain


def _params(n_axes, flags=None):
    return pltpu.CompilerParams(
        dimension_semantics=("arbitrary",) * n_axes, vmem_limit_bytes=VMEM_LIMIT_BYTES, flags=flags)


def _resident(shape):
    zeros = (0,) * len(shape)
    return pl.BlockSpec(shape, lambda *_: zeros, pipeline_mode=pl.Buffered(1))


def _mod_spec(chunk, tm, seq, d):
    return pl.BlockSpec((None, 1, d), lambda i: ((i * tm) // seq, 0, chunk))


def _adaln_kernel(c_ref, w_ref, b_ref, o_ref):
    c = c_ref[...]
    sc = (c * jax.nn.sigmoid(c)).astype(BF16)
    o_ref[...] = _dot(sc, w_ref[...].astype(BF16)) + b_ref[...]


def _adaln(c, w_ada, b_ada):
    bsz, d = c.shape
    n = w_ada.shape[1]
    tn = ADALN_COLS
    return pl.pallas_call(
        _adaln_kernel,
        grid=(n // tn,),
        in_specs=[pl.BlockSpec((bsz, d), lambda j: (0, 0)),
                  pl.BlockSpec((d, tn), lambda j: (0, j)),
                  pl.BlockSpec((1, tn), lambda j: (0, j))],
        out_specs=pl.BlockSpec((bsz, tn), lambda j: (0, j)),
        out_shape=jax.ShapeDtypeStruct((bsz, n), F32),
        compiler_params=_params(1),
        name="adaln",
    )(c, w_ada, b_ada.reshape(1, n))


def _modulated_norm(x, gain, shift, scale):
    return _rms(x, gain) * (1.0 + scale) + shift


def _ffn_kernel(*refs, d_ff, tf, mixer_prologue, epilogue):
    refs = list(refs)
    if mixer_prologue:
        h_ref, oa_ref, ob_ref, bg_ref, gtm_ref, wa_ref, wb_ref, wo_ref = refs[:8]
        refs = refs[8:]
    else:
        x_ref = refs.pop(0)
    g_ref, sh_ref, sc_ref, gt_ref, wg_ref, wu_ref, wd_ref = refs[:7]
    refs = refs[7:]
    if epilogue == "final_norm":
        gf_ref = refs.pop(0)
    elif epilogue == "next_mod":
        gn_ref, shn_ref, scn_ref = refs[:3]
        refs = refs[3:]
    o_ref = refs.pop(0)
    if epilogue == "next_mod":
        un_ref = refs.pop(0)
    u_scr, a_scr = refs[:2]

    if mixer_prologue:
        d = h_ref.shape[1]
        y_a = _dot(oa_ref[...], wa_ref[...])
        y_b = _dot(ob_ref[...], wb_ref[...])
        y = (jax.nn.sigmoid(bg_ref[:, 0:d].astype(F32)) * y_a
             + jax.nn.sigmoid(bg_ref[:, d:2 * d].astype(F32)) * y_b)
        x_ref = refs[2]
        x_ref[...] = h_ref[...] + gtm_ref[...] * _dot(y.astype(BF16), wo_ref[...])

    u_scr[...] = _modulated_norm(x_ref[...], g_ref[...], sh_ref[...], sc_ref[...]).astype(BF16)
    for j in range(d_ff // tf):
        cols = slice(j * tf, (j + 1) * tf)
        g = _dot(u_scr[...], wg_ref[:, cols])
        up = _dot(u_scr[...], wu_ref[:, cols])
        a_scr[:, cols] = (g * jax.nn.sigmoid(g) * up).astype(BF16)
    down = _dot(a_scr[...], wd_ref[...])
    out = x_ref[...] + (0.5 * gt_ref[...]) * down
    if epilogue == "final_norm":
        out = _rms(out, gf_ref[...])
    o_ref[...] = out
    if epilogue == "next_mod":
        un_ref[...] = _modulated_norm(out, gn_ref[...], shn_ref[...], scn_ref[...]).astype(BF16)


def _ffn(h, mod3, chunk0, norm_g, wg, wu, wd, seq, *, mixer=None, final_g=None, next_norm_g=None,
         next_chunk0=None):
    t, d = h.shape
    d_ff = wg.shape[1]
    tm = FFN_TM_MIXER if mixer is not None else FFN_TM
    row = lambda i: (i, 0)
    epilogue = "final_norm" if final_g is not None else "next_mod" if next_norm_g is not None else "plain"
    in_specs = [pl.BlockSpec((tm, d), row)]
    args = [h]
    scratch = [pltpu.VMEM((tm, d), BF16), pltpu.VMEM((tm, d_ff), BF16)]
    if mixer is not None:
        o_a, o_b, branch_gates, gate_chunk, w_a, w_b, w_o = mixer
        in_specs += [pl.BlockSpec((tm, o_a.shape[1]), row), pl.BlockSpec((tm, o_b.shape[1]), row),
                     pl.BlockSpec((tm, branch_gates.shape[1]), row),
                     _mod_spec(gate_chunk, tm, seq, d),
                     _resident(w_a.shape), _resident(w_b.shape), _resident(w_o.shape)]
        args += [o_a, o_b, branch_gates, mod3, w_a, w_b, w_o]
        scratch.append(pltpu.VMEM((tm, d), F32))
    in_specs += [_resident((1, d)), _mod_spec(chunk0, tm, seq, d), _mod_spec(chunk0 + 1, tm, seq, d),
                 _mod_spec(chunk0 + 2, tm, seq, d),
                 _resident((d, d_ff)), _resident((d, d_ff)), _resident((d_ff, d))]
    args += [norm_g.reshape(1, d), mod3, mod3, mod3, wg, wu, wd]
    out_specs = [pl.BlockSpec((tm, d), row)]
    out_shape = [jax.ShapeDtypeStruct((t, d), F32)]
    if epilogue == "final_norm":
        in_specs.append(_resident((1, d)))
        args.append(final_g.reshape(1, d))
    elif epilogue == "next_mod":
        in_specs += [_resident((1, d)), _mod_spec(next_chunk0, tm, seq, d),
                     _mod_spec(next_chunk0 + 1, tm, seq, d)]
        args += [next_norm_g.reshape(1, d), mod3, mod3]
        out_specs.append(pl.BlockSpec((tm, d), row))
        out_shape.append(jax.ShapeDtypeStruct((t, d), BF16))
    return pl.pallas_call(
        functools.partial(_ffn_kernel, d_ff=d_ff, tf=FFN_TF, mixer_prologue=mixer is not None,
                          epilogue=epilogue),
        grid=(t // tm,),
        in_specs=in_specs,
        out_specs=out_specs,
        out_shape=out_shape,
        scratch_shapes=scratch,
        compiler_params=_params(1),
        name=("mixout_ffn_" if mixer is not None else "ffn_") + epilogue,
    )(*args)


_C_QLAT = 0
_C_CKV = _C_QLAT + MLA_Q_RANK
_C_KPE = _C_CKV + MLA_KV_RANK
_C_XM = _C_KPE + HEAD_PAD
_C_VM = _C_XM + MLSTM_WIDTH
_C_OM = _C_VM + MLSTM_WIDTH
_C_IF = _C_OM + MLSTM_WIDTH
_C_GATE = _C_IF + V7X_LANES


ROPE_BLOCKS = HEAD_PAD // MLA_ROPE_DIM


def _pack_positions(positions, tm):
    rows = tm // ROPE_BLOCKS
    p = positions.reshape(-1, ROPE_BLOCKS, rows).transpose(0, 2, 1)
    return jnp.repeat(p, MLA_ROPE_DIM, axis=2).reshape(-1, HEAD_PAD)


def _rope_tables(pos_ref):
    n_blk = ROPE_BLOCKS
    lane = lax.broadcasted_iota(jnp.int32, (1, HEAD_PAD), 1)
    idx = (lane % ROPE_HALF).astype(F32)
    theta = jnp.full((1, HEAD_PAD), ROPE_THETA, F32)
    inv_freq = jnp.exp(-(idx / ROPE_HALF) * jnp.log(theta))
    ang = pos_ref[...].astype(F32) * inv_freq
    cos_p = jnp.cos(ang)
    sin_p = jnp.sin(ang)

    in_lo = (lane >= ROPE_LO) & (lane < ROPE_LO + ROPE_HALF)
    in_hi = (lane >= ROPE_LO + ROPE_HALF) & (lane < ROPE_LO + MLA_ROPE_DIM)
    cos_m, sin_up, sin_dn = [], [], []
    for b in range(n_blk):
        shift = (ROPE_LO - b * MLA_ROPE_DIM) % HEAD_PAD
        cos = cos_p if shift == 0 else pltpu.roll(cos_p, shift, axis=1)
        sin = sin_p if shift == 0 else pltpu.roll(sin_p, shift, axis=1)
        cos_m.append(jnp.where(lane < ROPE_LO, 1.0, jnp.where(in_lo | in_hi, cos, 0.0)))
        sin_up.append(jnp.where(in_lo, -sin, 0.0))
        sin_dn.append(jnp.where(in_hi, sin, 0.0))
    return tuple(jnp.concatenate(t, axis=0) for t in (cos_m, sin_up, sin_dn))


def _rope_group(x, tables):
    cos_m, sin_up, sin_dn = tables
    up = pltpu.roll(x, HEAD_PAD - ROPE_HALF, axis=1)
    dn = pltpu.roll(x, ROPE_HALF, axis=1)
    return x * cos_m + up * sin_up + dn * sin_dn


def _mixin_kernel(u_ref, pos_ref, win_ref, qan_ref, wq_ref, kvn_ref,
                  wk_ref, wv_ref, bif_ref, cw_ref, cb_ref,
                  q_out, k_out, v_out, xc_out, vm_out, og_out, if_out, gate_out, u_scr, xext_scr,
                  *, tm, seq):
    def col_chunks(width):
        return [slice(lo, min(lo + MXU_COLS, width)) for lo in range(0, width, MXU_COLS)]

    u_scr[...] = u_ref[...]

    def proj(lo, cols):
        return _dot(u_scr[...], win_ref[:, lo + cols.start:lo + cols.stop])

    def gate_chunk(cols):
        gate_out[:, cols] = proj(_C_GATE, cols).astype(BF16)

    gate_cols = col_chunks(gate_out.shape[1])
    q_lat = jnp.concatenate([proj(_C_QLAT, cols) for cols in col_chunks(MLA_Q_RANK)], axis=1)
    c_kv = proj(_C_CKV, slice(0, MLA_KV_RANK))
    k_pe_raw = proj(_C_KPE, slice(0, HEAD_PAD))
    for cols in gate_cols[:MIX_LEAD_CHUNKS]:
        gate_chunk(cols)
    tables = _rope_tables(pos_ref)
    q_n = _rms(q_lat, qan_ref[...]).astype(BF16)
    c_n = _rms(c_kv, kvn_ref[...]).astype(BF16)
    k_pe = _rope_group(k_pe_raw, tables)

    q_scale = MLA_QK_DIM ** -0.5 * LOG2_E

    def q_chunk(cols):
        q = _dot(q_n, wq_ref[:, cols])
        for g in range(MXU_COLS // HEAD_PAD):
            grp = slice(g * HEAD_PAD, (g + 1) * HEAD_PAD)
            out = slice(cols.start + grp.start, cols.start + grp.stop)
            q_out[:, out] = (_rope_group(q[:, grp], tables) * q_scale).astype(BF16)

    def k_chunk(cols):
        k_nope = _dot(c_n, wk_ref[:, cols])
        for g in range(MXU_COLS // HEAD_PAD):
            grp = slice(g * HEAD_PAD, (g + 1) * HEAD_PAD)
            out = slice(cols.start + grp.start, cols.start + grp.stop)
            k_out[:, out] = (k_nope[:, grp] + k_pe).astype(BF16)

    def v_chunk(cols):
        v_out[:, cols] = _dot(c_n, wv_ref[:, cols]).astype(BF16)

    @pl.when((pl.program_id(0) * tm) % seq == 0)
    def _():
        xext_scr[0:CONV_HALO, :] = jnp.zeros((CONV_HALO, MLSTM_WIDTH), F32)

    def conv_chunk(cols):
        xext_scr[CONV_HALO:CONV_HALO + tm, cols] = proj(_C_XM, cols)
        conv = cb_ref[:, cols] + (xext_scr[CONV_HALO:CONV_HALO + tm, cols]
                                  * cw_ref[CONV_WIDTH - 1:CONV_WIDTH, cols])
        for j in range(1, CONV_WIDTH):
            conv = conv + (xext_scr[CONV_HALO - j:CONV_HALO - j + tm, cols]
                           * cw_ref[CONV_WIDTH - 1 - j:CONV_WIDTH - j, cols])
        xext_scr[0:CONV_HALO, cols] = xext_scr[tm:tm + CONV_HALO, cols]
        xc_out[:, cols] = (conv * jax.nn.sigmoid(conv)).astype(BF16)

    def vm_chunk(cols):
        vm_out[:, cols] = proj(_C_VM, cols).astype(BF16)

    def og_chunk(cols):
        og_out[:, cols] = jax.nn.sigmoid(proj(_C_OM, cols)).astype(BF16)

    light = ([(gate_chunk, c) for c in gate_cols[MIX_LEAD_CHUNKS:]]
             + [(vm_chunk, c) for c in col_chunks(MLSTM_WIDTH)]
             + [(v_chunk, c) for c in col_chunks(MLA_HEADS * MLA_V_DIM)])
    heavy = ([(q_chunk, c) for c in col_chunks(MLA_HEADS * HEAD_PAD)]
             + [(conv_chunk, c) for c in col_chunks(MLSTM_WIDTH)]
             + [(og_chunk, c) for c in col_chunks(MLSTM_WIDTH)]
             + [(k_chunk, c) for c in col_chunks(MLA_HEADS * HEAD_PAD)])
    for n in range(max(len(light), len(heavy))):
        for items in (light, heavy):
            if n < len(items):
                fn, cols = items[n]
                fn(cols)
    if_out[...] = (proj(_C_IF, slice(0, V7X_LANES)) + bif_ref[...]).T[0:GATE_ROWS, :]


def _mix_in(u, pos, win_r, qan, wq_r, kvn, wk_r, wv_r, bif, conv_w, conv_b, seq):
    t, d = u.shape
    tm = MIX_TM
    row = lambda i: (i, 0)
    n_in = win_r.shape[1]
    qk_w = MLA_HEADS * HEAD_PAD
    v_w = MLA_HEADS * MLA_V_DIM
    outs = [(qk_w, BF16), (qk_w, BF16), (v_w, BF16), (MLSTM_WIDTH, BF16), (MLSTM_WIDTH, BF16),
            (MLSTM_WIDTH, BF16), None, (N_BRANCHES * d, BF16)]
    out_specs = [pl.BlockSpec((tm, o[0]), row) if o else pl.BlockSpec((GATE_ROWS, tm), lambda i: (0, i))
                 for o in outs]
    out_shape = [jax.ShapeDtypeStruct((t, o[0]), o[1]) if o else jax.ShapeDtypeStruct((GATE_ROWS, t), F32)
                 for o in outs]
    return pl.pallas_call(
        functools.partial(_mixin_kernel, tm=tm, seq=seq),
        grid=(t // tm,),
        in_specs=[pl.BlockSpec((tm, d), row), pl.BlockSpec((tm // ROPE_BLOCKS, HEAD_PAD), row),
                  _resident((d, n_in)), _resident((1, MLA_Q_RANK)),
                  _resident((MLA_Q_RANK, qk_w)), _resident((1, MLA_KV_RANK)),
                  _resident((MLA_KV_RANK, qk_w)), _resident((MLA_KV_RANK, v_w)),
                  _resident((1, V7X_LANES)), _resident((CONV_WIDTH, MLSTM_WIDTH)),
                  _resident((1, MLSTM_WIDTH))],
        out_specs=out_specs,
        out_shape=out_shape,
        scratch_shapes=[pltpu.VMEM((tm, d), BF16), pltpu.VMEM((tm + CONV_HALO, MLSTM_WIDTH), F32)],
        compiler_params=_params(1),
        name="mix_in",
    )(u, pos, win_r, qan.reshape(1, -1), wq_r, kvn.reshape(1, -1), wk_r, wv_r, bif, conv_w,
      conv_b.reshape(1, -1))


def _attn_kernel(q_ref, k_ref, v_ref, o_ref, vaug_scr, s_scr, *, seq, tq):
    pair_v = 2 * MLA_V_DIM
    lane = lax.broadcasted_iota(jnp.int32, (1, pair_v), 1)
    ones_col = jnp.where(
        lax.broadcasted_iota(jnp.int32, (seq, pair_v), 1) == 0, 1.0, 0.0).astype(BF16)
    for pair in range(ATTN_HEADS // 2):
        vaug_scr[pair, :, 0:pair_v] = v_ref[:, pair * pair_v:(pair + 1) * pair_v]
        vaug_scr[pair, :, pair_v:2 * pair_v] = ones_col
    r_i = lax.broadcasted_iota(jnp.int32, (tq, tq), 0)
    c_i = lax.broadcasted_iota(jnp.int32, (tq, tq), 1)
    causal = c_i <= r_i
    neg = jnp.finfo(F32).min
    order = list(reversed(range(seq // tq)))

    def buffer(qi, j):
        return (order.index(qi) % ATTN_DEPTH) * ATTN_HEADS + j

    def scores(qi):
        rows = slice(qi * tq, (qi + 1) * tq)
        maxima = []
        for j in range(ATTN_HEADS):
            grp = slice(j * HEAD_PAD, (j + 1) * HEAD_PAD)
            qh = q_ref[rows, grp]
            m_vec = None
            for cj in range(qi + 1):
                cols = slice(cj * tq, (cj + 1) * tq)
                s = lax.dot_general(qh, k_ref[cols, grp], (((1,), (1,)), ((), ())),
                                    preferred_element_type=F32)
                if cj == qi:
                    s = jnp.where(causal, s, neg)
                s_scr[buffer(qi, j), :, cols] = s
                for g in range(tq // V7X_LANES):
                    part = s[:, g * V7X_LANES:(g + 1) * V7X_LANES]
                    m_vec = part if m_vec is None else jnp.maximum(m_vec, part)
            maxima.append(jnp.max(m_vec, axis=-1, keepdims=True))
        return maxima

    def outputs(qi, maxima):
        outs = []
        for j in range(ATTN_HEADS):
            o_aug = None
            for cj in range(qi + 1):
                cols = slice(cj * tq, (cj + 1) * tq)
                p = jnp.exp2(s_scr[buffer(qi, j), :, cols] - maxima[j]).astype(BF16)
                part = _dot(p, vaug_scr[j // 2, cols, :])
                o_aug = part if o_aug is None else o_aug + part
            outs.append(o_aug[:, 0:pair_v] * (1.0 / o_aug[:, pair_v:pair_v + 1]))
        for pair in range(ATTN_HEADS // 2):
            o_ref[qi * tq:(qi + 1) * tq, pair * pair_v:(pair + 1) * pair_v] = jnp.where(
                lane < MLA_V_DIM, outs[2 * pair], outs[2 * pair + 1]).astype(BF16)

    lead = ATTN_DEPTH - 1
    pending = [scores(qi) for qi in order[:lead]]
    for n, qi in enumerate(order):
        if n + lead < len(order):
            pending.append(scores(order[n + lead]))
        outputs(qi, pending.pop(0))


def _attention(q, k, v, bsz, seq):
    t = q.shape[0]
    group_qk = ATTN_HEADS * HEAD_PAD
    group_v = ATTN_HEADS * MLA_V_DIM
    blk = lambda b, g: (b, g)
    return pl.pallas_call(
        functools.partial(_attn_kernel, seq=seq, tq=ATTN_TQ),
        grid=(bsz, MLA_HEADS // ATTN_HEADS),
        in_specs=[pl.BlockSpec((seq, group_qk), blk), pl.BlockSpec((seq, group_qk), blk),
                  pl.BlockSpec((seq, group_v), blk)],
        out_specs=pl.BlockSpec((seq, group_v), blk),
        out_shape=jax.ShapeDtypeStruct((t, MLA_HEADS * MLA_V_DIM), BF16),
        scratch_shapes=[pltpu.VMEM((ATTN_HEADS // 2, seq, 4 * MLA_V_DIM), BF16),
                        pltpu.VMEM((ATTN_HEADS * ATTN_DEPTH, ATTN_TQ, seq), F32)],
        compiler_params=_params(2),
        name="mla_attn",
    )(q, k, v)


def _log_sigmoid(x):
    return jnp.minimum(x, 0.0) - jnp.log1p(jnp.exp(-jnp.abs(x)))


def _chunk_scan(x, lane_in_chunk, chunk, combine, fill):
    step = 1
    while step < chunk:
        x = combine(x, jnp.where(lane_in_chunk >= step, pltpu.roll(x, step, axis=1), fill))
        step *= 2
    return x


def _mlstm_kernel(xc_ref, vm_ref, og_ref, if_ref, wq_ref, wk_ref, wkt_ref, hn_ref, o_ref,
                  q_scr, k_scr, kt_scr, vaug_scr, ct_scr, b_scr, src_scr, mi_scr, *, seq, chunk):
    dh = MLSTM_HEAD_DIM
    nh = MLSTM_HEADS
    nc = seq // chunk
    ones_col = jnp.where(lax.broadcasted_iota(jnp.int32, (seq, dh), 1) == 0, 1.0, 0.0).astype(BF16)
    for h in range(nh):
        hs = slice(h * dh, (h + 1) * dh)
        xc = xc_ref[:, hs]
        q_scr[h] = _dot(xc, wq_ref[h]).astype(BF16)
        k_scr[h] = (_dot(xc, wk_ref[h]) * (dh ** -0.5)).astype(BF16)
        kt = lax.dot_general(wkt_ref[h], xc, (((1,), (1,)), ((), ())),
                             preferred_element_type=F32) * (dh ** -0.5)
        for c in range(nc):
            kt_scr[h * nc + c] = kt[:, c * chunk:(c + 1) * chunk]
        vaug_scr[h, :, 0:dh] = vm_ref[:, hs]
        vaug_scr[h, :, dh:2 * dh] = ones_col
        ct_scr[h] = jnp.zeros((dh, 2 * dh), F32)

    gates_t = if_ref[...]
    lane_in_chunk = lax.broadcasted_iota(jnp.int32, (GATE_ROWS, seq), 1) % chunk
    b_all = _chunk_scan(_log_sigmoid(gates_t) * LOG2_E, lane_in_chunk, chunk, jnp.add, 0.0)
    b_all = pltpu.roll(b_all, nh, axis=0)
    src_all = gates_t * LOG2_E - b_all
    mi_all = b_all + _chunk_scan(src_all, lane_in_chunk, chunk, jnp.maximum, -jnp.inf)
    for c in range(nc):
        cs = slice(c * chunk, (c + 1) * chunk)
        b_scr[c] = b_all[:, cs]
        src_scr[c] = src_all[:, cs]
        mi_scr[c] = mi_all[:, cs]

    r_i = lax.broadcasted_iota(jnp.int32, (chunk, chunk), 0)
    c_i = lax.broadcasted_iota(jnp.int32, (chunk, chunk), 1)
    eye = r_i == c_i
    tril = c_i <= r_i
    last_lane = lax.broadcasted_iota(jnp.int32, (GATE_ROWS, chunk), 1) == chunk - 1

    def to_col(row):
        return jnp.sum(jnp.where(eye, row, 0.0), axis=1, keepdims=True)

    def chunk_step(c, m_prev):
        r0 = pl.multiple_of(c * chunk, chunk)
        b_c = b_scr[c]
        src_c = src_scr[c]
        g = jnp.sum(jnp.where(last_lane, b_c, 0.0), axis=1, keepdims=True)
        m_new = jnp.maximum(g + m_prev, jnp.max(g + src_c, axis=1, keepdims=True))
        decay = jnp.exp2(g + m_prev - m_new)
        e_rows = jnp.exp2(g + src_c - m_new)
        m_t = jnp.maximum(b_c + m_prev, mi_scr[c])
        u_rows = b_c - m_t
        nm_rows = jnp.exp2(-m_t)
        for h in range(nh):
            hs = slice(h * dh, (h + 1) * dh)
            row = slice(h, h + 1)
            u_col = to_col(u_rows[row])
            q_c = q_scr[h, pl.ds(r0, chunk), :]
            v_c = vaug_scr[h, pl.ds(r0, chunk), :]
            s_qk = lax.dot_general(q_c, k_scr[h, pl.ds(r0, chunk), :], (((1,), (1,)), ((), ())),
                                   preferred_element_type=F32)
            d_mat = jnp.exp2(jnp.where(tril, u_col + src_c[row], -jnp.inf))
            q_in = (q_c.astype(F32) * jnp.exp2(u_col + m_prev[row])).astype(BF16)
            ct = ct_scr[h]
            tot = _dot(q_in, ct.astype(BF16)) + _dot((d_mat * s_qk).astype(BF16), v_c)
            den = jnp.maximum(jnp.abs(tot[:, dh:dh + 1]), to_col(nm_rows[row]))
            hh = tot[:, 0:dh] * (1.0 / den) * og_ref[pl.ds(r0, chunk), hs].astype(F32)
            o_ref[pl.ds(r0, chunk), hs] = _rms(hh, hn_ref[:, hs]).astype(BF16)
            c_loc = _dot((kt_scr[h * nc + c] * e_rows[row]).astype(BF16), v_c)
            ct_scr[h] = decay[row] * ct + c_loc
        return m_new

    lax.fori_loop(0, nc, chunk_step, jnp.zeros((GATE_ROWS, 1), F32), unroll=2)


def _mlstm(xc, vm, og, ifg, wq, wk, head_norm, bsz, seq):
    t = xc.shape[0]
    dh = MLSTM_HEAD_DIM
    nh = MLSTM_HEADS
    nc = seq // MLSTM_CHUNK
    blk = lambda b: (b, 0)
    return pl.pallas_call(
        functools.partial(_mlstm_kernel, seq=seq, chunk=MLSTM_CHUNK),
        grid=(bsz,),
        in_specs=[pl.BlockSpec((seq, MLSTM_WIDTH), blk), pl.BlockSpec((seq, MLSTM_WIDTH), blk),
                  pl.BlockSpec((seq, MLSTM_WIDTH), blk),
                  pl.BlockSpec((GATE_ROWS, seq), lambda b: (0, b)),
                  _resident((nh, dh, dh)), _resident((nh, dh, dh)), _resident((nh, dh, dh)),
                  _resident((1, MLSTM_WIDTH))],
        out_specs=pl.BlockSpec((seq, MLSTM_WIDTH), blk),
        out_shape=jax.ShapeDtypeStruct((t, MLSTM_WIDTH), BF16),
        scratch_shapes=[pltpu.VMEM((nh, seq, dh), BF16),
                        pltpu.VMEM((nh, seq, dh), BF16),
                        pltpu.VMEM((nh * nc, dh, MLSTM_CHUNK), F32),
                        pltpu.VMEM((nh, seq, 2 * dh), BF16),
                        pltpu.VMEM((nh, dh, 2 * dh), F32),
                        pltpu.VMEM((nc, GATE_ROWS, MLSTM_CHUNK), F32),
                        pltpu.VMEM((nc, GATE_ROWS, MLSTM_CHUNK), F32),
                        pltpu.VMEM((nc, GATE_ROWS, MLSTM_CHUNK), F32)],
        compiler_params=_params(1),
        name="mlstm",
    )(xc, vm, og, ifg, wq, wk, jnp.swapaxes(wk, 1, 2), head_norm.reshape(1, -1))


def _pad_cols(w, lo, width):
    return jnp.pad(w, ((0, 0), (lo, width - lo - w.shape[1])))


def _w_in_moves(d_in):
    src_kpe = MLA_Q_RANK + MLA_KV_RANK
    src_xm = src_kpe + MLA_ROPE_DIM
    src_if = src_xm + 3 * MLSTM_WIDTH
    src_gate = src_if + 2 * MLSTM_HEADS
    return ((0, _C_QLAT, src_kpe),
            (src_kpe, _C_KPE + ROPE_LO, MLA_ROPE_DIM),
            (src_xm, _C_XM, 3 * MLSTM_WIDTH),
            (src_if, _C_IF, 2 * MLSTM_HEADS),
            (src_gate, _C_GATE, d_in - src_gate))


def _layout_w_in_kernel(w_ref, o_ref):
    o_ref[...] = jnp.zeros(o_ref.shape, BF16)
    for src, dst, width in _w_in_moves(w_ref.shape[1]):
        o_ref[:, dst:dst + width] = w_ref[:, src:src + width].astype(BF16)


def _layout_w_in(w_in, layer):
    _, rows, d_in = w_in.shape
    d_out = _C_GATE + (d_in - _w_in_moves(d_in)[-1][0])
    tr = rows // 4
    return pl.pallas_call(
        _layout_w_in_kernel,
        grid=(rows // tr,),
        in_specs=[pl.BlockSpec((None, tr, d_in), lambda i: (layer, i, 0))],
        out_specs=pl.BlockSpec((tr, d_out), lambda i: (i, 0)),
        out_shape=jax.ShapeDtypeStruct((rows, d_out), BF16),
        compiler_params=_params(1),
        name="layout_w_in",
    )(w_in)


def _layout_w_q_b(w_q_b):
    w = w_q_b.reshape(MLA_Q_RANK, MLA_HEADS, MLA_QK_DIM)
    w = jnp.pad(w, ((0, 0), (0, 0), (0, HEAD_PAD - MLA_QK_DIM)))
    return w.reshape(MLA_Q_RANK, MLA_HEADS * HEAD_PAD).astype(BF16)


def _layout_w_kv_b(w_kv_b):
    w = w_kv_b.reshape(MLA_KV_RANK, MLA_HEADS, MLA_NOPE_DIM + MLA_V_DIM)
    wk = jnp.pad(w[:, :, :MLA_NOPE_DIM], ((0, 0), (0, 0), (0, HEAD_PAD - MLA_NOPE_DIM)))
    wv = w[:, :, MLA_NOPE_DIM:]
    return (wk.reshape(MLA_KV_RANK, MLA_HEADS * HEAD_PAD).astype(BF16),
            wv.reshape(MLA_KV_RANK, MLA_HEADS * MLA_V_DIM).astype(BF16))


def kernel(x, c, positions, w_ada, b_ada, norm_ff1, ff1_w_gate, ff1_w_up, ff1_w_down, norm_mix, w_in,
           q_a_norm, w_q_b, kv_a_norm, w_kv_b, conv_w, conv_b, w_q_m, w_k_m, b_i, b_f, mlstm_norm,
           w_mla_out, w_mlstm_out, w_o, norm_ff2, ff2_w_gate, ff2_w_up, ff2_w_down, norm_final):
    bsz, seq, d = x.shape
    t = bsz * seq
    depth = w_ada.shape[0]
    h = x.reshape(t, d)
    pos = _pack_positions(positions.reshape(t), MIX_TM)
    for l in range(depth):
        mod3 = _adaln(c, w_ada[l], b_ada[l]).reshape(bsz, 1, N_MOD * d)
        h, u_mix = _ffn(h, mod3, 0, norm_ff1[l], ff1_w_gate[l].astype(BF16), ff1_w_up[l].astype(BF16),
                        ff1_w_down[l].astype(BF16), seq, next_norm_g=norm_mix[l], next_chunk0=3)
        wk_r, wv_r = _layout_w_kv_b(w_kv_b[l])
        bif = _pad_cols(jnp.concatenate([b_i[l], b_f[l]]).reshape(1, -1), 0, V7X_LANES)
        q, k, v, xc, vm, og, ifg, gates = _mix_in(
            u_mix, pos, _layout_w_in(w_in, l), q_a_norm[l], _layout_w_q_b(w_q_b[l]),
            kv_a_norm[l], wk_r, wv_r, bif, conv_w[l], conv_b[l], seq)
        o_a = _attention(q, k, v, bsz, seq)
        o_b = _mlstm(xc, vm, og, ifg, w_q_m[l].astype(BF16), w_k_m[l].astype(BF16), mlstm_norm[l],
                     bsz, seq)
        mixer = (o_a, o_b, gates, 5, w_mla_out[l].astype(BF16), w_mlstm_out[l].astype(BF16),
                 w_o[l].astype(BF16))
        final_g = norm_final if l == depth - 1 else None
        h = _ffn(h, mod3, 6, norm_ff2[l], ff2_w_gate[l].astype(BF16), ff2_w_up[l].astype(BF16),
                 ff2_w_down[l].astype(BF16), seq, mixer=mixer, final_g=final_g)[0]
    return h.reshape(bsz, seq, d)
```
